```python
import math
import jax, jax.numpy as jnp
from jax import lax
import numpy as np

D_MODEL = 4096
BATCH = 4
SEQ = 4096
DEPTH = 2

N_MIXERS = 2
N_S5_LAYERS = (DEPTH + 1) // 2
N_ATTN_LAYERS = DEPTH // 2

S5_GROUP = 16
S5_GROUPS = D_MODEL // S5_GROUP
S5_STATE = 64
DT_MIN = 1e-3
DT_MAX = 1e-1

HEAD_DIM = 64
N_Q_HEADS = D_MODEL // HEAD_DIM
N_KV_HEADS = 8
Q_PER_KV = N_Q_HEADS // N_KV_HEADS
WINDOW = 128
BLOCK = 128
ROPE_THETA = 10000.0
Q_WIDTH = N_Q_HEADS * HEAD_DIM
KV_WIDTH = N_KV_HEADS * HEAD_DIM

D_FF = -(-8 * D_MODEL // (3 * 256)) * 256

EPS = 1e-6

kernel_name = "s5_swa_sink_hybrid_sandwich"


def rms_norm(x, g):
    xf = x.astype(jnp.float32)
    y = xf * lax.rsqrt(jnp.mean(xf * xf, axis=-1, keepdims=True) + EPS)
    return (y * g.astype(jnp.float32)).astype(x.dtype)


def s5_mixer(h, lam_re, lam_im, log_step, b_re, b_im, c_re, c_im, d_skip,
             w_out1, b_out1, w_out2, b_out2):
    f32 = jnp.float32
    bsz, seq, _ = h.shape
    hf = h.astype(f32)
    u = hf.reshape(bsz, seq, S5_GROUPS, S5_GROUP).astype(jnp.complex64)
    lam = lax.complex(lam_re.astype(f32), lam_im.astype(f32))
    step = jnp.exp(log_step.astype(f32))[:, None]
    lam_bar = jnp.exp(lam * step)
    b = lax.complex(b_re.astype(f32), b_im.astype(f32))
    b_bar = ((lam_bar - 1.0) / lam)[..., None] * b
    bu = jnp.einsum('gpc,blgc->blgp', b_bar, u)
    a = jnp.broadcast_to(lam_bar, bu.shape)

    def combine(left, right):
        a_l, b_l = left
        a_r, b_r = right
        return a_r * a_l, a_r * b_l + b_r

    _, states = lax.associative_scan(combine, (a, bu), axis=1)
    c = lax.complex(c_re.astype(f32), c_im.astype(f32))
    y = jnp.einsum('gcp,blgp->blgc', c, states).real.reshape(bsz, seq, D_MODEL)
    y = y + d_skip.astype(f32) * hf
    g = jax.nn.gelu(y).astype(h.dtype)
    return (g @ w_out1 + b_out1) * jax.nn.sigmoid(g @ w_out2 + b_out2)


def rope(x, positions):
    half = HEAD_DIM // 2
    inv_freq = jnp.power(ROPE_THETA, -jnp.arange(half, dtype=jnp.float32) / half)
    ang = positions.astype(jnp.float32)[..., None] * inv_freq
    cos = jnp.cos(ang)[:, :, None, :]
    sin = jnp.sin(ang)[:, :, None, :]
    xf = x.astype(jnp.float32)
    x1, x2 = xf[..., :half], xf[..., half:]
    return jnp.concatenate([x1 * cos - x2 * sin, x2 * cos + x1 * sin], axis=-1).astype(x.dtype)


def swa_mixer(h, positions, w_qkv, b_qkv, w_o, b_o, sinks):
    bsz, seq, _ = h.shape
    nblk = seq // BLOCK
    qkv = h @ w_qkv + b_qkv
    q, k, v = jnp.split(qkv, [Q_WIDTH, Q_WIDTH + KV_WIDTH], axis=-1)
    q = rope(q.reshape(bsz, seq, N_Q_HEADS, HEAD_DIM), positions)
    k = rope(k.reshape(bsz, seq, N_KV_HEADS, HEAD_DIM), positions)
    v = v.reshape(bsz, seq, N_KV_HEADS, HEAD_DIM)
    qb = q.reshape(bsz, nblk, BLOCK, N_KV_HEADS, Q_PER_KV, HEAD_DIM)

    def band(t):
        tb = t.reshape(bsz, nblk, BLOCK, N_KV_HEADS, HEAD_DIM)
        prev = jnp.pad(tb[:, :-1], ((0, 0), (1, 0), (0, 0), (0, 0), (0, 0)))
        return jnp.concatenate([prev, tb], axis=2)

    kb, vb = band(k), band(v)
    scores = jnp.einsum('bnqkgd,bnskd->bnkgqs', qb, kb).astype(jnp.float32) * (HEAD_DIM ** -0.5)
    qi = jnp.arange(BLOCK)[:, None]
    si = jnp.arange(2 * BLOCK)[None, :]
    diff = BLOCK + qi - si
    band_ok = (diff >= 0) & (diff < WINDOW)
    blk = jnp.arange(nblk)[:, None, None]
    in_seq = (blk * BLOCK - BLOCK + si[None]) >= 0
    mask = band_ok[None] & in_seq
    scores = jnp.where(mask[None, :, None, None], scores, -jnp.inf)
    sink = jnp.broadcast_to(
        sinks.astype(jnp.float32).reshape(1, 1, N_KV_HEADS, Q_PER_KV, 1, 1),
        scores.shape[:-1] + (1,))
    probs = jax.nn.softmax(jnp.concatenate([scores, sink], axis=-1), axis=-1)[..., :-1]
    out = jnp.einsum('bnkgqs,bnskd->bnqkgd', probs.astype(v.dtype), vb)
    out = out.reshape(bsz, seq, Q_WIDTH)
    return out @ w_o + b_o


def swiglu(h, w_gate, w_up, w_down):
    return (jax.nn.silu(h @ w_gate) * (h @ w_up)) @ w_down


def setup_inputs(seed: int = 0) -> dict:
    key = jax.random.key(seed)
    ks = jax.random.split(key, 32)
    f32 = jnp.float32

    def nrm(k, shape, scale):
        return jax.random.normal(k, shape, f32) * scale

    x = jax.random.normal(ks[0], (BATCH, SEQ, D_MODEL), f32)
    offsets = jax.random.randint(ks[1], (BATCH, 1), 0, 1024, dtype=jnp.int32)
    positions = (offsets + jnp.arange(SEQ, dtype=jnp.int32)[None, :]).astype(jnp.int32)

    norm_pre_mix = 1.0 + nrm(ks[2], (DEPTH, D_MODEL), 0.02)
    norm_post_mix = 1.0 + nrm(ks[3], (DEPTH, D_MODEL), 0.02)
    norm_pre_ffn = 1.0 + nrm(ks[4], (DEPTH, D_MODEL), 0.02)
    norm_post_ffn = 1.0 + nrm(ks[5], (DEPTH, D_MODEL), 0.02)

    ns = N_S5_LAYERS
    n_idx = jnp.arange(S5_STATE, dtype=f32)
    s5_lam_re = -0.5 + nrm(ks[6], (ns, S5_GROUPS, S5_STATE), 0.01)
    s5_lam_im = math.pi * n_idx + nrm(ks[7], (ns, S5_GROUPS, S5_STATE), 0.01)
    s5_log_step = jax.random.uniform(ks[8], (ns, S5_GROUPS), f32,
                                     math.log(DT_MIN), math.log(DT_MAX))
    s5_b_re = nrm(ks[9], (ns, S5_GROUPS, S5_STATE, S5_GROUP), (2 * S5_GROUP) ** -0.5)
    s5_b_im = nrm(ks[10], (ns, S5_GROUPS, S5_STATE, S5_GROUP), (2 * S5_GROUP) ** -0.5)
    s5_c_re = nrm(ks[11], (ns, S5_GROUPS, S5_GROUP, S5_STATE), (2 * S5_STATE) ** -0.5)
    s5_c_im = nrm(ks[12], (ns, S5_GROUPS, S5_GROUP, S5_STATE), (2 * S5_STATE) ** -0.5)
    s5_d = nrm(ks[13], (ns, D_MODEL), 1.0)
    s5_w_out1 = nrm(ks[14], (ns, D_MODEL, D_MODEL), D_MODEL ** -0.5)
    s5_b_out1 = nrm(ks[15], (ns, D_MODEL), 0.01)
    s5_w_out2 = nrm(ks[16], (ns, D_MODEL, D_MODEL), D_MODEL ** -0.5)
    s5_b_out2 = nrm(ks[17], (ns, D_MODEL), 0.01)

    na = N_ATTN_LAYERS
    attn_w_qkv = nrm(ks[18], (na, D_MODEL, Q_WIDTH + 2 * KV_WIDTH), D_MODEL ** -0.5)
    attn_b_qkv = nrm(ks[19], (na, Q_WIDTH + 2 * KV_WIDTH), 0.01)
    attn_w_o = nrm(ks[20], (na, Q_WIDTH, D_MODEL), Q_WIDTH ** -0.5)
    attn_b_o = nrm(ks[21], (na, D_MODEL), 0.01)
    attn_sinks = nrm(ks[22], (na, N_Q_HEADS), 0.5)

    ffn_w_gate = nrm(ks[23], (DEPTH, D_MODEL, D_FF), D_MODEL ** -0.5)
    ffn_w_up = nrm(ks[24], (DEPTH, D_MODEL, D_FF), D_MODEL ** -0.5)
    ffn_w_down = nrm(ks[25], (DEPTH, D_FF, D_MODEL), D_FF ** -0.5)

    return {
        "x": x, "positions": positions,
        "norm_pre_mix": norm_pre_mix, "norm_post_mix": norm_post_mix,
        "norm_pre_ffn": norm_pre_ffn, "norm_post_ffn": norm_post_ffn,
        "s5_lam_re": s5_lam_re, "s5_lam_im": s5_lam_im, "s5_log_step": s5_log_step,
        "s5_b_re": s5_b_re, "s5_b_im": s5_b_im, "s5_c_re": s5_c_re, "s5_c_im": s5_c_im,
        "s5_d": s5_d, "s5_w_out1": s5_w_out1, "s5_b_out1": s5_b_out1,
        "s5_w_out2": s5_w_out2, "s5_b_out2": s5_b_out2,
        "attn_w_qkv": attn_w_qkv, "attn_b_qkv": attn_b_qkv,
        "attn_w_o": attn_w_o, "attn_b_o": attn_b_o, "attn_sinks": attn_sinks,
        "ffn_w_gate": ffn_w_gate, "ffn_w_up": ffn_w_up, "ffn_w_down": ffn_w_down,
    }


def reference(x, positions, norm_pre_mix, norm_post_mix, norm_pre_ffn, norm_post_ffn,
              s5_lam_re, s5_lam_im, s5_log_step, s5_b_re, s5_b_im, s5_c_re, s5_c_im,
              s5_d, s5_w_out1, s5_b_out1, s5_w_out2, s5_b_out2,
              attn_w_qkv, attn_b_qkv, attn_w_o, attn_b_o, attn_sinks,
              ffn_w_gate, ffn_w_up, ffn_w_down):
    h = x
    for i in range(DEPTH):
        j = i // N_MIXERS
        hn = rms_norm(h, norm_pre_mix[i])
        if i % N_MIXERS == 0:
            m = s5_mixer(hn, s5_lam_re[j], s5_lam_im[j], s5_log_step[j],
                         s5_b_re[j], s5_b_im[j], s5_c_re[j], s5_c_im[j], s5_d[j],
                         s5_w_out1[j], s5_b_out1[j], s5_w_out2[j], s5_b_out2[j])
        else:
            m = swa_mixer(hn, positions, attn_w_qkv[j], attn_b_qkv[j],
                          attn_w_o[j], attn_b_o[j], attn_sinks[j])
        h = h + rms_norm(m, norm_post_mix[i])
        f = swiglu(rms_norm(h, norm_pre_ffn[i]), ffn_w_gate[i], ffn_w_up[i], ffn_w_down[i])
        h = h + rms_norm(f, norm_post_ffn[i])
    return h
```

```python
import functools

import jax
import jax.numpy as jnp
from jax import lax
from jax.experimental import pallas as pl
from jax.experimental.pallas import tpu as pltpu

F32 = jnp.float32
BF16 = jnp.bfloat16

EPS = 1e-6
LANES = 128
S5_GROUP = 16
GROUPS_PER_TILE = LANES // S5_GROUP
S5_CHUNK = 16
HEAD_DIM = 64
N_KV_HEADS = 8
ATTN_BLOCK = 128
ROPE_THETA = 10000.0
VMEM_LIMIT = 60 * 1024 * 1024


def _params(n_axes, vmem=VMEM_LIMIT):
    return pltpu.CompilerParams(
        dimension_semantics=("arbitrary",) * n_axes, vmem_limit_bytes=vmem)


def _rms(x, g):
    r = lax.rsqrt(jnp.mean(x * x, axis=-1, keepdims=True) + EPS)
    return x * r * g


def _rmsnorm_body(x_ref, g_ref, o_ref):
    o_ref[...] = _rms(x_ref[...], g_ref[...]).astype(o_ref.dtype)


def rmsnorm(x, g, out_dtype, rows=256):
    t, d = x.shape
    return pl.pallas_call(
        _rmsnorm_body,
        grid=(t // rows,),
        in_specs=[pl.BlockSpec((rows, d), lambda i: (i, 0)),
                  pl.BlockSpec((1, d), lambda i: (0, 0))],
        out_specs=pl.BlockSpec((rows, d), lambda i: (i, 0)),
        out_shape=jax.ShapeDtypeStruct((t, d), out_dtype),
        compiler_params=_params(1),
        name="rmsnorm",
    )(x, g.reshape(1, d))


def _resid_norm_body(h_ref, m_ref, gpost_ref, gpre_ref, hout_ref, hn_ref):
    h = h_ref[...] + _rms(m_ref[...].astype(F32), gpost_ref[...])
    hout_ref[...] = h
    hn_ref[...] = _rms(h, gpre_ref[...]).astype(hn_ref.dtype)


def resid_norm(h, m, g_post, g_pre, rows=256):
    t, d = h.shape
    row_spec = pl.BlockSpec((rows, d), lambda i: (i, 0))
    vec_spec = pl.BlockSpec((1, d), lambda i: (0, 0))
    return pl.pallas_call(
        _resid_norm_body,
        grid=(t // rows,),
        in_specs=[row_spec, row_spec, vec_spec, vec_spec],
        out_specs=[row_spec, row_spec],
        out_shape=[jax.ShapeDtypeStruct((t, d), F32),
                   jax.ShapeDtypeStruct((t, d), BF16)],
        compiler_params=_params(1),
        name="resid_norm",
    )(h, m, g_post.reshape(1, d), g_pre.reshape(1, d))


def _resid_body(h_ref, m_ref, gpost_ref, hout_ref):
    hout_ref[...] = h_ref[...] + _rms(m_ref[...].astype(F32), gpost_ref[...])


def resid(h, m, g_post, rows=256):
    t, d = h.shape
    row_spec = pl.BlockSpec((rows, d), lambda i: (i, 0))
    return pl.pallas_call(
        _resid_body,
        grid=(t // rows,),
        in_specs=[row_spec, row_spec, pl.BlockSpec((1, d), lambda i: (0, 0))],
        out_specs=row_spec,
        out_shape=jax.ShapeDtypeStruct((t, d), F32),
        compiler_params=_params(1),
        name="resid",
    )(h, m, g_post.reshape(1, d))


def _mm_bias_body(x_ref, w_ref, b_ref, o_ref):
    acc = jnp.dot(x_ref[...], w_ref[...], preferred_element_type=F32)
    o_ref[...] = (acc + b_ref[...]).astype(o_ref.dtype)


def _mm_body(x_ref, w_ref, o_ref):
    o_ref[...] = jnp.dot(x_ref[...], w_ref[...],
                         preferred_element_type=F32).astype(o_ref.dtype)


def matmul(x, w, b=None, *, bm, bn, out_dtype=BF16, name="matmul"):
    m, k = x.shape
    n = w.shape[1]
    in_specs = [pl.BlockSpec((bm, k), lambda i, j: (i, 0)),
                pl.BlockSpec((k, bn), lambda i, j: (0, j))]
    args = [x, w]
    body = _mm_body
    if b is not None:
        in_specs.append(pl.BlockSpec((1, bn), lambda i, j: (0, j)))
        args.append(b.reshape(1, n).astype(F32))
        body = _mm_bias_body
    return pl.pallas_call(
        body,
        grid=(m // bm, n // bn),
        in_specs=in_specs,
        out_specs=pl.BlockSpec((bm, bn), lambda i, j: (i, j)),
        out_shape=jax.ShapeDtypeStruct((m, n), out_dtype),
        compiler_params=_params(2),
        name=name,
    )(*args)


def _glu_body(x_ref, w1_ref, b1_ref, w2_ref, b2_ref, o_ref):
    x = x_ref[...]
    a = jnp.dot(x, w1_ref[...], preferred_element_type=F32) + b1_ref[...]
    b = jnp.dot(x, w2_ref[...], preferred_element_type=F32) + b2_ref[...]
    o_ref[...] = (a * jax.nn.sigmoid(b)).astype(o_ref.dtype)


def glu_matmul(x, w1, b1, w2, b2, *, bm, bn):
    m, k = x.shape
    n = w1.shape[1]
    w_spec = pl.BlockSpec((k, bn), lambda i, j: (0, j))
    b_spec = pl.BlockSpec((1, bn), lambda i, j: (0, j))
    return pl.pallas_call(
        _glu_body,
        grid=(m // bm, n // bn),
        in_specs=[pl.BlockSpec((bm, k), lambda i, j: (i, 0)),
                  w_spec, b_spec, w_spec, b_spec],
        out_specs=pl.BlockSpec((bm, bn), lambda i, j: (i, j)),
        out_shape=jax.ShapeDtypeStruct((m, n), BF16),
        compiler_params=_params(2),
        name="glu_matmul",
    )(x, w1, b1.reshape(1, n), w2, b2.reshape(1, n))


def _swiglu_up_body(x_ref, wg_ref, wu_ref, o_ref):
    x = x_ref[...]
    a = jnp.dot(x, wg_ref[...], preferred_element_type=F32)
    b = jnp.dot(x, wu_ref[...], preferred_element_type=F32)
    o_ref[...] = (jax.nn.silu(a) * b).astype(o_ref.dtype)


def swiglu_up(x, wg, wu, *, bm, bn):
    m, k = x.shape
    n = wg.shape[1]
    w_spec = pl.BlockSpec((k, bn), lambda i, j: (0, j))
    return pl.pallas_call(
        _swiglu_up_body,
        grid=(m // bm, n // bn),
        in_specs=[pl.BlockSpec((bm, k), lambda i, j: (i, 0)), w_spec, w_spec],
        out_specs=pl.BlockSpec((bm, bn), lambda i, j: (i, j)),
        out_shape=jax.ShapeDtypeStruct((m, n), BF16),
        compiler_params=_params(2),
        name="swiglu_up",
    )(x, wg, wu)


def _s5_tables(lam_re, lam_im, log_step, b_re, b_im, c_re, c_im, lc):
    g, p = lam_re.shape
    ch = b_re.shape[-1]
    nt = g // GROUPS_PER_TILE
    hi = lax.Precision.HIGHEST
    lam = lax.complex(lam_re.astype(F32), lam_im.astype(F32))
    step = jnp.exp(log_step.astype(F32))[:, None]
    lam_bar = jnp.exp(lam * step)
    b_bar = ((lam_bar - 1.0) / lam)[..., None] * lax.complex(b_re.astype(F32), b_im.astype(F32))
    taus = jnp.arange(lc + 1, dtype=F32)[:, None, None]
    pw = jnp.exp((lam * step)[None] * taus)
    pw_re, pw_im = jnp.real(pw), jnp.imag(pw)
    bb_re, bb_im = jnp.real(b_bar), jnp.imag(b_bar)
    cr, ci = c_re.astype(F32), c_im.astype(F32)
    w_re = pw_re[:lc, :, :, None] * bb_re[None] - pw_im[:lc, :, :, None] * bb_im[None]
    w_im = pw_re[:lc, :, :, None] * bb_im[None] + pw_im[:lc, :, :, None] * bb_re[None]
    k = (jnp.einsum('gop,tgpc->tgoc', cr, w_re, precision=hi)
         - jnp.einsum('gop,tgpc->tgoc', ci, w_im, precision=hi))
    eye = jnp.eye(GROUPS_PER_TILE, dtype=F32)
    k5 = k.reshape(lc, nt, GROUPS_PER_TILE, ch, ch)
    kb = jnp.einsum('tjaoc,ab->jtacbo', k5, eye).reshape(nt, lc, LANES, LANES)

    def expand_ws(w):
        wf = w[::-1].reshape(lc, nt, GROUPS_PER_TILE, p, ch)
        return jnp.einsum('sjapc,ab->jsacbp', wf, eye).reshape(
            nt, lc * LANES, GROUPS_PER_TILE * p)

    ws = jnp.concatenate([expand_ws(w_re), expand_ws(w_im)], axis=-1)
    v_re = cr[None] * pw_re[1:, :, None, :] - ci[None] * pw_im[1:, :, None, :]
    v_im = cr[None] * pw_im[1:, :, None, :] + ci[None] * pw_re[1:, :, None, :]

    def expand_wy(v):
        vf = v.reshape(lc, nt, GROUPS_PER_TILE, ch, p)
        return jnp.einsum('tjaop,ab->japtbo', vf, eye).reshape(
            nt, GROUPS_PER_TILE * p, lc * LANES)

    wy = jnp.concatenate([expand_wy(v_re), expand_wy(-v_im)], axis=1)
    a_re = pw_re[lc].reshape(nt, 1, GROUPS_PER_TILE * p)
    a_im = pw_im[lc].reshape(nt, 1, GROUPS_PER_TILE * p)
    return kb.astype(BF16), ws.astype(BF16), wy.astype(BF16), a_re, a_im


def _s5_body(x_ref, kb_ref, ws_ref, wy_ref, are_ref, aim_ref, d_ref, o_ref,
             m_ref, lhs_ref, s_ref, xp_ref, stage_ref, carry_ref, *, lc, nb, nc, ns):
    @pl.when(pl.program_id(1) == 0)
    def _():
        zero = jnp.zeros((LANES, LANES), BF16)
        for s in range(lc):
            for t in range(lc):
                m_ref[s * LANES:(s + 1) * LANES, t * LANES:(t + 1) * LANES] = (
                    kb_ref[0, t - s] if t >= s else zero)
        carry_ref[...] = jnp.zeros_like(carry_ref)

    for b in range(nb):
        for s in range(lc):
            piece = x_ref[b, pl.ds(s, nc, stride=lc), :]
            lhs_ref[b * nc:(b + 1) * nc, s * LANES:(s + 1) * LANES] = piece.astype(BF16)
    lhs = lhs_ref[...]
    rows = nb * nc
    nk = ns // LANES
    s_all = jnp.dot(lhs, ws_ref[0], preferred_element_type=F32)
    for k in range(2 * nk):
        s_ref[k * rows:(k + 1) * rows, :] = s_all[:, k * LANES:(k + 1) * LANES]

    ar = [are_ref[0, :, k * LANES:(k + 1) * LANES] for k in range(nk)]
    ai = [aim_ref[0, :, k * LANES:(k + 1) * LANES] for k in range(nk)]

    def step(c, carry):
        new = []
        for k in range(nk):
            xr, xi = carry[2 * k], carry[2 * k + 1]
            at_re = pl.ds(k * rows + c, nb, stride=nc)
            at_im = pl.ds((nk + k) * rows + c, nb, stride=nc)
            xp_ref[at_re, :] = xr
            xp_ref[at_im, :] = xi
            new.append(ar[k] * xr - ai[k] * xi + s_ref[at_re, :])
            new.append(ar[k] * xi + ai[k] * xr + s_ref[at_im, :])
        return tuple(new)

    carry = lax.fori_loop(0, nc, step,
                          tuple(carry_ref[k, 0:nb, :] for k in range(2 * nk)))
    for k in range(2 * nk):
        carry_ref[k, 0:nb, :] = carry[k]

    xp = jnp.concatenate([xp_ref[k * rows:(k + 1) * rows, :] for k in range(2 * nk)], axis=-1)
    y = (jnp.dot(lhs, m_ref[...], preferred_element_type=F32)
         + jnp.dot(xp.astype(BF16), wy_ref[0], preferred_element_type=F32))
    for b in range(nb):
        for t in range(lc):
            stage_ref[b, pl.ds(t, nc, stride=lc), :] = (
                y[b * nc:(b + 1) * nc, t * LANES:(t + 1) * LANES])
    yy = stage_ref[...] + d_ref[...] * x_ref[...]
    o_ref[...] = jax.nn.gelu(yy).astype(o_ref.dtype)


def s5_scan(hn, tables, d_skip, *, tc, lc=S5_CHUNK):
    kb, ws, wy, a_re, a_im = tables
    nb, l, d = hn.shape
    nt = d // LANES
    nc = tc // lc
    ns = a_re.shape[-1]
    body = functools.partial(_s5_body, lc=lc, nb=nb, nc=nc, ns=ns)
    tile = lambda j, i: (j, 0, 0)
    return pl.pallas_call(
        body,
        grid=(nt, l // tc),
        in_specs=[pl.BlockSpec((nb, tc, LANES), lambda j, i: (0, i, j)),
                  pl.BlockSpec((1, lc, LANES, LANES), lambda j, i: (j, 0, 0, 0)),
                  pl.BlockSpec((1, lc * LANES, 2 * ns), tile),
                  pl.BlockSpec((1, 2 * ns, lc * LANES), tile),
                  pl.BlockSpec((1, 1, ns), tile),
                  pl.BlockSpec((1, 1, ns), tile),
                  pl.BlockSpec((1, 1, LANES), lambda j, i: (0, 0, j))],
        out_specs=pl.BlockSpec((nb, tc, LANES), lambda j, i: (0, i, j)),
        out_shape=jax.ShapeDtypeStruct((nb, l, d), BF16),
        scratch_shapes=[pltpu.VMEM((lc * LANES, lc * LANES), BF16),
                        pltpu.VMEM((nb * nc, lc * LANES), BF16),
                        pltpu.VMEM((2 * ns // LANES * nb * nc, LANES), F32),
                        pltpu.VMEM((2 * ns // LANES * nb * nc, LANES), F32),
                        pltpu.VMEM((nb, tc, LANES), F32),
                        pltpu.VMEM((2 * ns // LANES, 8, LANES), F32)],
        compiler_params=_params(2),
        name="s5_scan",
    )(hn, kb, ws, wy, a_re, a_im, d_skip.reshape(1, 1, d).astype(F32))


def _rope(x, cos, sin_signed, first_half):
    w = x.shape[1]
    reps = w // LANES
    cosw = jnp.tile(cos, (1, reps))
    sinw = jnp.tile(sin_signed, (1, reps))
    sel = jnp.tile(first_half, (1, reps))
    half = HEAD_DIM // 2
    swapped = jnp.where(sel, pltpu.roll(x, w - half, 1), pltpu.roll(x, half, 1))
    return x * cosw + swapped * sinw


def _attn_body(sink_ref, posc_ref, posp_ref, invf_ref, q_ref, kc_ref, kp_ref, vc_ref, vp_ref,
               o_ref, *, q_per_kv):
    n = pl.program_id(1)
    blk = ATTN_BLOCK
    lane = lax.broadcasted_iota(jnp.int32, (blk, LANES), 1)
    first_half = (lane % HEAD_DIM) < (HEAD_DIM // 2)

    def tables(pos_ref):
        ang = pos_ref[...].astype(F32) * invf_ref[...]
        sin = jnp.sin(ang)
        return jnp.cos(ang), jnp.where(first_half, -sin, sin)

    cos_c, sin_c = tables(posc_ref)
    cos_p, sin_p = tables(posp_ref)
    scale = HEAD_DIM ** -0.5
    q = _rope(q_ref[...].astype(F32), cos_c, sin_c, first_half)
    q = (q * scale).astype(BF16)
    kc = _rope(kc_ref[...].astype(F32), cos_c, sin_c, first_half).astype(BF16)
    kp = _rope(kp_ref[...].astype(F32), cos_p, sin_p, first_half).astype(BF16)
    k = jnp.concatenate([kp, kc], axis=0)
    v = jnp.concatenate([vp_ref[...], vc_ref[...]], axis=0)

    rows = q_per_kv * blk
    qi = lax.broadcasted_iota(jnp.int32, (rows, 2 * blk), 0) % blk
    si = lax.broadcasted_iota(jnp.int32, (rows, 2 * blk), 1)
    diff = blk + qi - si
    mask = (diff >= 0) & (diff < blk) & ((si >= blk) | (n > 0))

    for h in range(N_KV_HEADS):
        kh = k[:, h * HEAD_DIM:(h + 1) * HEAD_DIM]
        vh = v[:, h * HEAD_DIM:(h + 1) * HEAD_DIM]
        heads = [h * q_per_kv + j for j in range(q_per_kv)]
        qs = jnp.concatenate([q[:, a * HEAD_DIM:(a + 1) * HEAD_DIM] for a in heads], axis=0)
        sink = jnp.concatenate([jnp.full((blk, 1), sink_ref[a], F32) for a in heads], axis=0)
        s = lax.dot_general(qs, kh, (((1,), (1,)), ((), ())), preferred_element_type=F32)
        s = jnp.where(mask, s, -jnp.inf)
        m = jnp.maximum(jnp.max(s, axis=-1, keepdims=True), sink)
        e = jnp.exp(s - m)
        denom = jnp.sum(e, axis=-1, keepdims=True) + jnp.exp(sink - m)
        p = (e / denom).astype(BF16)
        o = jnp.dot(p, vh, preferred_element_type=F32)
        for j, a in enumerate(heads):
            o_ref[:, a * HEAD_DIM:(a + 1) * HEAD_DIM] = o[j * blk:(j + 1) * blk].astype(o_ref.dtype)


def swa_attention(qkv, positions, sinks, *, batch, seq):
    t, width = qkv.shape
    kvw = N_KV_HEADS * HEAD_DIM
    qw = width - 2 * kvw
    n_q_heads = qw // HEAD_DIM
    blk = ATTN_BLOCK
    nblk = seq // blk
    half = HEAD_DIM // 2
    inv_freq = jnp.power(ROPE_THETA, -jnp.arange(half, dtype=F32) / half)
    invf = jnp.tile(inv_freq, LANES // half).reshape(1, LANES)
    pos = positions.reshape(t, 1).astype(jnp.int32)
    cur = lambda b, n: (b * nblk + n, 0)
    prev = lambda b, n: (b * nblk + jnp.maximum(n - 1, 0), 0)
    kcol = qw // kvw
    body = functools.partial(_attn_body, q_per_kv=n_q_heads // N_KV_HEADS)
    return pl.pallas_call(
        body,
        grid=(batch, nblk),
        in_specs=[pl.BlockSpec(memory_space=pltpu.SMEM),
                  pl.BlockSpec((blk, 1), cur),
                  pl.BlockSpec((blk, 1), prev),
                  pl.BlockSpec((1, LANES), lambda b, n: (0, 0)),
                  pl.BlockSpec((blk, qw), cur),
                  pl.BlockSpec((blk, kvw), lambda b, n: (b * nblk + n, kcol)),
                  pl.BlockSpec((blk, kvw), lambda b, n: (b * nblk + jnp.maximum(n - 1, 0), kcol)),
                  pl.BlockSpec((blk, kvw), lambda b, n: (b * nblk + n, kcol + 1)),
                  pl.BlockSpec((blk, kvw), lambda b, n: (b * nblk + jnp.maximum(n - 1, 0), kcol + 1))],
        out_specs=pl.BlockSpec((blk, qw), cur),
        out_shape=jax.ShapeDtypeStruct((t, qw), BF16),
        compiler_params=_params(2),
        name="swa_attention",
    )(sinks.astype(F32), pos, pos, invf, qkv, qkv, qkv, qkv, qkv)


def _ffn(hn, w_gate, w_up, w_down):
    act = swiglu_up(hn, w_gate.astype(BF16), w_up.astype(BF16), bm=1024, bn=256)
    return matmul(act, w_down.astype(BF16), bm=512, bn=512, name="ffn_down")


def kernel(x, positions, norm_pre_mix, norm_post_mix, norm_pre_ffn, norm_post_ffn, s5_lam_re, s5_lam_im, s5_log_step, s5_b_re, s5_b_im, s5_c_re, s5_c_im, s5_d, s5_w_out1, s5_b_out1, s5_w_out2, s5_b_out2, attn_w_qkv, attn_b_qkv, attn_w_o, attn_b_o, attn_sinks, ffn_w_gate, ffn_w_up, ffn_w_down):
    bsz, seq, d = x.shape
    t = bsz * seq
    h = x.reshape(t, d)

    hn = rmsnorm(h, norm_pre_mix[0], F32)
    tables = _s5_tables(s5_lam_re[0], s5_lam_im[0], s5_log_step[0], s5_b_re[0], s5_b_im[0],
                        s5_c_re[0], s5_c_im[0], S5_CHUNK)
    g = s5_scan(hn.reshape(bsz, seq, d), tables, s5_d[0], tc=1024).reshape(t, d)
    m = glu_matmul(g, s5_w_out1[0].astype(BF16), s5_b_out1[0], s5_w_out2[0].astype(BF16),
                   s5_b_out2[0], bm=1024, bn=512)
    h, hn = resid_norm(h, m, norm_post_mix[0], norm_pre_ffn[0])
    f = _ffn(hn, ffn_w_gate[0], ffn_w_up[0], ffn_w_down[0])
    h, hn = resid_norm(h, f, norm_post_ffn[0], norm_pre_mix[1])

    qkv = matmul(hn, attn_w_qkv[0].astype(BF16), attn_b_qkv[0], bm=1024, bn=1024, name="qkv_proj")
    att = swa_attention(qkv, positions, attn_sinks[0], batch=bsz, seq=seq)
    m = matmul(att, attn_w_o[0].astype(BF16), attn_b_o[0], bm=1024, bn=1024, name="o_proj")
    h, hn = resid_norm(h, m, norm_post_mix[1], norm_pre_ffn[1])
    f = _ffn(hn, ffn_w_gate[1], ffn_w_up[1], ffn_w_down[1])
    h = resid(h, f, norm_post_ffn[1])
    return h.reshape(bsz, seq, d)
```

```python
import functools
import math

import jax
import jax.numpy as jnp
from jax import lax
from jax.experimental import pallas as pl
from jax.experimental.pallas import tpu as pltpu

F32 = jnp.float32
BF16 = jnp.bfloat16

EPS = 1e-6
LANES = 128
S5_GROUP = 16
GROUPS_PER_TILE = LANES // S5_GROUP
S5_CHUNK = 16
HEAD_DIM = 64
N_KV_HEADS = 8
ATTN_BLOCK = 128
ROPE_THETA = 10000.0
LOG2E = math.log2(math.e)
VMEM_LIMIT = 60 * 1024 * 1024


def _params(n_axes, vmem=VMEM_LIMIT):
    return pltpu.CompilerParams(
        dimension_semantics=("arbitrary",) * n_axes, vmem_limit_bytes=vmem)


def _rms(x, g):
    r = lax.rsqrt(jnp.mean(x * x, axis=-1, keepdims=True) + EPS)
    return x * r * g


def _rmsnorm_body(x_ref, g_ref, o_ref):
    o_ref[...] = _rms(x_ref[...], g_ref[...]).astype(o_ref.dtype)


def rmsnorm(x, g, out_dtype, rows=256):
    t, d = x.shape
    return pl.pallas_call(
        _rmsnorm_body,
        grid=(t // rows,),
        in_specs=[pl.BlockSpec((rows, d), lambda i: (i, 0)),
                  pl.BlockSpec((1, d), lambda i: (0, 0))],
        out_specs=pl.BlockSpec((rows, d), lambda i: (i, 0)),
        out_shape=jax.ShapeDtypeStruct((t, d), out_dtype),
        compiler_params=_params(1),
        name="rmsnorm",
    )(x, g.reshape(1, d))


def _resid_norm_body(h_ref, m_ref, gpost_ref, gpre_ref, hout_ref, hn_ref):
    h = h_ref[...] + _rms(m_ref[...].astype(F32), gpost_ref[...])
    hout_ref[...] = h
    hn_ref[...] = _rms(h, gpre_ref[...]).astype(hn_ref.dtype)


def resid_norm(h, m, g_post, g_pre, rows=256):
    t, d = h.shape
    row_spec = pl.BlockSpec((rows, d), lambda i: (i, 0))
    vec_spec = pl.BlockSpec((1, d), lambda i: (0, 0))
    return pl.pallas_call(
        _resid_norm_body,
        grid=(t // rows,),
        in_specs=[row_spec, row_spec, vec_spec, vec_spec],
        out_specs=[row_spec, row_spec],
        out_shape=[jax.ShapeDtypeStruct((t, d), F32),
                   jax.ShapeDtypeStruct((t, d), BF16)],
        compiler_params=_params(1),
        name="resid_norm",
    )(h, m, g_post.reshape(1, d), g_pre.reshape(1, d))


def _resid_body(h_ref, m_ref, gpost_ref, hout_ref):
    hout_ref[...] = h_ref[...] + _rms(m_ref[...].astype(F32), gpost_ref[...])


def resid(h, m, g_post, rows=256):
    t, d = h.shape
    row_spec = pl.BlockSpec((rows, d), lambda i: (i, 0))
    return pl.pallas_call(
        _resid_body,
        grid=(t // rows,),
        in_specs=[row_spec, row_spec, pl.BlockSpec((1, d), lambda i: (0, 0))],
        out_specs=row_spec,
        out_shape=jax.ShapeDtypeStruct((t, d), F32),
        compiler_params=_params(1),
        name="resid",
    )(h, m, g_post.reshape(1, d))


def _mm_bias_body(x_ref, w_ref, b_ref, o_ref):
    acc = jnp.dot(x_ref[...], w_ref[...].astype(BF16), preferred_element_type=F32)
    o_ref[...] = (acc + b_ref[...]).astype(o_ref.dtype)


def _mm_body(x_ref, w_ref, o_ref):
    o_ref[...] = jnp.dot(x_ref[...], w_ref[...].astype(BF16),
                         preferred_element_type=F32).astype(o_ref.dtype)


def _layer_spec(rows, bn, layer):
    return pl.BlockSpec((None, rows, bn), lambda i, j: (layer, 0, j))


def matmul(x, w, b=None, *, layer, bm, bn, out_dtype=BF16, name="matmul"):
    m, k = x.shape
    n = w.shape[2]
    in_specs = [pl.BlockSpec((bm, k), lambda i, j: (i, 0)), _layer_spec(k, bn, layer)]
    args = [x, w]
    body = _mm_body
    if b is not None:
        in_specs.append(_layer_spec(1, bn, layer))
        args.append(b.reshape(b.shape[0], 1, n))
        body = _mm_bias_body
    return pl.pallas_call(
        body,
        grid=(m // bm, n // bn),
        in_specs=in_specs,
        out_specs=pl.BlockSpec((bm, bn), lambda i, j: (i, j)),
        out_shape=jax.ShapeDtypeStruct((m, n), out_dtype),
        compiler_params=_params(2),
        name=name,
    )(*args)


def _glu_body(x_ref, w1_ref, b1_ref, w2_ref, b2_ref, o_ref):
    x = x_ref[...]
    a = jnp.dot(x, w1_ref[...].astype(BF16), preferred_element_type=F32) + b1_ref[...]
    b = jnp.dot(x, w2_ref[...].astype(BF16), preferred_element_type=F32) + b2_ref[...]
    o_ref[...] = (a * jax.nn.sigmoid(b)).astype(o_ref.dtype)


def glu_matmul(x, w1, b1, w2, b2, *, layer, bm, bn):
    m, k = x.shape
    n = w1.shape[2]
    w_spec = _layer_spec(k, bn, layer)
    b_spec = _layer_spec(1, bn, layer)
    return pl.pallas_call(
        _glu_body,
        grid=(m // bm, n // bn),
        in_specs=[pl.BlockSpec((bm, k), lambda i, j: (i, 0)),
                  w_spec, b_spec, w_spec, b_spec],
        out_specs=pl.BlockSpec((bm, bn), lambda i, j: (i, j)),
        out_shape=jax.ShapeDtypeStruct((m, n), BF16),
        compiler_params=_params(2),
        name="glu_matmul",
    )(x, w1, b1.reshape(b1.shape[0], 1, n), w2, b2.reshape(b2.shape[0], 1, n))


def _swiglu_up_body(x_ref, wg_ref, wu_ref, o_ref):
    x = x_ref[...]
    a = jnp.dot(x, wg_ref[...].astype(BF16), preferred_element_type=F32)
    b = jnp.dot(x, wu_ref[...].astype(BF16), preferred_element_type=F32)
    o_ref[...] = (jax.nn.silu(a) * b).astype(o_ref.dtype)


def swiglu_up(x, wg, wu, *, layer, bm, bn):
    m, k = x.shape
    n = wg.shape[2]
    w_spec = _layer_spec(k, bn, layer)
    return pl.pallas_call(
        _swiglu_up_body,
        grid=(m // bm, n // bn),
        in_specs=[pl.BlockSpec((bm, k), lambda i, j: (i, 0)), w_spec, w_spec],
        out_specs=pl.BlockSpec((bm, bn), lambda i, j: (i, j)),
        out_shape=jax.ShapeDtypeStruct((m, n), BF16),
        compiler_params=_params(2),
        name="swiglu_up",
    )(x, wg, wu)


def _s5_tables(lam_re, lam_im, log_step, b_re, b_im, c_re, c_im, lc):
    g, p = lam_re.shape
    ch = b_re.shape[-1]
    assert ch == S5_GROUP and 2 * p == LANES
    nt = g // GROUPS_PER_TILE
    hi = lax.Precision.HIGHEST
    lr, li = lam_re.astype(F32), lam_im.astype(F32)
    step = jnp.exp(log_step.astype(F32))[:, None]
    zr, zi = lr * step, li * step
    taus = jnp.arange(lc + 1, dtype=F32)[:, None, None]
    mag = jnp.exp(zr[None] * taus)
    pw_re = mag * jnp.cos(zi[None] * taus)
    pw_im = mag * jnp.sin(zi[None] * taus)
    nr, ni = pw_re[1] - 1.0, pw_im[1]
    den = lr * lr + li * li
    qr = (nr * lr + ni * li) / den
    qi = (ni * lr - nr * li) / den
    br, bi = b_re.astype(F32), b_im.astype(F32)
    bb_re = qr[..., None] * br - qi[..., None] * bi
    bb_im = qr[..., None] * bi + qi[..., None] * br
    cr, ci = c_re.astype(F32), c_im.astype(F32)
    w_re = pw_re[:lc, :, :, None] * bb_re[None] - pw_im[:lc, :, :, None] * bb_im[None]
    w_im = pw_re[:lc, :, :, None] * bb_im[None] + pw_im[:lc, :, :, None] * bb_re[None]
    k = (jnp.einsum('gop,tgpc->tgoc', cr, w_re, precision=hi)
         - jnp.einsum('gop,tgpc->tgoc', ci, w_im, precision=hi))
    eye = jnp.eye(GROUPS_PER_TILE, dtype=F32)
    k5 = k.reshape(lc, nt, GROUPS_PER_TILE, ch, ch)
    kb = jnp.einsum('tjaoc,ab->jtacbo', k5, eye).reshape(nt, lc, LANES, LANES)

    half = jax.nn.one_hot(jnp.arange(GROUPS_PER_TILE) % 2, 2, dtype=F32)

    def place(w):
        wf = w.reshape(lc, nt, GROUPS_PER_TILE, ch, p)
        out = wf[:, :, :, :, None, :] * half[None, None, :, None, :, None]
        return out.transpose(1, 0, 2, 3, 4, 5).reshape(nt, lc * LANES, LANES)

    ws_re = jnp.swapaxes(w_re[::-1], 2, 3)
    ws_im = jnp.swapaxes(w_im[::-1], 2, 3)
    wssc = jnp.concatenate([place(ws_re), place(ws_im)], axis=-1)
    v_re = cr[None] * pw_re[1:, :, None, :] - ci[None] * pw_im[1:, :, None, :]
    v_im = cr[None] * pw_im[1:, :, None, :] + ci[None] * pw_re[1:, :, None, :]
    wysc = jnp.concatenate([place(v_re), place(-v_im)], axis=-1)
    a_re = pw_re[lc].reshape(nt, 1, GROUPS_PER_TILE * p)
    a_im = pw_im[lc].reshape(nt, 1, GROUPS_PER_TILE * p)
    return kb.astype(BF16), wssc.astype(BF16), wysc.astype(BF16), a_re, a_im


def _s5_body(x_ref, kb_ref, wssc_ref, wysc_ref, are_ref, aim_ref, d_ref, o_ref,
             m_ref, ws_ref, wyt_ref, lhs_ref, s_ref, xp_ref, stage_ref, carry_ref,
             *, lc, nb, nc, ns):
    nk = ns // LANES

    @pl.when(pl.program_id(1) == 0)
    def _():
        zero = jnp.zeros((LANES, LANES), BF16)
        for s in range(lc):
            for t in range(lc):
                m_ref[s * LANES:(s + 1) * LANES, t * LANES:(t + 1) * LANES] = (
                    kb_ref[0, t - s] if t >= s else zero)
        ws_ref[...] = jnp.zeros_like(ws_ref)
        wyt_ref[...] = jnp.zeros_like(wyt_ref)
        for s in range(lc):
            for a in range(GROUPS_PER_TILE):
                r = slice(s * LANES + a * S5_GROUP, s * LANES + (a + 1) * S5_GROUP)
                for part in range(2):
                    dst = slice((part * nk + a // 2) * LANES, (part * nk + a // 2 + 1) * LANES)
                    src = slice(part * LANES, (part + 1) * LANES)
                    ws_ref[r, dst] = wssc_ref[0, r, src]
                    wyt_ref[r, dst] = wysc_ref[0, r, src]
        carry_ref[...] = jnp.zeros_like(carry_ref)

    for b in range(nb):
        for s in range(lc):
            piece = x_ref[b, pl.ds(s, nc, stride=lc), :]
            lhs_ref[b * nc:(b + 1) * nc, s * LANES:(s + 1) * LANES] = piece.astype(BF16)
    lhs = lhs_ref[...]
    rows = nb * nc
    s_all = jnp.dot(lhs, ws_ref[...], preferred_element_type=F32)
    for k in range(2 * nk):
        s_ref[k * rows:(k + 1) * rows, :] = s_all[:, k * LANES:(k + 1) * LANES]

    ar = [are_ref[0, :, k * LANES:(k + 1) * LANES] for k in range(nk)]
    ai = [aim_ref[0, :, k * LANES:(k + 1) * LANES] for k in range(nk)]

    def step(c, carry):
        new = []
        for k in range(nk):
            xr, xi = carry[2 * k], carry[2 * k + 1]
            at_re = pl.ds(k * rows + c, nb, stride=nc)
            at_im = pl.ds((nk + k) * rows + c, nb, stride=nc)
            xp_ref[at_re, :] = xr
            xp_ref[at_im, :] = xi
            new.append(ar[k] * xr - ai[k] * xi + s_ref[at_re, :])
            new.append(ar[k] * xi + ai[k] * xr + s_ref[at_im, :])
        return tuple(new)

    carry = lax.fori_loop(0, nc, step,
                          tuple(carry_ref[k, 0:nb, :] for k in range(2 * nk)))
    for k in range(2 * nk):
        carry_ref[k, 0:nb, :] = carry[k]

    xp = jnp.concatenate([xp_ref[k * rows:(k + 1) * rows, :] for k in range(2 * nk)], axis=-1)
    y = (jnp.dot(lhs, m_ref[...], preferred_element_type=F32)
         + lax.dot_general(xp.astype(BF16), wyt_ref[...], (((1,), (1,)), ((), ())),
                           preferred_element_type=F32))
    for b in range(nb):
        for t in range(lc):
            stage_ref[b, pl.ds(t, nc, stride=lc), :] = (
                y[b * nc:(b + 1) * nc, t * LANES:(t + 1) * LANES])
    yy = stage_ref[...] + d_ref[...] * x_ref[...]
    o_ref[...] = jax.nn.gelu(yy).astype(o_ref.dtype)


def s5_scan(hn, tables, d_skip, *, tc, lc=S5_CHUNK):
    kb, wssc, wysc, a_re, a_im = tables
    nb, l, d = hn.shape
    nt = d // LANES
    nc = tc // lc
    ns = a_re.shape[-1]
    body = functools.partial(_s5_body, lc=lc, nb=nb, nc=nc, ns=ns)
    tile = lambda j, i: (j, 0, 0)
    return pl.pallas_call(
        body,
        grid=(nt, l // tc),
        in_specs=[pl.BlockSpec((nb, tc, LANES), lambda j, i: (0, i, j)),
                  pl.BlockSpec((1, lc, LANES, LANES), lambda j, i: (j, 0, 0, 0)),
                  pl.BlockSpec((1, lc * LANES, 2 * LANES), tile),
                  pl.BlockSpec((1, lc * LANES, 2 * LANES), tile),
                  pl.BlockSpec((1, 1, ns), tile),
                  pl.BlockSpec((1, 1, ns), tile),
                  pl.BlockSpec((1, 1, LANES), lambda j, i: (0, 0, j))],
        out_specs=pl.BlockSpec((nb, tc, LANES), lambda j, i: (0, i, j)),
        out_shape=jax.ShapeDtypeStruct((nb, l, d), BF16),
        scratch_shapes=[pltpu.VMEM((lc * LANES, lc * LANES), BF16),
                        pltpu.VMEM((lc * LANES, 2 * ns), BF16),
                        pltpu.VMEM((lc * LANES, 2 * ns), BF16),
                        pltpu.VMEM((nb * nc, lc * LANES), BF16),
                        pltpu.VMEM((2 * ns // LANES * nb * nc, LANES), F32),
                        pltpu.VMEM((2 * ns // LANES * nb * nc, LANES), F32),
                        pltpu.VMEM((nb, tc, LANES), F32),
                        pltpu.VMEM((2 * ns // LANES, 8, LANES), F32)],
        compiler_params=_params(2),
        name="s5_scan",
    )(hn, kb, wssc, wysc, a_re, a_im, d_skip.reshape(1, 1, d).astype(F32))


def _rope(x, cos, sin_signed, first_half):
    w = x.shape[1]
    reps = w // LANES
    cosw = jnp.tile(cos, (1, reps))
    sinw = jnp.tile(sin_signed, (1, reps))
    sel = jnp.tile(first_half, (1, reps))
    half = HEAD_DIM // 2
    swapped = jnp.where(sel, pltpu.roll(x, w - half, 1), pltpu.roll(x, half, 1))
    return x * cosw + swapped * sinw


def _attn_body(sink_ref, posc_ref, posp_ref, invf_ref, q_ref, kc_ref, kp_ref, vc_ref, vp_ref,
               o_ref, *, q_per_kv):
    n = pl.program_id(1)
    blk = ATTN_BLOCK
    lane = lax.broadcasted_iota(jnp.int32, (blk, LANES), 1)
    first_half = (lane % HEAD_DIM) < (HEAD_DIM // 2)

    def tables(pos_ref):
        ang = pos_ref[...].astype(F32) * invf_ref[...]
        sin = jnp.sin(ang)
        return jnp.cos(ang), jnp.where(first_half, -sin, sin)

    cos_c, sin_c = tables(posc_ref)
    cos_p, sin_p = tables(posp_ref)
    q = _rope(q_ref[...].astype(F32), cos_c, sin_c, first_half)
    q = (q * (HEAD_DIM ** -0.5 * LOG2E)).astype(BF16)
    kc = _rope(kc_ref[...].astype(F32), cos_c, sin_c, first_half)
    kp = _rope(kp_ref[...].astype(F32), cos_p, sin_p, first_half)
    k = jnp.concatenate([kp, kc], axis=0)
    v = jnp.concatenate([vp_ref[...], vc_ref[...]], axis=0).astype(F32)

    qi = lax.broadcasted_iota(jnp.int32, (blk, 2 * blk), 0)
    si = lax.broadcasted_iota(jnp.int32, (blk, 2 * blk), 1)
    diff = blk + qi - si
    mask = (diff >= 0) & (diff < blk) & ((si >= blk) | (n > 0))
    bias = jnp.where(mask, 0.0, -jnp.inf).astype(F32)

    low = lax.broadcasted_iota(jnp.int32, (2 * blk, LANES), 1) < HEAD_DIM

    def split(tile, head_in_low):
        moved = pltpu.roll(tile, HEAD_DIM, 1)
        in_lo, in_hi = (tile, moved) if head_in_low else (moved, tile)
        return (jnp.where(low, in_lo, 0.0).astype(BF16),
                jnp.where(low, 0.0, in_hi).astype(BF16))

    for h in range(N_KV_HEADS):
        tile = slice((h // 2) * LANES, (h // 2 + 1) * LANES)
        k_lo, k_hi = split(k[:, tile], h % 2 == 0)
        v_lo, v_hi = split(v[:, tile], h % 2 == 0)
        for i in range(q_per_kv // 2):
            head = h * q_per_kv + 2 * i
            qt = q[:, (head // 2) * LANES:(head // 2 + 1) * LANES]
            acc = None
            for kk, vv, a in ((k_lo, v_lo, head), (k_hi, v_hi, head + 1)):
                s = lax.dot_general(qt, kk, (((1,), (1,)), ((), ())),
                                    preferred_element_type=F32) + bias
                sink = sink_ref[a] * LOG2E
                m = jnp.maximum(jnp.max(s, axis=-1, keepdims=True), sink)
                e = jnp.exp2(s - m)
                denom = jnp.sum(e, axis=-1, keepdims=True) + jnp.exp2(sink - m)
                o = jnp.dot(e.astype(BF16), vv, preferred_element_type=F32) * (1.0 / denom)
                acc = o if acc is None else acc + o
            o_ref[:, (head // 2) * LANES:(head // 2 + 1) * LANES] = acc.astype(o_ref.dtype)


def swa_attention(qkv, positions, sinks, *, batch, seq):
    t, width = qkv.shape
    kvw = N_KV_HEADS * HEAD_DIM
    qw = width - 2 * kvw
    n_q_heads = qw // HEAD_DIM
    q_per_kv = n_q_heads // N_KV_HEADS
    assert q_per_kv % 2 == 0 and 2 * HEAD_DIM == LANES
    blk = ATTN_BLOCK
    nblk = seq // blk
    half = HEAD_DIM // 2
    inv_freq = jnp.power(ROPE_THETA, -jnp.arange(half, dtype=F32) / half)
    invf = jnp.tile(inv_freq, LANES // half).reshape(1, LANES)
    pos = positions.reshape(t, 1).astype(jnp.int32)
    cur = lambda b, n: (b * nblk + n, 0)
    prev = lambda b, n: (b * nblk + jnp.maximum(n - 1, 0), 0)
    kcol = qw // kvw
    body = functools.partial(_attn_body, q_per_kv=q_per_kv)
    return pl.pallas_call(
        body,
        grid=(batch, nblk),
        in_specs=[pl.BlockSpec(memory_space=pltpu.SMEM),
                  pl.BlockSpec((blk, 1), cur),
                  pl.BlockSpec((blk, 1), prev),
                  pl.BlockSpec((1, LANES), lambda b, n: (0, 0)),
                  pl.BlockSpec((blk, qw), cur),
                  pl.BlockSpec((blk, kvw), lambda b, n: (b * nblk + n, kcol)),
                  pl.BlockSpec((blk, kvw), lambda b, n: (b * nblk + jnp.maximum(n - 1, 0), kcol)),
                  pl.BlockSpec((blk, kvw), lambda b, n: (b * nblk + n, kcol + 1)),
                  pl.BlockSpec((blk, kvw), lambda b, n: (b * nblk + jnp.maximum(n - 1, 0), kcol + 1))],
        out_specs=pl.BlockSpec((blk, qw), cur),
        out_shape=jax.ShapeDtypeStruct((t, qw), BF16),
        compiler_params=_params(2),
        name="swa_attention",
    )(sinks.astype(F32), pos, pos, invf, qkv, qkv, qkv, qkv, qkv)


def _ffn(hn, w_gate, w_up, w_down, layer):
    act = swiglu_up(hn, w_gate, w_up, layer=layer, bm=1024, bn=256)
    return matmul(act, w_down, layer=layer, bm=512, bn=512, name="ffn_down")


def kernel(x, positions, norm_pre_mix, norm_post_mix, norm_pre_ffn, norm_post_ffn, s5_lam_re, s5_lam_im, s5_log_step, s5_b_re, s5_b_im, s5_c_re, s5_c_im, s5_d, s5_w_out1, s5_b_out1, s5_w_out2, s5_b_out2, attn_w_qkv, attn_b_qkv, attn_w_o, attn_b_o, attn_sinks, ffn_w_gate, ffn_w_up, ffn_w_down):
    bsz, seq, d = x.shape
    t = bsz * seq
    h = x.reshape(t, d)

    hn = rmsnorm(h, norm_pre_mix[0], F32)
    tables = _s5_tables(s5_lam_re[0], s5_lam_im[0], s5_log_step[0], s5_b_re[0], s5_b_im[0],
                        s5_c_re[0], s5_c_im[0], S5_CHUNK)
    g = s5_scan(hn.reshape(bsz, seq, d), tables, s5_d[0], tc=2048).reshape(t, d)
    m = glu_matmul(g, s5_w_out1, s5_b_out1, s5_w_out2, s5_b_out2, layer=0, bm=1024, bn=512)
    h, hn = resid_norm(h, m, norm_post_mix[0], norm_pre_ffn[0])
    w_down = ffn_w_down.astype(BF16)
    f = _ffn(hn, ffn_w_gate, ffn_w_up, w_down, 0)
    h, hn = resid_norm(h, f, norm_post_ffn[0], norm_pre_mix[1])

    qkv = matmul(hn, attn_w_qkv, attn_b_qkv, layer=0, bm=1024, bn=512, name="qkv_proj")
    att = swa_attention(qkv, positions, attn_sinks[0], batch=bsz, seq=seq)
    m = matmul(att, attn_w_o, attn_b_o, layer=0, bm=1024, bn=512, name="o_proj")
    h, hn = resid_norm(h, m, norm_post_mix[1], norm_pre_ffn[1])
    f = _ffn(hn, ffn_w_gate, ffn_w_up, w_down, 1)
    h = resid(h, f, norm_post_ffn[1])
    return h.reshape(bsz, seq, d)
```

```python
import functools
import math

import jax
import jax.numpy as jnp
from jax import lax
from jax.experimental import pallas as pl
from jax.experimental.pallas import tpu as pltpu

F32 = jnp.float32
BF16 = jnp.bfloat16

EPS = 1e-6
LANES = 128
S5_GROUP = 16
GROUPS_PER_TILE = LANES // S5_GROUP
S5_CHUNK = 8
HEAD_DIM = 64
N_KV_HEADS = 8
ATTN_BLOCK = 128
ROPE_THETA = 10000.0
LOG2E = math.log2(math.e)
VMEM_LIMIT = 60 * 1024 * 1024


def _params(n_axes, vmem=VMEM_LIMIT):
    return pltpu.CompilerParams(
        dimension_semantics=("arbitrary",) * n_axes, vmem_limit_bytes=vmem)


def _rms(x, g):
    r = lax.rsqrt(jnp.mean(x * x, axis=-1, keepdims=True) + EPS)
    return x * r * g


def _rmsnorm_body(x_ref, g_ref, o_ref):
    o_ref[...] = _rms(x_ref[...], g_ref[...]).astype(o_ref.dtype)


def rmsnorm(x, g, out_dtype, rows=256):
    t, d = x.shape
    return pl.pallas_call(
        _rmsnorm_body,
        grid=(t // rows,),
        in_specs=[pl.BlockSpec((rows, d), lambda i: (i, 0)),
                  pl.BlockSpec((1, d), lambda i: (0, 0))],
        out_specs=pl.BlockSpec((rows, d), lambda i: (i, 0)),
        out_shape=jax.ShapeDtypeStruct((t, d), out_dtype),
        compiler_params=_params(1),
        name="rmsnorm",
    )(x, g.reshape(1, d))


def _resid_norm_body(h_ref, m_ref, gpost_ref, gpre_ref, hout_ref, hn_ref):
    h = h_ref[...] + _rms(m_ref[...].astype(F32), gpost_ref[...])
    hout_ref[...] = h
    hn_ref[...] = _rms(h, gpre_ref[...]).astype(hn_ref.dtype)


def resid_norm(h, m, g_post, g_pre, rows=256):
    t, d = h.shape
    row_spec = pl.BlockSpec((rows, d), lambda i: (i, 0))
    vec_spec = pl.BlockSpec((1, d), lambda i: (0, 0))
    return pl.pallas_call(
        _resid_norm_body,
        grid=(t // rows,),
        in_specs=[row_spec, row_spec, vec_spec, vec_spec],
        out_specs=[row_spec, row_spec],
        out_shape=[jax.ShapeDtypeStruct((t, d), F32),
                   jax.ShapeDtypeStruct((t, d), BF16)],
        compiler_params=_params(1),
        name="resid_norm",
    )(h, m, g_post.reshape(1, d), g_pre.reshape(1, d))


def _resid_body(h_ref, m_ref, gpost_ref, hout_ref):
    hout_ref[...] = h_ref[...] + _rms(m_ref[...].astype(F32), gpost_ref[...])


def resid(h, m, g_post, rows=256):
    t, d = h.shape
    row_spec = pl.BlockSpec((rows, d), lambda i: (i, 0))
    return pl.pallas_call(
        _resid_body,
        grid=(t // rows,),
        in_specs=[row_spec, row_spec, pl.BlockSpec((1, d), lambda i: (0, 0))],
        out_specs=row_spec,
        out_shape=jax.ShapeDtypeStruct((t, d), F32),
        compiler_params=_params(1),
        name="resid",
    )(h, m, g_post.reshape(1, d))


def _mm_bias_body(x_ref, w_ref, b_ref, o_ref):
    acc = jnp.dot(x_ref[...], w_ref[...].astype(BF16), preferred_element_type=F32)
    o_ref[...] = (acc + b_ref[...]).astype(o_ref.dtype)


def _mm_body(x_ref, w_ref, o_ref):
    o_ref[...] = jnp.dot(x_ref[...], w_ref[...].astype(BF16),
                         preferred_element_type=F32).astype(o_ref.dtype)


def _layer_spec(rows, bn, layer):
    return pl.BlockSpec((None, rows, bn), lambda i, j: (layer, 0, j))


def matmul(x, w, b=None, *, layer, bm, bn, out_dtype=BF16, name="matmul"):
    m, k = x.shape
    n = w.shape[2]
    in_specs = [pl.BlockSpec((bm, k), lambda i, j: (i, 0)), _layer_spec(k, bn, layer)]
    args = [x, w]
    body = _mm_body
    if b is not None:
        in_specs.append(_layer_spec(1, bn, layer))
        args.append(b.reshape(b.shape[0], 1, n))
        body = _mm_bias_body
    return pl.pallas_call(
        body,
        grid=(m // bm, n // bn),
        in_specs=in_specs,
        out_specs=pl.BlockSpec((bm, bn), lambda i, j: (i, j)),
        out_shape=jax.ShapeDtypeStruct((m, n), out_dtype),
        compiler_params=_params(2),
        name=name,
    )(*args)


def _glu_body(x_ref, w1_ref, b1_ref, w2_ref, b2_ref, o_ref):
    x = x_ref[...]
    a = jnp.dot(x, w1_ref[...].astype(BF16), preferred_element_type=F32) + b1_ref[...]
    b = jnp.dot(x, w2_ref[...].astype(BF16), preferred_element_type=F32) + b2_ref[...]
    o_ref[...] = (a * jax.nn.sigmoid(b)).astype(o_ref.dtype)


def glu_matmul(x, w1, b1, w2, b2, *, layer, bm, bn):
    m, k = x.shape
    n = w1.shape[2]
    w_spec = _layer_spec(k, bn, layer)
    b_spec = _layer_spec(1, bn, layer)
    return pl.pallas_call(
        _glu_body,
        grid=(m // bm, n // bn),
        in_specs=[pl.BlockSpec((bm, k), lambda i, j: (i, 0)),
                  w_spec, b_spec, w_spec, b_spec],
        out_specs=pl.BlockSpec((bm, bn), lambda i, j: (i, j)),
        out_shape=jax.ShapeDtypeStruct((m, n), BF16),
        compiler_params=_params(2),
        name="glu_matmul",
    )(x, w1, b1.reshape(b1.shape[0], 1, n), w2, b2.reshape(b2.shape[0], 1, n))


def _swiglu_up_body(x_ref, wg_ref, wu_ref, o_ref):
    x = x_ref[...]
    a = jnp.dot(x, wg_ref[...].astype(BF16), preferred_element_type=F32)
    b = jnp.dot(x, wu_ref[...].astype(BF16), preferred_element_type=F32)
    o_ref[...] = (jax.nn.silu(a) * b).astype(o_ref.dtype)


def swiglu_up(x, wg, wu, *, layer, bm, bn):
    m, k = x.shape
    n = wg.shape[2]
    w_spec = _layer_spec(k, bn, layer)
    return pl.pallas_call(
        _swiglu_up_body,
        grid=(m // bm, n // bn),
        in_specs=[pl.BlockSpec((bm, k), lambda i, j: (i, 0)), w_spec, w_spec],
        out_specs=pl.BlockSpec((bm, bn), lambda i, j: (i, j)),
        out_shape=jax.ShapeDtypeStruct((m, n), BF16),
        compiler_params=_params(2),
        name="swiglu_up",
    )(x, wg, wu)


def _s5_tables(lam_re, lam_im, log_step, b_re, b_im, c_re, c_im, lc):
    g, p = lam_re.shape
    ch = b_re.shape[-1]
    assert ch == S5_GROUP and 2 * p == LANES
    ng = GROUPS_PER_TILE
    nt = g // ng
    hi = lax.Precision.HIGHEST
    lr, li = lam_re.astype(F32), lam_im.astype(F32)
    step = jnp.exp(log_step.astype(F32))[:, None]
    zr, zi = lr * step, li * step
    taus = jnp.arange(lc + 1, dtype=F32)[:, None, None]
    mag = jnp.exp(zr[None] * taus)
    pr = mag * jnp.cos(zi[None] * taus)
    pi = mag * jnp.sin(zi[None] * taus)
    nr, ni = pr[1] - 1.0, pi[1]
    den = lr * lr + li * li
    qr = (nr * lr + ni * li) / den
    qi = (ni * lr - nr * li) / den
    b_r = jnp.swapaxes(b_re.astype(F32), 1, 2)
    b_i = jnp.swapaxes(b_im.astype(F32), 1, 2)
    br = qr[:, None, :] * b_r - qi[:, None, :] * b_i
    bi = qr[:, None, :] * b_i + qi[:, None, :] * b_r
    cr, ci = c_re.astype(F32), c_im.astype(F32)
    w_re = pr[:lc, :, None, :] * br[None] - pi[:lc, :, None, :] * bi[None]
    w_im = pr[:lc, :, None, :] * bi[None] + pi[:lc, :, None, :] * br[None]
    k = (jnp.einsum('gop,tgcp->tgoc', cr, w_re, precision=hi)
         - jnp.einsum('gop,tgcp->tgoc', ci, w_im, precision=hi))
    eye = jnp.eye(ng, dtype=F32)
    kb = jnp.einsum('tjaoc,ab->jtacbo', k.reshape(lc, nt, ng, ch, ch), eye).reshape(
        nt, lc, LANES, LANES)

    zero = jnp.zeros_like(pr)
    even = (jnp.arange(g) % 2 == 0)[None, :, None]
    halves = lambda a: jnp.concatenate(
        [jnp.where(even, a, zero), jnp.where(even, zero, a)], axis=-1)
    per_tile = lambda a: jnp.swapaxes(a.reshape(lc + 1, nt, ng, a.shape[-1]), 0, 1)
    grp = lambda a: a.reshape((nt, ng) + a.shape[1:])
    cat = lambda *xs: jnp.concatenate(xs, axis=-1)
    return (kb.astype(BF16), per_tile(halves(pr)), per_tile(halves(pi)),
            grp(cat(br, br, bi, bi)), grp(cat(cr, cr, ci, ci)),
            pr[lc].reshape(nt, 1, ng * p), pi[lc].reshape(nt, 1, ng * p))


def _s5_build(kb_ref, prh_ref, pih_ref, bd_ref, cd_ref, m_ref, ws_ref, wyt_ref, lc, nk):
    zero = jnp.zeros((LANES, LANES), BF16)
    for s in range(lc):
        for t in range(lc):
            m_ref[s * LANES:(s + 1) * LANES, t * LANES:(t + 1) * LANES] = (
                kb_ref[0, t - s] if t >= s else zero)
    ws_ref[...] = jnp.zeros_like(ws_ref)
    wyt_ref[...] = jnp.zeros_like(wyt_ref)
    for a in range(GROUPS_PER_TILE):
        bd, cd = bd_ref[0, a], cd_ref[0, a]
        b_r, b_i = bd[:, :LANES], bd[:, LANES:]
        c_r, c_i = cd[:, :LANES], cd[:, LANES:]
        re_tile = slice((a // 2) * LANES, (a // 2 + 1) * LANES)
        im_tile = slice((nk + a // 2) * LANES, (nk + a // 2 + 1) * LANES)
        for s in range(lc):
            rows = slice(s * LANES + a * S5_GROUP, s * LANES + (a + 1) * S5_GROUP)
            p_r, p_i = prh_ref[0, lc - 1 - s, a:a + 1, :], pih_ref[0, lc - 1 - s, a:a + 1, :]
            ws_ref[rows, re_tile] = (p_r * b_r - p_i * b_i).astype(BF16)
            ws_ref[rows, im_tile] = (p_r * b_i + p_i * b_r).astype(BF16)
            p_r, p_i = prh_ref[0, s + 1, a:a + 1, :], pih_ref[0, s + 1, a:a + 1, :]
            wyt_ref[rows, re_tile] = (p_r * c_r - p_i * c_i).astype(BF16)
            wyt_ref[rows, im_tile] = (-(p_r * c_i + p_i * c_r)).astype(BF16)


def _s5_body(x_ref, kb_ref, prh_ref, pih_ref, bd_ref, cd_ref, are_ref, aim_ref, d_ref, o_ref,
             m_ref, ws_ref, wyt_ref, lhs_ref, s_ref, xp_ref, stage_ref, carry_ref,
             *, lc, nb, nc, ns):
    nk = ns // LANES

    @pl.when(pl.program_id(1) == 0)
    def _():
        _s5_build(kb_ref, prh_ref, pih_ref, bd_ref, cd_ref, m_ref, ws_ref, wyt_ref, lc, nk)
        carry_ref[...] = jnp.zeros_like(carry_ref)

    for b in range(nb):
        for s in range(lc):
            piece = x_ref[b, pl.ds(s, nc, stride=lc), :]
            lhs_ref[b * nc:(b + 1) * nc, s * LANES:(s + 1) * LANES] = piece.astype(BF16)
    lhs = lhs_ref[...]
    s_all = jnp.dot(lhs, ws_ref[...], preferred_element_type=F32)
    for k in range(2 * nk):
        for b in range(nb):
            s_ref[k, pl.ds(b, nc, stride=nb), :] = (
                s_all[b * nc:(b + 1) * nc, k * LANES:(k + 1) * LANES])

    tile_rows = lambda ref, k: jnp.broadcast_to(ref[0, :, k * LANES:(k + 1) * LANES], (nb, LANES))
    ar = [tile_rows(are_ref, k) for k in range(nk)]
    ai = [tile_rows(aim_ref, k) for k in range(nk)]

    def step(c, carry):
        r = pl.ds(pl.multiple_of(c * nb, nb), nb)
        new = []
        for k in range(nk):
            xr, xi = carry[2 * k], carry[2 * k + 1]
            xp_ref[k, r, :] = xr
            xp_ref[nk + k, r, :] = xi
            new.append(ar[k] * xr - ai[k] * xi + s_ref[k, r, :])
            new.append(ar[k] * xi + ai[k] * xr + s_ref[nk + k, r, :])
        return tuple(new)

    carry = lax.fori_loop(0, nc, step,
                          tuple(carry_ref[k, 0:nb, :] for k in range(2 * nk)))
    for k in range(2 * nk):
        carry_ref[k, 0:nb, :] = carry[k]

    xp = jnp.concatenate(
        [jnp.concatenate([xp_ref[k, pl.ds(b, nc, stride=nb), :] for b in range(nb)], axis=0)
         for k in range(2 * nk)], axis=-1)
    y = (jnp.dot(lhs, m_ref[...], preferred_element_type=F32)
         + lax.dot_general(xp.astype(BF16), wyt_ref[...], (((1,), (1,)), ((), ())),
                           preferred_element_type=F32))
    for b in range(nb):
        for t in range(lc):
            stage_ref[b, pl.ds(t, nc, stride=lc), :] = (
                y[b * nc:(b + 1) * nc, t * LANES:(t + 1) * LANES])
    yy = stage_ref[...] + d_ref[...] * x_ref[...]
    o_ref[...] = jax.nn.gelu(yy).astype(o_ref.dtype)


def s5_scan(hn, tables, d_skip, *, tc, lc=S5_CHUNK):
    kb, prh, pih, bd, cd, a_re, a_im = tables
    ng = GROUPS_PER_TILE
    nb, l, d = hn.shape
    nt = d // LANES
    nc = tc // lc
    ns = a_re.shape[-1]
    rows = nb * nc
    body = functools.partial(_s5_body, lc=lc, nb=nb, nc=nc, ns=ns)
    tile4 = lambda j, i: (j, 0, 0, 0)
    tile3 = lambda j, i: (j, 0, 0)
    return pl.pallas_call(
        body,
        grid=(nt, l // tc),
        in_specs=[pl.BlockSpec((nb, tc, LANES), lambda j, i: (0, i, j)),
                  pl.BlockSpec((1, lc, LANES, LANES), tile4),
                  pl.BlockSpec((1, lc + 1, ng, LANES), tile4),
                  pl.BlockSpec((1, lc + 1, ng, LANES), tile4),
                  pl.BlockSpec((1, ng, S5_GROUP, 2 * LANES), tile4),
                  pl.BlockSpec((1, ng, S5_GROUP, 2 * LANES), tile4),
                  pl.BlockSpec((1, 1, ns), tile3),
                  pl.BlockSpec((1, 1, ns), tile3),
                  pl.BlockSpec((1, 1, LANES), lambda j, i: (0, 0, j))],
        out_specs=pl.BlockSpec((nb, tc, LANES), lambda j, i: (0, i, j)),
        out_shape=jax.ShapeDtypeStruct((nb, l, d), BF16),
        scratch_shapes=[pltpu.VMEM((lc * LANES, lc * LANES), BF16),
                        pltpu.VMEM((lc * LANES, 2 * ns), BF16),
                        pltpu.VMEM((lc * LANES, 2 * ns), BF16),
                        pltpu.VMEM((rows, lc * LANES), BF16),
                        pltpu.VMEM((2 * ns // LANES, rows, LANES), F32),
                        pltpu.VMEM((2 * ns // LANES, rows, LANES), F32),
                        pltpu.VMEM((nb, tc, LANES), F32),
                        pltpu.VMEM((2 * ns // LANES, 8, LANES), F32)],
        compiler_params=_params(2),
        name="s5_scan",
    )(hn, kb, prh, pih, bd, cd, a_re, a_im, d_skip.reshape(1, 1, d).astype(F32))


def _rope(x, cos, sin_signed, first_half):
    w = x.shape[1]
    reps = w // LANES
    cosw = jnp.tile(cos, (1, reps))
    sinw = jnp.tile(sin_signed, (1, reps))
    sel = jnp.tile(first_half, (1, reps))
    half = HEAD_DIM // 2
    swapped = jnp.where(sel, pltpu.roll(x, w - half, 1), pltpu.roll(x, half, 1))
    return x * cosw + swapped * sinw


def _attn_body(sink_ref, posc_ref, posp_ref, invf_ref, q_ref, kc_ref, kp_ref, vc_ref, vp_ref,
               o_ref, *, q_per_kv):
    n = pl.program_id(1)
    blk = ATTN_BLOCK
    lane = lax.broadcasted_iota(jnp.int32, (blk, LANES), 1)
    first_half = (lane % HEAD_DIM) < (HEAD_DIM // 2)

    def tables(pos_ref):
        ang = pos_ref[...].astype(F32) * invf_ref[...]
        sin = jnp.sin(ang)
        return jnp.cos(ang), jnp.where(first_half, -sin, sin)

    cos_c, sin_c = tables(posc_ref)
    cos_p, sin_p = tables(posp_ref)
    q = _rope(q_ref[...].astype(F32), cos_c, sin_c, first_half)
    q = (q * (HEAD_DIM ** -0.5 * LOG2E)).astype(BF16)
    kc = _rope(kc_ref[...].astype(F32), cos_c, sin_c, first_half)
    kp = _rope(kp_ref[...].astype(F32), cos_p, sin_p, first_half)
    k = jnp.concatenate([kp, kc], axis=0)
    v = jnp.concatenate([vp_ref[...], vc_ref[...]], axis=0).astype(F32)

    qi = lax.broadcasted_iota(jnp.int32, (blk, 2 * blk), 0)
    si = lax.broadcasted_iota(jnp.int32, (blk, 2 * blk), 1)
    diff = blk + qi - si
    mask = (diff >= 0) & (diff < blk) & ((si >= blk) | (n > 0))
    bias = jnp.where(mask, 0.0, -jnp.inf).astype(F32)

    low = lax.broadcasted_iota(jnp.int32, (2 * blk, LANES), 1) < HEAD_DIM

    def split(tile, head_in_low):
        moved = pltpu.roll(tile, HEAD_DIM, 1)
        in_lo, in_hi = (tile, moved) if head_in_low else (moved, tile)
        return (jnp.where(low, in_lo, 0.0).astype(BF16),
                jnp.where(low, 0.0, in_hi).astype(BF16))

    for h in range(N_KV_HEADS):
        tile = slice((h // 2) * LANES, (h // 2 + 1) * LANES)
        k_lo, k_hi = split(k[:, tile], h % 2 == 0)
        v_lo, v_hi = split(v[:, tile], h % 2 == 0)
        for i in range(q_per_kv // 2):
            head = h * q_per_kv + 2 * i
            qt = q[:, (head // 2) * LANES:(head // 2 + 1) * LANES]
            acc = None
            for kk, vv, a in ((k_lo, v_lo, head), (k_hi, v_hi, head + 1)):
                s = lax.dot_general(qt, kk, (((1,), (1,)), ((), ())),
                                    preferred_element_type=F32) + bias
                sink = sink_ref[a] * LOG2E
                m = jnp.maximum(jnp.max(s, axis=-1, keepdims=True), sink)
                e = jnp.exp2(s - m)
                denom = jnp.sum(e, axis=-1, keepdims=True) + jnp.exp2(sink - m)
                o = jnp.dot(e.astype(BF16), vv, preferred_element_type=F32) * (1.0 / denom)
                acc = o if acc is None else acc + o
            o_ref[:, (head // 2) * LANES:(head // 2 + 1) * LANES] = acc.astype(o_ref.dtype)


def swa_attention(qkv, positions, sinks, *, batch, seq):
    t, width = qkv.shape
    kvw = N_KV_HEADS * HEAD_DIM
    qw = width - 2 * kvw
    n_q_heads = qw // HEAD_DIM
    q_per_kv = n_q_heads // N_KV_HEADS
    assert q_per_kv % 2 == 0 and 2 * HEAD_DIM == LANES
    blk = ATTN_BLOCK
    nblk = seq // blk
    half = HEAD_DIM // 2
    inv_freq = jnp.power(ROPE_THETA, -jnp.arange(half, dtype=F32) / half)
    invf = jnp.tile(inv_freq, LANES // half).reshape(1, LANES)
    pos = positions.reshape(t, 1).astype(jnp.int32)
    cur = lambda b, n: (b * nblk + n, 0)
    prev = lambda b, n: (b * nblk + jnp.maximum(n - 1, 0), 0)
    kcol = qw // kvw
    body = functools.partial(_attn_body, q_per_kv=q_per_kv)
    return pl.pallas_call(
        body,
        grid=(batch, nblk),
        in_specs=[pl.BlockSpec(memory_space=pltpu.SMEM),
                  pl.BlockSpec((blk, 1), cur),
                  pl.BlockSpec((blk, 1), prev),
                  pl.BlockSpec((1, LANES), lambda b, n: (0, 0)),
                  pl.BlockSpec((blk, qw), cur),
                  pl.BlockSpec((blk, kvw), lambda b, n: (b * nblk + n, kcol)),
                  pl.BlockSpec((blk, kvw), lambda b, n: (b * nblk + jnp.maximum(n - 1, 0), kcol)),
                  pl.BlockSpec((blk, kvw), lambda b, n: (b * nblk + n, kcol + 1)),
                  pl.BlockSpec((blk, kvw), lambda b, n: (b * nblk + jnp.maximum(n - 1, 0), kcol + 1))],
        out_specs=pl.BlockSpec((blk, qw), cur),
        out_shape=jax.ShapeDtypeStruct((t, qw), BF16),
        compiler_params=_params(2),
        name="swa_attention",
    )(sinks.astype(F32), pos, pos, invf, qkv, qkv, qkv, qkv, qkv)


def _ffn(hn, w_gate, w_up, w_down, layer):
    act = swiglu_up(hn, w_gate, w_up, layer=layer, bm=1024, bn=256)
    return matmul(act, w_down, layer=layer, bm=512, bn=512, name="ffn_down")


def kernel(x, positions, norm_pre_mix, norm_post_mix, norm_pre_ffn, norm_post_ffn, s5_lam_re, s5_lam_im, s5_log_step, s5_b_re, s5_b_im, s5_c_re, s5_c_im, s5_d, s5_w_out1, s5_b_out1, s5_w_out2, s5_b_out2, attn_w_qkv, attn_b_qkv, attn_w_o, attn_b_o, attn_sinks, ffn_w_gate, ffn_w_up, ffn_w_down):
    bsz, seq, d = x.shape
    t = bsz * seq
    h = x.reshape(t, d)

    hn = rmsnorm(h, norm_pre_mix[0], F32)
    tables = _s5_tables(s5_lam_re[0], s5_lam_im[0], s5_log_step[0], s5_b_re[0], s5_b_im[0],
                        s5_c_re[0], s5_c_im[0], S5_CHUNK)
    g = s5_scan(hn.reshape(bsz, seq, d), tables, s5_d[0], tc=2048).reshape(t, d)
    m = glu_matmul(g, s5_w_out1, s5_b_out1, s5_w_out2, s5_b_out2, layer=0, bm=1024, bn=512)
    h, hn = resid_norm(h, m, norm_post_mix[0], norm_pre_ffn[0])
    w_down = ffn_w_down.astype(BF16)
    f = _ffn(hn, ffn_w_gate, ffn_w_up, w_down, 0)
    h, hn = resid_norm(h, f, norm_post_ffn[0], norm_pre_mix[1])

    qkv = matmul(hn, attn_w_qkv, attn_b_qkv, layer=0, bm=1024, bn=512, name="qkv_proj")
    att = swa_attention(qkv, positions, attn_sinks[0], batch=bsz, seq=seq)
    m = matmul(att, attn_w_o, attn_b_o, layer=0, bm=1024, bn=512, name="o_proj")
    h, hn = resid_norm(h, m, norm_post_mix[1], norm_pre_ffn[1])
    f = _ffn(hn, ffn_w_gate, ffn_w_up, w_down, 1)
    h = resid(h, f, norm_post_ffn[1])
    return h.reshape(bsz, seq, d)
```

```python
import functools
import math

import jax
import jax.numpy as jnp
from jax import lax
from jax.experimental import pallas as pl
from jax.experimental.pallas import tpu as pltpu

F32 = jnp.float32
BF16 = jnp.bfloat16

EPS = 1e-6
LANES = 128
S5_GROUP = 16
GROUPS_PER_TILE = LANES // S5_GROUP
S5_CHUNK = 8
HEAD_DIM = 64
N_KV_HEADS = 8
ATTN_BLOCK = 128
ROPE_THETA = 10000.0
ROPE_ROW_CHUNKS = 4
LOG2E = math.log2(math.e)
VMEM_LIMIT = 60 * 1024 * 1024


def _params(n_axes, vmem=VMEM_LIMIT):
    return pltpu.CompilerParams(
        dimension_semantics=("arbitrary",) * n_axes, vmem_limit_bytes=vmem)


def _rms(x, g):
    r = lax.rsqrt(jnp.mean(x * x, axis=-1, keepdims=True) + EPS)
    return x * r * g


def _rmsnorm_body(x_ref, g_ref, o_ref):
    o_ref[...] = _rms(x_ref[...], g_ref[...]).astype(o_ref.dtype)


def _row_call(body, n_rows_in, n_vec_in, out_dtypes, t, d, name):
    streams = n_rows_in + len(out_dtypes)
    rows = 512 if streams <= 3 else 256
    row_spec = pl.BlockSpec((rows, d), lambda i: (i, 0))
    vec_spec = pl.BlockSpec((1, d), lambda i: (0, 0))
    return pl.pallas_call(
        body,
        grid=(t // rows,),
        in_specs=[row_spec] * n_rows_in + [vec_spec] * n_vec_in,
        out_specs=[row_spec] * len(out_dtypes),
        out_shape=[jax.ShapeDtypeStruct((t, d), dt) for dt in out_dtypes],
        compiler_params=_params(1),
        name=name,
    )


def rmsnorm(x, g, out_dtype):
    t, d = x.shape
    return _row_call(_rmsnorm_body, 1, 1, [out_dtype], t, d, "rmsnorm")(x, g.reshape(1, d))[0]


def _mix_norm_body(h_ref, m_ref, gm_ref, gpre_ref, hn_ref):
    h = h_ref[...] + _rms(m_ref[...].astype(F32), gm_ref[...])
    hn_ref[...] = _rms(h, gpre_ref[...]).astype(hn_ref.dtype)


def mix_norm(h, m, g_m, g_pre):
    t, d = h.shape
    return _row_call(_mix_norm_body, 2, 2, [BF16], t, d, "mix_norm")(
        h, m, g_m.reshape(1, d), g_pre.reshape(1, d))[0]


def _block_resid_norm_body(h_ref, m_ref, f_ref, gm_ref, gf_ref, gpre_ref, hout_ref, hn_ref):
    h = h_ref[...] + _rms(m_ref[...].astype(F32), gm_ref[...])
    h = h + _rms(f_ref[...].astype(F32), gf_ref[...])
    hout_ref[...] = h
    hn_ref[...] = _rms(h, gpre_ref[...]).astype(hn_ref.dtype)


def block_resid_norm(h, m, f, g_m, g_f, g_pre):
    t, d = h.shape
    return _row_call(_block_resid_norm_body, 3, 3, [F32, BF16], t, d, "block_resid_norm")(
        h, m, f, g_m.reshape(1, d), g_f.reshape(1, d), g_pre.reshape(1, d))


def _block_resid_body(h_ref, m_ref, f_ref, gm_ref, gf_ref, hout_ref):
    h = h_ref[...] + _rms(m_ref[...].astype(F32), gm_ref[...])
    hout_ref[...] = h + _rms(f_ref[...].astype(F32), gf_ref[...])


def block_resid(h, m, f, g_m, g_f):
    t, d = h.shape
    return _row_call(_block_resid_body, 3, 2, [F32], t, d, "block_resid")(
        h, m, f, g_m.reshape(1, d), g_f.reshape(1, d))[0]


def _mm_bias_body(x_ref, w_ref, b_ref, o_ref):
    acc = jnp.dot(x_ref[...], w_ref[...].astype(BF16), preferred_element_type=F32)
    o_ref[...] = (acc + b_ref[...]).astype(o_ref.dtype)


def _mm_body(x_ref, w_ref, o_ref):
    o_ref[...] = jnp.dot(x_ref[...], w_ref[...].astype(BF16),
                         preferred_element_type=F32).astype(o_ref.dtype)


def _layer_spec(rows, bn, layer):
    return pl.BlockSpec((None, rows, bn), lambda i, j: (layer, 0, j))


def matmul(x, w, b=None, *, layer, bm, bn, out_dtype=BF16, name="matmul"):
    m, k = x.shape
    n = w.shape[2]
    in_specs = [pl.BlockSpec((bm, k), lambda i, j: (i, 0)), _layer_spec(k, bn, layer)]
    args = [x, w]
    body = _mm_body
    if b is not None:
        in_specs.append(_layer_spec(1, bn, layer))
        args.append(b.reshape(b.shape[0], 1, n))
        body = _mm_bias_body
    return pl.pallas_call(
        body,
        grid=(m // bm, n // bn),
        in_specs=in_specs,
        out_specs=pl.BlockSpec((bm, bn), lambda i, j: (i, j)),
        out_shape=jax.ShapeDtypeStruct((m, n), out_dtype),
        compiler_params=_params(2),
        name=name,
    )(*args)


def _glu_body(x_ref, w1_ref, b1_ref, w2_ref, b2_ref, o_ref):
    x = x_ref[...]
    a = jnp.dot(x, w1_ref[...].astype(BF16), preferred_element_type=F32) + b1_ref[...]
    b = jnp.dot(x, w2_ref[...].astype(BF16), preferred_element_type=F32) + b2_ref[...]
    o_ref[...] = (a * jax.nn.sigmoid(b)).astype(o_ref.dtype)


def glu_matmul(x, w1, b1, w2, b2, *, layer, bm, bn):
    m, k = x.shape
    n = w1.shape[2]
    w_spec = _layer_spec(k, bn, layer)
    b_spec = _layer_spec(1, bn, layer)
    return pl.pallas_call(
        _glu_body,
        grid=(m // bm, n // bn),
        in_specs=[pl.BlockSpec((bm, k), lambda i, j: (i, 0)),
                  w_spec, b_spec, w_spec, b_spec],
        out_specs=pl.BlockSpec((bm, bn), lambda i, j: (i, j)),
        out_shape=jax.ShapeDtypeStruct((m, n), BF16),
        compiler_params=_params(2),
        name="glu_matmul",
    )(x, w1, b1.reshape(b1.shape[0], 1, n), w2, b2.reshape(b2.shape[0], 1, n))


def _swiglu_up_body(x_ref, wg_ref, wu_ref, o_ref):
    x = x_ref[...]
    a = jnp.dot(x, wg_ref[...].astype(BF16), preferred_element_type=F32)
    b = jnp.dot(x, wu_ref[...].astype(BF16), preferred_element_type=F32)
    o_ref[...] = (jax.nn.silu(a) * b).astype(o_ref.dtype)


def swiglu_up(x, wg, wu, *, layer, bm, bn):
    m, k = x.shape
    n = wg.shape[2]
    w_spec = _layer_spec(k, bn, layer)
    x_spec = pl.BlockSpec((bm, k), lambda i, j: (i, 0), pipeline_mode=pl.Buffered(1))
    return pl.pallas_call(
        _swiglu_up_body,
        grid=(m // bm, n // bn),
        in_specs=[x_spec, w_spec, w_spec],
        out_specs=pl.BlockSpec((bm, bn), lambda i, j: (i, j)),
        out_shape=jax.ShapeDtypeStruct((m, n), BF16),
        compiler_params=_params(2),
        name="swiglu_up",
    )(x, wg, wu)


def _s5_tables(lam_re, lam_im, log_step, b_re, b_im, c_re, c_im, lc):
    g, p = lam_re.shape
    ch = b_re.shape[-1]
    assert ch == S5_GROUP and 2 * p == LANES
    ng = GROUPS_PER_TILE
    nt = g // ng
    hi = lax.Precision.HIGHEST
    lr, li = lam_re.astype(F32), lam_im.astype(F32)
    step = jnp.exp(log_step.astype(F32))[:, None]
    zr, zi = lr * step, li * step
    taus = jnp.arange(lc + 1, dtype=F32)[:, None, None]
    mag = jnp.exp(zr[None] * taus)
    pr = mag * jnp.cos(zi[None] * taus)
    pi = mag * jnp.sin(zi[None] * taus)
    nr, ni = pr[1] - 1.0, pi[1]
    den = lr * lr + li * li
    qr = (nr * lr + ni * li) / den
    qi = (ni * lr - nr * li) / den
    b_r = jnp.swapaxes(b_re.astype(F32), 1, 2)
    b_i = jnp.swapaxes(b_im.astype(F32), 1, 2)
    br = qr[:, None, :] * b_r - qi[:, None, :] * b_i
    bi = qr[:, None, :] * b_i + qi[:, None, :] * b_r
    cr, ci = c_re.astype(F32), c_im.astype(F32)
    w_re = pr[:lc, :, None, :] * br[None] - pi[:lc, :, None, :] * bi[None]
    w_im = pr[:lc, :, None, :] * bi[None] + pi[:lc, :, None, :] * br[None]
    k = (jnp.einsum('gop,tgcp->tgoc', cr, w_re, precision=hi)
         - jnp.einsum('gop,tgcp->tgoc', ci, w_im, precision=hi))
    eye = jnp.eye(ng, dtype=F32)
    kb = jnp.einsum('tjaoc,ab->jtacbo', k.reshape(lc, nt, ng, ch, ch), eye).reshape(
        nt, lc, LANES, LANES)

    zero = jnp.zeros_like(pr)
    even = (jnp.arange(g) % 2 == 0)[None, :, None]
    halves = lambda a: jnp.concatenate(
        [jnp.where(even, a, zero), jnp.where(even, zero, a)], axis=-1)
    per_tile = lambda a: jnp.swapaxes(a.reshape(lc + 1, nt, ng, a.shape[-1]), 0, 1)
    grp = lambda a: a.reshape((nt, ng) + a.shape[1:])
    cat = lambda *xs: jnp.concatenate(xs, axis=-1)
    return (kb.astype(BF16), per_tile(halves(pr)), per_tile(halves(pi)),
            grp(cat(br, br, bi, bi)), grp(cat(cr, cr, ci, ci)),
            pr[lc].reshape(nt, 1, ng * p), pi[lc].reshape(nt, 1, ng * p))


def _s5_build(kb_ref, prh_ref, pih_ref, bd_ref, cd_ref, m_ref, ws_ref, wyt_ref, lc, nk):
    zero = jnp.zeros((LANES, LANES), BF16)
    for s in range(lc):
        for t in range(lc):
            m_ref[s * LANES:(s + 1) * LANES, t * LANES:(t + 1) * LANES] = (
                kb_ref[0, t - s] if t >= s else zero)
    ws_ref[...] = jnp.zeros_like(ws_ref)
    wyt_ref[...] = jnp.zeros_like(wyt_ref)
    for a in range(GROUPS_PER_TILE):
        bd, cd = bd_ref[0, a], cd_ref[0, a]
        b_r, b_i = bd[:, :LANES], bd[:, LANES:]
        c_r, c_i = cd[:, :LANES], cd[:, LANES:]
        re_tile = slice((a // 2) * LANES, (a // 2 + 1) * LANES)
        im_tile = slice((nk + a // 2) * LANES, (nk + a // 2 + 1) * LANES)
        for s in range(lc):
            rows = slice(s * LANES + a * S5_GROUP, s * LANES + (a + 1) * S5_GROUP)
            p_r, p_i = prh_ref[0, lc - 1 - s, a:a + 1, :], pih_ref[0, lc - 1 - s, a:a + 1, :]
            ws_ref[rows, re_tile] = (p_r * b_r - p_i * b_i).astype(BF16)
            ws_ref[rows, im_tile] = (p_r * b_i + p_i * b_r).astype(BF16)
            p_r, p_i = prh_ref[0, s + 1, a:a + 1, :], pih_ref[0, s + 1, a:a + 1, :]
            wyt_ref[rows, re_tile] = (p_r * c_r - p_i * c_i).astype(BF16)
            wyt_ref[rows, im_tile] = (-(p_r * c_i + p_i * c_r)).astype(BF16)


def _s5_body(x_ref, kb_ref, prh_ref, pih_ref, bd_ref, cd_ref, are_ref, aim_ref, d_ref, o_ref,
             m_ref, ws_ref, wyt_ref, lhs_ref, s_ref, xp_ref, stage_ref, carry_ref,
             *, lc, nb, nc, ns):
    nk = ns // LANES

    @pl.when(pl.program_id(1) == 0)
    def _():
        _s5_build(kb_ref, prh_ref, pih_ref, bd_ref, cd_ref, m_ref, ws_ref, wyt_ref, lc, nk)
        carry_ref[...] = jnp.zeros_like(carry_ref)

    for b in range(nb):
        for s in range(lc):
            piece = x_ref[b, pl.ds(s, nc, stride=lc), :]
            lhs_ref[b * nc:(b + 1) * nc, s * LANES:(s + 1) * LANES] = piece.astype(BF16)
    lhs = lhs_ref[...]
    s_all = jnp.dot(lhs, ws_ref[...], preferred_element_type=F32)
    for k in range(2 * nk):
        for b in range(nb):
            s_ref[k, pl.ds(b, nc, stride=nb), :] = (
                s_all[b * nc:(b + 1) * nc, k * LANES:(k + 1) * LANES])

    tile_rows = lambda ref, k: jnp.broadcast_to(ref[0, :, k * LANES:(k + 1) * LANES], (nb, LANES))
    ar = [tile_rows(are_ref, k) for k in range(nk)]
    ai = [tile_rows(aim_ref, k) for k in range(nk)]

    per = 8 // nb

    def step(i, carry):
        r = pl.ds(pl.multiple_of(i * 8, 8), 8)
        s_in = [s_ref[k, r, :] for k in range(2 * nk)]
        before = [[] for _ in range(2 * nk)]
        for u in range(per):
            new = []
            for k in range(nk):
                xr, xi = carry[2 * k], carry[2 * k + 1]
                before[k].append(xr)
                before[nk + k].append(xi)
                rows = slice(u * nb, (u + 1) * nb)
                new.append(ar[k] * xr - ai[k] * xi + s_in[k][rows])
                new.append(ar[k] * xi + ai[k] * xr + s_in[nk + k][rows])
            carry = tuple(new)
        for k in range(2 * nk):
            xp_ref[k, r, :] = jnp.concatenate(before[k], axis=0)
        return carry

    carry = lax.fori_loop(0, nc // per, step,
                          tuple(carry_ref[k, 0:nb, :] for k in range(2 * nk)))
    for k in range(2 * nk):
        carry_ref[k, 0:nb, :] = carry[k]

    xp = jnp.concatenate(
        [jnp.concatenate([xp_ref[k, pl.ds(b, nc, stride=nb), :] for b in range(nb)], axis=0)
         for k in range(2 * nk)], axis=-1)
    tw = 2 * LANES
    y_intra = jnp.concatenate(
        [jnp.dot(lhs[:, :t0 + tw], m_ref[:t0 + tw, t0:t0 + tw], preferred_element_type=F32)
         for t0 in range(0, lc * LANES, tw)], axis=-1)
    y = y_intra + lax.dot_general(xp.astype(BF16), wyt_ref[...], (((1,), (1,)), ((), ())),
                                  preferred_element_type=F32)
    for b in range(nb):
        for t in range(lc):
            stage_ref[b, pl.ds(t, nc, stride=lc), :] = (
                y[b * nc:(b + 1) * nc, t * LANES:(t + 1) * LANES])
    yy = stage_ref[...] + d_ref[...] * x_ref[...]
    o_ref[...] = jax.nn.gelu(yy).astype(o_ref.dtype)


def s5_scan(hn, tables, d_skip, *, tc, lc=S5_CHUNK):
    kb, prh, pih, bd, cd, a_re, a_im = tables
    ng = GROUPS_PER_TILE
    nb, l, d = hn.shape
    nt = d // LANES
    nc = tc // lc
    ns = a_re.shape[-1]
    rows = nb * nc
    body = functools.partial(_s5_body, lc=lc, nb=nb, nc=nc, ns=ns)
    tile4 = lambda j, i: (j, 0, 0, 0)
    tile3 = lambda j, i: (j, 0, 0)
    return pl.pallas_call(
        body,
        grid=(nt, l // tc),
        in_specs=[pl.BlockSpec((nb, tc, LANES), lambda j, i: (0, i, j)),
                  pl.BlockSpec((1, lc, LANES, LANES), tile4),
                  pl.BlockSpec((1, lc + 1, ng, LANES), tile4),
                  pl.BlockSpec((1, lc + 1, ng, LANES), tile4),
                  pl.BlockSpec((1, ng, S5_GROUP, 2 * LANES), tile4),
                  pl.BlockSpec((1, ng, S5_GROUP, 2 * LANES), tile4),
                  pl.BlockSpec((1, 1, ns), tile3),
                  pl.BlockSpec((1, 1, ns), tile3),
                  pl.BlockSpec((1, 1, LANES), lambda j, i: (0, 0, j))],
        out_specs=pl.BlockSpec((nb, tc, LANES), lambda j, i: (0, i, j)),
        out_shape=jax.ShapeDtypeStruct((nb, l, d), BF16),
        scratch_shapes=[pltpu.VMEM((lc * LANES, lc * LANES), BF16),
                        pltpu.VMEM((lc * LANES, 2 * ns), BF16),
                        pltpu.VMEM((lc * LANES, 2 * ns), BF16),
                        pltpu.VMEM((rows, lc * LANES), BF16),
                        pltpu.VMEM((2 * ns // LANES, rows, LANES), F32),
                        pltpu.VMEM((2 * ns // LANES, rows, LANES), F32),
                        pltpu.VMEM((nb, tc, LANES), F32),
                        pltpu.VMEM((2 * ns // LANES, 8, LANES), F32)],
        compiler_params=_params(2),
        name="s5_scan",
    )(hn, kb, prh, pih, bd, cd, a_re, a_im, d_skip.reshape(1, 1, d).astype(F32))


def _rope(x, cos, sin_signed, first_half):
    w = x.shape[1]
    reps = w // LANES
    cosw = jnp.tile(cos, (1, reps))
    sinw = jnp.tile(sin_signed, (1, reps))
    sel = jnp.tile(first_half, (1, reps))
    half = HEAD_DIM // 2
    swapped = jnp.where(sel, pltpu.roll(x, w - half, 1), pltpu.roll(x, half, 1))
    return x * cosw + swapped * sinw


def _attn_body(sink_ref, q_ref, kc_ref, kp_ref, vc_ref, vp_ref, o_ref, *, q_per_kv):
    n = pl.program_id(1)
    blk = ATTN_BLOCK
    q = q_ref[...]
    k = jnp.concatenate([kp_ref[...], kc_ref[...]], axis=0).astype(F32)
    v = jnp.concatenate([vp_ref[...], vc_ref[...]], axis=0).astype(F32)

    qi = lax.broadcasted_iota(jnp.int32, (blk, 2 * blk), 0)
    si = lax.broadcasted_iota(jnp.int32, (blk, 2 * blk), 1)
    diff = blk + qi - si
    mask = (diff >= 0) & (diff < blk) & ((si >= blk) | (n > 0))
    bias = jnp.where(mask, 0.0, -jnp.inf).astype(F32)

    low = lax.broadcasted_iota(jnp.int32, (2 * blk, LANES), 1) < HEAD_DIM

    def split(tile, head_in_low):
        moved = pltpu.roll(tile, HEAD_DIM, 1)
        in_lo, in_hi = (tile, moved) if head_in_low else (moved, tile)
        return (jnp.where(low, in_lo, 0.0).astype(BF16),
                jnp.where(low, 0.0, in_hi).astype(BF16))

    for h in range(N_KV_HEADS):
        tile = slice((h // 2) * LANES, (h // 2 + 1) * LANES)
        k_lo, k_hi = split(k[:, tile], h % 2 == 0)
        v_lo, v_hi = split(v[:, tile], h % 2 == 0)
        for i in range(q_per_kv // 2):
            head = h * q_per_kv + 2 * i
            qt = q[:, (head // 2) * LANES:(head // 2 + 1) * LANES]
            acc = None
            for kk, vv, a in ((k_lo, v_lo, head), (k_hi, v_hi, head + 1)):
                s = lax.dot_general(qt, kk, (((1,), (1,)), ((), ())),
                                    preferred_element_type=F32) + bias
                sink = sink_ref[a] * LOG2E
                m = jnp.maximum(jnp.max(s, axis=-1, keepdims=True), sink)
                e = jnp.exp2(s - m)
                denom = jnp.sum(e, axis=-1, keepdims=True) + jnp.exp2(sink - m)
                o = jnp.dot(e.astype(BF16), vv, preferred_element_type=F32) * (1.0 / denom)
                acc = o if acc is None else acc + o
            o_ref[:, (head // 2) * LANES:(head // 2 + 1) * LANES] = acc.astype(o_ref.dtype)


def _qkv_rope_body(x_ref, w_ref, b_ref, pos_ref, invf_ref, o_ref, cos_ref, sin_ref,
                   *, q_tiles, rope_tiles):
    j = pl.program_id(1)
    lane = lax.broadcasted_iota(jnp.int32, cos_ref.shape, 1)
    first_half = (lane % HEAD_DIM) < (HEAD_DIM // 2)

    @pl.when(j == 0)
    def _():
        ang = pos_ref[...].astype(F32) * invf_ref[...]
        sin = jnp.sin(ang)
        cos_ref[...] = jnp.cos(ang)
        sin_ref[...] = jnp.where(first_half, -sin, sin)

    w = w_ref[...].astype(BF16)
    scale = jnp.where(j < q_tiles, HEAD_DIM ** -0.5 * LOG2E, 1.0)
    chunk = x_ref.shape[0] // ROPE_ROW_CHUNKS
    lane_c = lax.broadcasted_iota(jnp.int32, (chunk, LANES), 1)
    first_half_c = (lane_c % HEAD_DIM) < (HEAD_DIM // 2)
    for r in range(ROPE_ROW_CHUNKS):
        rows = slice(r * chunk, (r + 1) * chunk)
        acc = jnp.dot(x_ref[rows, :], w, preferred_element_type=F32) + b_ref[...]
        rot = _rope(acc, cos_ref[rows, :], sin_ref[rows, :], first_half_c) * scale
        o_ref[rows, :] = jnp.where(j < rope_tiles, rot, acc).astype(o_ref.dtype)


def qkv_rope_proj(x, w, b, positions, *, layer, bm, bn):
    m, k = x.shape
    n = w.shape[2]
    kvw = N_KV_HEADS * HEAD_DIM
    qw = n - 2 * kvw
    assert qw % bn == 0 and kvw % bn == 0
    half = HEAD_DIM // 2
    inv_freq = jnp.power(ROPE_THETA, -jnp.arange(half, dtype=F32) / half)
    invf = jnp.tile(inv_freq, LANES // half).reshape(1, LANES)
    pos = positions.reshape(m, 1).astype(jnp.int32)
    body = functools.partial(_qkv_rope_body, q_tiles=qw // bn, rope_tiles=(qw + kvw) // bn)
    return pl.pallas_call(
        body,
        grid=(m // bm, n // bn),
        in_specs=[pl.BlockSpec((bm, k), lambda i, j: (i, 0)),
                  _layer_spec(k, bn, layer),
                  _layer_spec(1, bn, layer),
                  pl.BlockSpec((bm, 1), lambda i, j: (i, 0)),
                  pl.BlockSpec((1, LANES), lambda i, j: (0, 0))],
        out_specs=pl.BlockSpec((bm, bn), lambda i, j: (i, j)),
        out_shape=jax.ShapeDtypeStruct((m, n), BF16),
        scratch_shapes=[pltpu.VMEM((bm, LANES), F32), pltpu.VMEM((bm, LANES), F32)],
        compiler_params=_params(2),
        name="qkv_rope_proj",
    )(x, w, b.reshape(b.shape[0], 1, n), pos, invf)


def swa_attention(qkv, sinks, *, batch, seq):
    t, width = qkv.shape
    kvw = N_KV_HEADS * HEAD_DIM
    qw = width - 2 * kvw
    n_q_heads = qw // HEAD_DIM
    q_per_kv = n_q_heads // N_KV_HEADS
    assert q_per_kv % 2 == 0 and 2 * HEAD_DIM == LANES
    blk = ATTN_BLOCK
    nblk = seq // blk
    cur = lambda b, n: (b * nblk + n, 0)
    kcol = qw // kvw
    body = functools.partial(_attn_body, q_per_kv=q_per_kv)
    return pl.pallas_call(
        body,
        grid=(batch, nblk),
        in_specs=[pl.BlockSpec(memory_space=pltpu.SMEM),
                  pl.BlockSpec((blk, qw), cur),
                  pl.BlockSpec((blk, kvw), lambda b, n: (b * nblk + n, kcol)),
                  pl.BlockSpec((blk, kvw), lambda b, n: (b * nblk + jnp.maximum(n - 1, 0), kcol)),
                  pl.BlockSpec((blk, kvw), lambda b, n: (b * nblk + n, kcol + 1)),
                  pl.BlockSpec((blk, kvw), lambda b, n: (b * nblk + jnp.maximum(n - 1, 0), kcol + 1))],
        out_specs=pl.BlockSpec((blk, qw), cur),
        out_shape=jax.ShapeDtypeStruct((t, qw), BF16),
        compiler_params=_params(2),
        name="swa_attention",
    )(sinks.astype(F32), qkv, qkv, qkv, qkv, qkv)


def _ffn(hn, w_gate, w_up, w_down, layer):
    act = swiglu_up(hn, w_gate, w_up, layer=layer, bm=2048, bn=256)
    return matmul(act, w_down, layer=layer, bm=512, bn=512, name="ffn_down")


def kernel(x, positions, norm_pre_mix, norm_post_mix, norm_pre_ffn, norm_post_ffn, s5_lam_re, s5_lam_im, s5_log_step, s5_b_re, s5_b_im, s5_c_re, s5_c_im, s5_d, s5_w_out1, s5_b_out1, s5_w_out2, s5_b_out2, attn_w_qkv, attn_b_qkv, attn_w_o, attn_b_o, attn_sinks, ffn_w_gate, ffn_w_up, ffn_w_down):
    bsz, seq, d = x.shape
    t = bsz * seq
    h = x.reshape(t, d)

    hn = rmsnorm(h, norm_pre_mix[0], F32)
    tables = _s5_tables(s5_lam_re[0], s5_lam_im[0], s5_log_step[0], s5_b_re[0], s5_b_im[0],
                        s5_c_re[0], s5_c_im[0], S5_CHUNK)
    g = s5_scan(hn.reshape(bsz, seq, d), tables, s5_d[0], tc=2048).reshape(t, d)
    m = glu_matmul(g, s5_w_out1, s5_b_out1, s5_w_out2, s5_b_out2, layer=0, bm=1024, bn=512)
    hn = mix_norm(h, m, norm_post_mix[0], norm_pre_ffn[0])
    w_down = ffn_w_down.astype(BF16)
    f = _ffn(hn, ffn_w_gate, ffn_w_up, w_down, 0)
    h, hn = block_resid_norm(h, m, f, norm_post_mix[0], norm_post_ffn[0], norm_pre_mix[1])

    qkv = qkv_rope_proj(hn, attn_w_qkv, attn_b_qkv, positions, layer=0, bm=1024, bn=512)
    att = swa_attention(qkv, attn_sinks[0], batch=bsz, seq=seq)
    m = matmul(att, attn_w_o, attn_b_o, layer=0, bm=1024, bn=512, name="o_proj")
    hn = mix_norm(h, m, norm_post_mix[1], norm_pre_ffn[1])
    f = _ffn(hn, ffn_w_gate, ffn_w_up, w_down, 1)
    h = block_resid(h, m, f, norm_post_mix[1], norm_post_ffn[1])
    return h.reshape(bsz, seq, d)
```

```python
import functools
import math

import jax
import jax.numpy as jnp
from jax import lax
from jax.experimental import pallas as pl
from jax.experimental.pallas import tpu as pltpu

F32 = jnp.float32
BF16 = jnp.bfloat16

EPS = 1e-6
LANES = 128
S5_GROUP = 16
GROUPS_PER_TILE = LANES // S5_GROUP
S5_CHUNK = 8
HEAD_DIM = 64
N_KV_HEADS = 8
ATTN_BLOCK = 128
ROPE_THETA = 10000.0
ROPE_ROW_CHUNKS = 4
LOG2E = math.log2(math.e)
VMEM_LIMIT = 60 * 1024 * 1024


def _params(n_axes, vmem=VMEM_LIMIT):
    return pltpu.CompilerParams(
        dimension_semantics=("arbitrary",) * n_axes, vmem_limit_bytes=vmem)


def _rms(x, g):
    r = lax.rsqrt(jnp.mean(x * x, axis=-1, keepdims=True) + EPS)
    return x * r * g


def _rmsnorm_body(x_ref, g_ref, o_ref):
    o_ref[...] = _rms(x_ref[...], g_ref[...]).astype(o_ref.dtype)


def _row_call(body, n_rows_in, n_vec_in, out_dtypes, t, d, name):
    streams = n_rows_in + len(out_dtypes)
    rows = 512 if streams <= 3 else 256
    row_spec = pl.BlockSpec((rows, d), lambda i: (i, 0))
    vec_spec = pl.BlockSpec((1, d), lambda i: (0, 0))
    return pl.pallas_call(
        body,
        grid=(t // rows,),
        in_specs=[row_spec] * n_rows_in + [vec_spec] * n_vec_in,
        out_specs=[row_spec] * len(out_dtypes),
        out_shape=[jax.ShapeDtypeStruct((t, d), dt) for dt in out_dtypes],
        compiler_params=_params(1),
        name=name,
    )


def rmsnorm(x, g, out_dtype):
    t, d = x.shape
    return _row_call(_rmsnorm_body, 1, 1, [out_dtype], t, d, "rmsnorm")(x, g.reshape(1, d))[0]


def _mix_norm_body(h_ref, m_ref, gm_ref, gpre_ref, hn_ref):
    h = h_ref[...] + _rms(m_ref[...].astype(F32), gm_ref[...])
    hn_ref[...] = _rms(h, gpre_ref[...]).astype(hn_ref.dtype)


def mix_norm(h, m, g_m, g_pre):
    t, d = h.shape
    return _row_call(_mix_norm_body, 2, 2, [BF16], t, d, "mix_norm")(
        h, m, g_m.reshape(1, d), g_pre.reshape(1, d))[0]


def _block_resid_norm_body(h_ref, m_ref, f_ref, gm_ref, gf_ref, gpre_ref, hout_ref, hn_ref):
    h = h_ref[...] + _rms(m_ref[...].astype(F32), gm_ref[...])
    h = h + _rms(f_ref[...].astype(F32), gf_ref[...])
    hout_ref[...] = h
    hn_ref[...] = _rms(h, gpre_ref[...]).astype(hn_ref.dtype)


def block_resid_norm(h, m, f, g_m, g_f, g_pre):
    t, d = h.shape
    return _row_call(_block_resid_norm_body, 3, 3, [F32, BF16], t, d, "block_resid_norm")(
        h, m, f, g_m.reshape(1, d), g_f.reshape(1, d), g_pre.reshape(1, d))


def _block_resid_body(h_ref, m_ref, f_ref, gm_ref, gf_ref, hout_ref):
    h = h_ref[...] + _rms(m_ref[...].astype(F32), gm_ref[...])
    hout_ref[...] = h + _rms(f_ref[...].astype(F32), gf_ref[...])


def block_resid(h, m, f, g_m, g_f):
    t, d = h.shape
    return _row_call(_block_resid_body, 3, 2, [F32], t, d, "block_resid")(
        h, m, f, g_m.reshape(1, d), g_f.reshape(1, d))[0]


def _mm_bias_body(x_ref, w_ref, b_ref, o_ref):
    acc = jnp.dot(x_ref[...], w_ref[...].astype(BF16), preferred_element_type=F32)
    o_ref[...] = (acc + b_ref[...]).astype(o_ref.dtype)


def _mm_body(x_ref, w_ref, o_ref):
    o_ref[...] = jnp.dot(x_ref[...], w_ref[...].astype(BF16),
                         preferred_element_type=F32).astype(o_ref.dtype)


def _layer_spec(rows, bn, layer):
    return pl.BlockSpec((None, rows, bn), lambda i, j: (layer, 0, j))


def matmul(x, w, b=None, *, layer, bm, bn, out_dtype=BF16, name="matmul"):
    m, k = x.shape
    n = w.shape[2]
    in_specs = [pl.BlockSpec((bm, k), lambda i, j: (i, 0)), _layer_spec(k, bn, layer)]
    args = [x, w]
    body = _mm_body
    if b is not None:
        in_specs.append(_layer_spec(1, bn, layer))
        args.append(b.reshape(b.shape[0], 1, n))
        body = _mm_bias_body
    return pl.pallas_call(
        body,
        grid=(m // bm, n // bn),
        in_specs=in_specs,
        out_specs=pl.BlockSpec((bm, bn), lambda i, j: (i, j)),
        out_shape=jax.ShapeDtypeStruct((m, n), out_dtype),
        compiler_params=_params(2),
        name=name,
    )(*args)


def _glu_body(x_ref, w1_ref, b1_ref, w2_ref, b2_ref, o_ref):
    x = x_ref[...]
    a = jnp.dot(x, w1_ref[...].astype(BF16), preferred_element_type=F32) + b1_ref[...]
    b = jnp.dot(x, w2_ref[...].astype(BF16), preferred_element_type=F32) + b2_ref[...]
    o_ref[...] = (a * jax.nn.sigmoid(b)).astype(o_ref.dtype)


def glu_matmul(x, w1, b1, w2, b2, *, layer, bm, bn):
    m, k = x.shape
    n = w1.shape[2]
    w_spec = _layer_spec(k, bn, layer)
    b_spec = _layer_spec(1, bn, layer)
    return pl.pallas_call(
        _glu_body,
        grid=(m // bm, n // bn),
        in_specs=[pl.BlockSpec((bm, k), lambda i, j: (i, 0)),
                  w_spec, b_spec, w_spec, b_spec],
        out_specs=pl.BlockSpec((bm, bn), lambda i, j: (i, j)),
        out_shape=jax.ShapeDtypeStruct((m, n), BF16),
        compiler_params=_params(2),
        name="glu_matmul",
    )(x, w1, b1.reshape(b1.shape[0], 1, n), w2, b2.reshape(b2.shape[0], 1, n))


def _swiglu_up_body(x_ref, wg_ref, wu_ref, wd_ref, o_ref, wd_out_ref):
    x = x_ref[...]
    a = jnp.dot(x, wg_ref[...].astype(BF16), preferred_element_type=F32)
    b = jnp.dot(x, wu_ref[...].astype(BF16), preferred_element_type=F32)
    o_ref[...] = (jax.nn.silu(a) * b).astype(o_ref.dtype)
    wd_out_ref[...] = wd_ref[...].astype(wd_out_ref.dtype)


def swiglu_up(x, wg, wu, wd, *, layer, bm, bn):
    m, k = x.shape
    n = wg.shape[2]
    steps_j = n // bn
    steps = (m // bm) * steps_j
    layers, dk, dn = wd.shape
    slab = dk // steps
    assert slab * steps == dk and slab % 16 == 0
    w_spec = _layer_spec(k, bn, layer)
    x_spec = pl.BlockSpec((bm, k), lambda i, j: (i, 0), pipeline_mode=pl.Buffered(1))
    act, wd_bf16 = pl.pallas_call(
        _swiglu_up_body,
        grid=(m // bm, steps_j),
        in_specs=[x_spec, w_spec, w_spec,
                  pl.BlockSpec((None, None, slab, dn), lambda i, j: (layer, i * steps_j + j, 0, 0))],
        out_specs=[pl.BlockSpec((bm, bn), lambda i, j: (i, j)),
                   pl.BlockSpec((None, slab, dn), lambda i, j: (i * steps_j + j, 0, 0))],
        out_shape=[jax.ShapeDtypeStruct((m, n), BF16),
                   jax.ShapeDtypeStruct((steps, slab, dn), BF16)],
        compiler_params=_params(2),
        name="swiglu_up",
    )(x, wg, wu, wd.reshape(layers, steps, slab, dn))
    return act, wd_bf16.reshape(1, dk, dn)


def _s5_tables(lam_re, lam_im, log_step, b_re, b_im, c_re, c_im, lc):
    g, p = lam_re.shape
    ch = b_re.shape[-1]
    assert ch == S5_GROUP and 2 * p == LANES
    ng = GROUPS_PER_TILE
    nt = g // ng
    hi = lax.Precision.HIGHEST
    lr, li = lam_re.astype(F32), lam_im.astype(F32)
    step = jnp.exp(log_step.astype(F32))[:, None]
    zr, zi = lr * step, li * step
    taus = jnp.arange(lc + 1, dtype=F32)[:, None, None]
    mag = jnp.exp(zr[None] * taus)
    pr = mag * jnp.cos(zi[None] * taus)
    pi = mag * jnp.sin(zi[None] * taus)
    nr, ni = pr[1] - 1.0, pi[1]
    den = lr * lr + li * li
    qr = (nr * lr + ni * li) / den
    qi = (ni * lr - nr * li) / den
    b_r = jnp.swapaxes(b_re.astype(F32), 1, 2)
    b_i = jnp.swapaxes(b_im.astype(F32), 1, 2)
    br = qr[:, None, :] * b_r - qi[:, None, :] * b_i
    bi = qr[:, None, :] * b_i + qi[:, None, :] * b_r
    cr, ci = c_re.astype(F32), c_im.astype(F32)
    w_re = pr[:lc, :, None, :] * br[None] - pi[:lc, :, None, :] * bi[None]
    w_im = pr[:lc, :, None, :] * bi[None] + pi[:lc, :, None, :] * br[None]
    k = (jnp.einsum('gop,tgcp->tgoc', cr, w_re, precision=hi)
         - jnp.einsum('gop,tgcp->tgoc', ci, w_im, precision=hi))
    eye = jnp.eye(ng, dtype=F32)
    kb = jnp.einsum('tjaoc,ab->jtacbo', k.reshape(lc, nt, ng, ch, ch), eye).reshape(
        nt, lc, LANES, LANES)

    zero = jnp.zeros_like(pr)
    even = (jnp.arange(g) % 2 == 0)[None, :, None]
    halves = lambda a: jnp.concatenate(
        [jnp.where(even, a, zero), jnp.where(even, zero, a)], axis=-1)
    per_tile = lambda a: jnp.swapaxes(a.reshape(lc + 1, nt, ng, a.shape[-1]), 0, 1)
    grp = lambda a: a.reshape((nt, ng) + a.shape[1:])
    cat = lambda *xs: jnp.concatenate(xs, axis=-1)
    return (kb.astype(BF16), per_tile(halves(pr)), per_tile(halves(pi)),
            grp(cat(br, br, bi, bi)), grp(cat(cr, cr, ci, ci)),
            pr[lc].reshape(nt, 1, ng * p), pi[lc].reshape(nt, 1, ng * p))


def _s5_build(kb_ref, prh_ref, pih_ref, bd_ref, cd_ref, m_ref, ws_ref, wyt_ref, lc, nk):
    zero = jnp.zeros((LANES, LANES), BF16)
    for s in range(lc):
        for t in range(lc):
            m_ref[s * LANES:(s + 1) * LANES, t * LANES:(t + 1) * LANES] = (
                kb_ref[0, t - s] if t >= s else zero)
    ws_ref[...] = jnp.zeros_like(ws_ref)
    wyt_ref[...] = jnp.zeros_like(wyt_ref)
    for a in range(GROUPS_PER_TILE):
        bd, cd = bd_ref[0, a], cd_ref[0, a]
        b_r, b_i = bd[:, :LANES], bd[:, LANES:]
        c_r, c_i = cd[:, :LANES], cd[:, LANES:]
        re_tile = slice((a // 2) * LANES, (a // 2 + 1) * LANES)
        im_tile = slice((nk + a // 2) * LANES, (nk + a // 2 + 1) * LANES)
        for s in range(lc):
            rows = slice(s * LANES + a * S5_GROUP, s * LANES + (a + 1) * S5_GROUP)
            p_r, p_i = prh_ref[0, lc - 1 - s, a:a + 1, :], pih_ref[0, lc - 1 - s, a:a + 1, :]
            ws_ref[rows, re_tile] = (p_r * b_r - p_i * b_i).astype(BF16)
            ws_ref[rows, im_tile] = (p_r * b_i + p_i * b_r).astype(BF16)
            p_r, p_i = prh_ref[0, s + 1, a:a + 1, :], pih_ref[0, s + 1, a:a + 1, :]
            wyt_ref[rows, re_tile] = (p_r * c_r - p_i * c_i).astype(BF16)
            wyt_ref[rows, im_tile] = (-(p_r * c_i + p_i * c_r)).astype(BF16)


def _s5_body(x_ref, kb_ref, prh_ref, pih_ref, bd_ref, cd_ref, are_ref, aim_ref, d_ref, o_ref,
             m_ref, ws_ref, wyt_ref, lhs_ref, yi_ref, s_ref, xp_ref, stage_ref, carry_ref,
             *, lc, nb, nc, ns):
    nk = ns // LANES

    @pl.when(pl.program_id(1) == 0)
    def _():
        _s5_build(kb_ref, prh_ref, pih_ref, bd_ref, cd_ref, m_ref, ws_ref, wyt_ref, lc, nk)
        carry_ref[...] = jnp.zeros_like(carry_ref)

    tw = 2 * LANES
    for b in range(nb):
        rows_b = slice(b * nc, (b + 1) * nc)
        for s in range(lc):
            piece = x_ref[b, pl.ds(s, nc, stride=lc), :]
            lhs_ref[rows_b, s * LANES:(s + 1) * LANES] = piece.astype(BF16)
        lhs = lhs_ref[rows_b, :]
        s_b = jnp.dot(lhs, ws_ref[...], preferred_element_type=F32)
        for k in range(2 * nk):
            s_ref[k, pl.ds(b, nc, stride=nb), :] = s_b[:, k * LANES:(k + 1) * LANES]
        for t0 in range(0, lc * LANES, tw):
            yi_ref[rows_b, t0:t0 + tw] = jnp.dot(
                lhs[:, :t0 + tw], m_ref[:t0 + tw, t0:t0 + tw], preferred_element_type=F32)

    tile_rows = lambda ref, k: jnp.broadcast_to(ref[0, :, k * LANES:(k + 1) * LANES], (nb, LANES))
    ar = [tile_rows(are_ref, k) for k in range(nk)]
    ai = [tile_rows(aim_ref, k) for k in range(nk)]

    per = 8 // nb

    def step(i, carry):
        r = pl.ds(pl.multiple_of(i * 8, 8), 8)
        s_in = [s_ref[k, r, :] for k in range(2 * nk)]
        before = [[] for _ in range(2 * nk)]
        for u in range(per):
            new = []
            for k in range(nk):
                xr, xi = carry[2 * k], carry[2 * k + 1]
                before[k].append(xr)
                before[nk + k].append(xi)
                rows = slice(u * nb, (u + 1) * nb)
                new.append(ar[k] * xr - ai[k] * xi + s_in[k][rows])
                new.append(ar[k] * xi + ai[k] * xr + s_in[nk + k][rows])
            carry = tuple(new)
        for k in range(2 * nk):
            xp_ref[k, r, :] = jnp.concatenate(before[k], axis=0)
        return carry

    carry = lax.fori_loop(0, nc // per, step,
                          tuple(carry_ref[k, 0:nb, :] for k in range(2 * nk)))
    for k in range(2 * nk):
        carry_ref[k, 0:nb, :] = carry[k]

    for b in range(nb):
        xp = jnp.concatenate(
            [xp_ref[k, pl.ds(b, nc, stride=nb), :] for k in range(2 * nk)], axis=-1)
        y = yi_ref[b * nc:(b + 1) * nc, :] + lax.dot_general(
            xp.astype(BF16), wyt_ref[...], (((1,), (1,)), ((), ())), preferred_element_type=F32)
        for t in range(lc):
            stage_ref[b, pl.ds(t, nc, stride=lc), :] = y[:, t * LANES:(t + 1) * LANES]
        yy = stage_ref[b] + d_ref[0] * x_ref[b]
        o_ref[b] = jax.nn.gelu(yy).astype(o_ref.dtype)


def s5_scan(hn, tables, d_skip, *, tc, lc=S5_CHUNK):
    kb, prh, pih, bd, cd, a_re, a_im = tables
    ng = GROUPS_PER_TILE
    nb, l, d = hn.shape
    nt = d // LANES
    nc = tc // lc
    ns = a_re.shape[-1]
    rows = nb * nc
    body = functools.partial(_s5_body, lc=lc, nb=nb, nc=nc, ns=ns)
    tile4 = lambda j, i: (j, 0, 0, 0)
    tile3 = lambda j, i: (j, 0, 0)
    return pl.pallas_call(
        body,
        grid=(nt, l // tc),
        in_specs=[pl.BlockSpec((nb, tc, LANES), lambda j, i: (0, i, j)),
                  pl.BlockSpec((1, lc, LANES, LANES), tile4),
                  pl.BlockSpec((1, lc + 1, ng, LANES), tile4),
                  pl.BlockSpec((1, lc + 1, ng, LANES), tile4),
                  pl.BlockSpec((1, ng, S5_GROUP, 2 * LANES), tile4),
                  pl.BlockSpec((1, ng, S5_GROUP, 2 * LANES), tile4),
                  pl.BlockSpec((1, 1, ns), tile3),
                  pl.BlockSpec((1, 1, ns), tile3),
                  pl.BlockSpec((1, 1, LANES), lambda j, i: (0, 0, j))],
        out_specs=pl.BlockSpec((nb, tc, LANES), lambda j, i: (0, i, j)),
        out_shape=jax.ShapeDtypeStruct((nb, l, d), BF16),
        scratch_shapes=[pltpu.VMEM((lc * LANES, lc * LANES), BF16),
                        pltpu.VMEM((lc * LANES, 2 * ns), BF16),
                        pltpu.VMEM((lc * LANES, 2 * ns), BF16),
                        pltpu.VMEM((rows, lc * LANES), BF16),
                        pltpu.VMEM((rows, lc * LANES), F32),
                        pltpu.VMEM((2 * ns // LANES, rows, LANES), F32),
                        pltpu.VMEM((2 * ns // LANES, rows, LANES), F32),
                        pltpu.VMEM((nb, tc, LANES), F32),
                        pltpu.VMEM((2 * ns // LANES, 8, LANES), F32)],
        compiler_params=_params(2),
        name="s5_scan",
    )(hn, kb, prh, pih, bd, cd, a_re, a_im, d_skip.reshape(1, 1, d).astype(F32))


def _rope(x, cos, sin_signed, first_half):
    w = x.shape[1]
    reps = w // LANES
    cosw = jnp.tile(cos, (1, reps))
    sinw = jnp.tile(sin_signed, (1, reps))
    sel = jnp.tile(first_half, (1, reps))
    half = HEAD_DIM // 2
    swapped = jnp.where(sel, pltpu.roll(x, w - half, 1), pltpu.roll(x, half, 1))
    return x * cosw + swapped * sinw


def _attn_body(sink_ref, q_ref, kc_ref, kp_ref, vc_ref, vp_ref, o_ref, *, q_per_kv):
    n = pl.program_id(1)
    blk = ATTN_BLOCK
    q = q_ref[...]
    k = jnp.concatenate([kp_ref[...], kc_ref[...]], axis=0).astype(F32)
    v = jnp.concatenate([vp_ref[...], vc_ref[...]], axis=0).astype(F32)

    qi = lax.broadcasted_iota(jnp.int32, (blk, 2 * blk), 0)
    si = lax.broadcasted_iota(jnp.int32, (blk, 2 * blk), 1)
    diff = blk + qi - si
    mask = (diff >= 0) & (diff < blk) & ((si >= blk) | (n > 0))
    bias = jnp.where(mask, 0.0, -jnp.inf).astype(F32)

    low = lax.broadcasted_iota(jnp.int32, (2 * blk, LANES), 1) < HEAD_DIM

    def split(tile, head_in_low):
        moved = pltpu.roll(tile, HEAD_DIM, 1)
        in_lo, in_hi = (tile, moved) if head_in_low else (moved, tile)
        return (jnp.where(low, in_lo, 0.0).astype(BF16),
                jnp.where(low, 0.0, in_hi).astype(BF16))

    for h in range(N_KV_HEADS):
        tile = slice((h // 2) * LANES, (h // 2 + 1) * LANES)
        k_lo, k_hi = split(k[:, tile], h % 2 == 0)
        v_lo, v_hi = split(v[:, tile], h % 2 == 0)
        for i in range(q_per_kv // 2):
            head = h * q_per_kv + 2 * i
            qt = q[:, (head // 2) * LANES:(head // 2 + 1) * LANES]
            acc = None
            for kk, vv, a in ((k_lo, v_lo, head), (k_hi, v_hi, head + 1)):
                s = lax.dot_general(qt, kk, (((1,), (1,)), ((), ())),
                                    preferred_element_type=F32) + bias
                sink = sink_ref[a] * LOG2E
                m = jnp.maximum(jnp.max(s, axis=-1, keepdims=True), sink)
                e = jnp.exp2(s - m)
                denom = jnp.sum(e, axis=-1, keepdims=True) + jnp.exp2(sink - m)
                o = jnp.dot(e.astype(BF16), vv, preferred_element_type=F32) * (1.0 / denom)
                acc = o if acc is None else acc + o
            o_ref[:, (head // 2) * LANES:(head // 2 + 1) * LANES] = acc.astype(o_ref.dtype)


def _qkv_rope_body(x_ref, w_ref, b_ref, pos_ref, invf_ref, o_ref, cos_ref, sin_ref,
                   *, q_tiles, rope_tiles):
    j = pl.program_id(1)
    lane = lax.broadcasted_iota(jnp.int32, cos_ref.shape, 1)
    first_half = (lane % HEAD_DIM) < (HEAD_DIM // 2)

    @pl.when(j == 0)
    def _():
        ang = pos_ref[...].astype(F32) * invf_ref[...]
        sin = jnp.sin(ang)
        cos_ref[...] = jnp.cos(ang)
        sin_ref[...] = jnp.where(first_half, -sin, sin)

    w = w_ref[...].astype(BF16)
    scale = jnp.where(j < q_tiles, HEAD_DIM ** -0.5 * LOG2E, 1.0)
    chunk = x_ref.shape[0] // ROPE_ROW_CHUNKS
    lane_c = lax.broadcasted_iota(jnp.int32, (chunk, LANES), 1)
    first_half_c = (lane_c % HEAD_DIM) < (HEAD_DIM // 2)
    for r in range(ROPE_ROW_CHUNKS):
        rows = slice(r * chunk, (r + 1) * chunk)
        acc = jnp.dot(x_ref[rows, :], w, preferred_element_type=F32) + b_ref[...]
        rot = _rope(acc, cos_ref[rows, :], sin_ref[rows, :], first_half_c) * scale
        o_ref[rows, :] = jnp.where(j < rope_tiles, rot, acc).astype(o_ref.dtype)


def qkv_rope_proj(x, w, b, positions, *, layer, bm, bn):
    m, k = x.shape
    n = w.shape[2]
    kvw = N_KV_HEADS * HEAD_DIM
    qw = n - 2 * kvw
    assert qw % bn == 0 and kvw % bn == 0
    half = HEAD_DIM // 2
    inv_freq = jnp.power(ROPE_THETA, -jnp.arange(half, dtype=F32) / half)
    invf = jnp.tile(inv_freq, LANES // half).reshape(1, LANES)
    pos = positions.reshape(m, 1).astype(jnp.int32)
    body = functools.partial(_qkv_rope_body, q_tiles=qw // bn, rope_tiles=(qw + kvw) // bn)
    return pl.pallas_call(
        body,
        grid=(m // bm, n // bn),
        in_specs=[pl.BlockSpec((bm, k), lambda i, j: (i, 0)),
                  _layer_spec(k, bn, layer),
                  _layer_spec(1, bn, layer),
                  pl.BlockSpec((bm, 1), lambda i, j: (i, 0)),
                  pl.BlockSpec((1, LANES), lambda i, j: (0, 0))],
        out_specs=pl.BlockSpec((bm, bn), lambda i, j: (i, j)),
        out_shape=jax.ShapeDtypeStruct((m, n), BF16),
        scratch_shapes=[pltpu.VMEM((bm, LANES), F32), pltpu.VMEM((bm, LANES), F32)],
        compiler_params=_params(2),
        name="qkv_rope_proj",
    )(x, w, b.reshape(b.shape[0], 1, n), pos, invf)


def swa_attention(qkv, sinks, *, batch, seq):
    t, width = qkv.shape
    kvw = N_KV_HEADS * HEAD_DIM
    qw = width - 2 * kvw
    n_q_heads = qw // HEAD_DIM
    q_per_kv = n_q_heads // N_KV_HEADS
    assert q_per_kv % 2 == 0 and 2 * HEAD_DIM == LANES
    blk = ATTN_BLOCK
    nblk = seq // blk
    cur = lambda b, n: (b * nblk + n, 0)
    kcol = qw // kvw
    body = functools.partial(_attn_body, q_per_kv=q_per_kv)
    return pl.pallas_call(
        body,
        grid=(batch, nblk),
        in_specs=[pl.BlockSpec(memory_space=pltpu.SMEM),
                  pl.BlockSpec((blk, qw), cur),
                  pl.BlockSpec((blk, kvw), lambda b, n: (b * nblk + n, kcol)),
                  pl.BlockSpec((blk, kvw), lambda b, n: (b * nblk + jnp.maximum(n - 1, 0), kcol)),
                  pl.BlockSpec((blk, kvw), lambda b, n: (b * nblk + n, kcol + 1)),
                  pl.BlockSpec((blk, kvw), lambda b, n: (b * nblk + jnp.maximum(n - 1, 0), kcol + 1))],
        out_specs=pl.BlockSpec((blk, qw), cur),
        out_shape=jax.ShapeDtypeStruct((t, qw), BF16),
        compiler_params=_params(2),
        name="swa_attention",
    )(sinks.astype(F32), qkv, qkv, qkv, qkv, qkv)


def _ffn(hn, w_gate, w_up, w_down, layer):
    act, w_down_bf16 = swiglu_up(hn, w_gate, w_up, w_down, layer=layer, bm=2048, bn=256)
    return matmul(act, w_down_bf16, layer=0, bm=512, bn=512, name="ffn_down")


def kernel(x, positions, norm_pre_mix, norm_post_mix, norm_pre_ffn, norm_post_ffn, s5_lam_re, s5_lam_im, s5_log_step, s5_b_re, s5_b_im, s5_c_re, s5_c_im, s5_d, s5_w_out1, s5_b_out1, s5_w_out2, s5_b_out2, attn_w_qkv, attn_b_qkv, attn_w_o, attn_b_o, attn_sinks, ffn_w_gate, ffn_w_up, ffn_w_down):
    bsz, seq, d = x.shape
    t = bsz * seq
    h = x.reshape(t, d)

    hn = rmsnorm(h, norm_pre_mix[0], F32)
    tables = _s5_tables(s5_lam_re[0], s5_lam_im[0], s5_log_step[0], s5_b_re[0], s5_b_im[0],
                        s5_c_re[0], s5_c_im[0], S5_CHUNK)
    g = s5_scan(hn.reshape(bsz, seq, d), tables, s5_d[0], tc=2048).reshape(t, d)
    m = glu_matmul(g, s5_w_out1, s5_b_out1, s5_w_out2, s5_b_out2, layer=0, bm=1024, bn=512)
    hn = mix_norm(h, m, norm_post_mix[0], norm_pre_ffn[0])
    f = _ffn(hn, ffn_w_gate, ffn_w_up, ffn_w_down, 0)
    h, hn = block_resid_norm(h, m, f, norm_post_mix[0], norm_post_ffn[0], norm_pre_mix[1])

    qkv = qkv_rope_proj(hn, attn_w_qkv, attn_b_qkv, positions, layer=0, bm=1024, bn=512)
    att = swa_attention(qkv, attn_sinks[0], batch=bsz, seq=seq)
    m = matmul(att, attn_w_o, attn_b_o, layer=0, bm=1024, bn=512, name="o_proj")
    hn = mix_norm(h, m, norm_post_mix[1], norm_pre_ffn[1])
    f = _ffn(hn, ffn_w_gate, ffn_w_up, ffn_w_down, 1)
    h = block_resid(h, m, f, norm_post_mix[1], norm_post_ffn[1])
    return h.reshape(bsz, seq, d)
```

```python
import functools
import math

import jax
import jax.numpy as jnp
from jax import lax
from jax.experimental import pallas as pl
from jax.experimental.pallas import tpu as pltpu

F32 = jnp.float32
BF16 = jnp.bfloat16

EPS = 1e-6
LANES = 128
S5_GROUP = 16
GROUPS_PER_TILE = LANES // S5_GROUP
S5_CHUNK = 8
HEAD_DIM = 64
N_KV_HEADS = 8
ATTN_BLOCK = 128
ROPE_THETA = 10000.0
ROPE_ROW_CHUNKS = 4
LOG2E = math.log2(math.e)
VMEM_LIMIT = 60 * 1024 * 1024


def _params(n_axes, vmem=VMEM_LIMIT):
    return pltpu.CompilerParams(
        dimension_semantics=("arbitrary",) * n_axes, vmem_limit_bytes=vmem)


def _rms(x, g):
    r = lax.rsqrt(jnp.mean(x * x, axis=-1, keepdims=True) + EPS)
    return x * r * g


def _rmsnorm_body(x_ref, g_ref, o_ref):
    o_ref[...] = _rms(x_ref[...], g_ref[...]).astype(o_ref.dtype)


def _row_call(body, n_rows_in, n_vec_in, out_dtypes, t, d, name):
    streams = n_rows_in + len(out_dtypes)
    rows = 512 if streams <= 3 else 256
    row_spec = pl.BlockSpec((rows, d), lambda i: (i, 0))
    vec_spec = pl.BlockSpec((1, d), lambda i: (0, 0))
    return pl.pallas_call(
        body,
        grid=(t // rows,),
        in_specs=[row_spec] * n_rows_in + [vec_spec] * n_vec_in,
        out_specs=[row_spec] * len(out_dtypes),
        out_shape=[jax.ShapeDtypeStruct((t, d), dt) for dt in out_dtypes],
        compiler_params=_params(1),
        name=name,
    )


def rmsnorm(x, g, out_dtype):
    t, d = x.shape
    return _row_call(_rmsnorm_body, 1, 1, [out_dtype], t, d, "rmsnorm")(x, g.reshape(1, d))[0]


def _mix_norm_body(h_ref, m_ref, gm_ref, gpre_ref, hn_ref):
    h = h_ref[...] + _rms(m_ref[...].astype(F32), gm_ref[...])
    hn_ref[...] = _rms(h, gpre_ref[...]).astype(hn_ref.dtype)


def mix_norm(h, m, g_m, g_pre):
    t, d = h.shape
    return _row_call(_mix_norm_body, 2, 2, [BF16], t, d, "mix_norm")(
        h, m, g_m.reshape(1, d), g_pre.reshape(1, d))[0]


def _block_resid_norm_body(h_ref, m_ref, f_ref, gm_ref, gf_ref, gpre_ref, hout_ref, hn_ref):
    h = h_ref[...] + _rms(m_ref[...].astype(F32), gm_ref[...])
    h = h + _rms(f_ref[...].astype(F32), gf_ref[...])
    hout_ref[...] = h
    hn_ref[...] = _rms(h, gpre_ref[...]).astype(hn_ref.dtype)


def block_resid_norm(h, m, f, g_m, g_f, g_pre):
    t, d = h.shape
    return _row_call(_block_resid_norm_body, 3, 3, [F32, BF16], t, d, "block_resid_norm")(
        h, m, f, g_m.reshape(1, d), g_f.reshape(1, d), g_pre.reshape(1, d))


def _block_resid_body(h_ref, m_ref, f_ref, gm_ref, gf_ref, hout_ref):
    h = h_ref[...] + _rms(m_ref[...].astype(F32), gm_ref[...])
    hout_ref[...] = h + _rms(f_ref[...].astype(F32), gf_ref[...])


def block_resid(h, m, f, g_m, g_f):
    t, d = h.shape
    return _row_call(_block_resid_body, 3, 2, [F32], t, d, "block_resid")(
        h, m, f, g_m.reshape(1, d), g_f.reshape(1, d))[0]


def _mm_bias_body(x_ref, w_ref, b_ref, o_ref):
    acc = jnp.dot(x_ref[...], w_ref[...].astype(BF16), preferred_element_type=F32)
    o_ref[...] = (acc + b_ref[...]).astype(o_ref.dtype)


def _mm_body(x_ref, w_ref, o_ref):
    o_ref[...] = jnp.dot(x_ref[...], w_ref[...].astype(BF16),
                         preferred_element_type=F32).astype(o_ref.dtype)


def _layer_spec(rows, bn, layer):
    return pl.BlockSpec((None, rows, bn), lambda i, j: (layer, 0, j))


def matmul(x, w, b=None, *, layer, bm, bn, out_dtype=BF16, name="matmul"):
    m, k = x.shape
    n = w.shape[2]
    in_specs = [pl.BlockSpec((bm, k), lambda i, j: (i, 0)), _layer_spec(k, bn, layer)]
    args = [x, w]
    body = _mm_body
    if b is not None:
        in_specs.append(_layer_spec(1, bn, layer))
        args.append(b.reshape(b.shape[0], 1, n))
        body = _mm_bias_body
    return pl.pallas_call(
        body,
        grid=(m // bm, n // bn),
        in_specs=in_specs,
        out_specs=pl.BlockSpec((bm, bn), lambda i, j: (i, j)),
        out_shape=jax.ShapeDtypeStruct((m, n), out_dtype),
        compiler_params=_params(2),
        name=name,
    )(*args)


def _cast_slabs(w, layer, steps, step_of):
    layers, rows, cols = w.shape
    slab = rows // steps
    assert slab * steps == rows and slab % 16 == 0
    in_spec = pl.BlockSpec((None, None, slab, cols), lambda i, j: (layer, step_of(i, j), 0, 0))
    out_spec = pl.BlockSpec((None, slab, cols), lambda i, j: (step_of(i, j), 0, 0))
    out_shape = jax.ShapeDtypeStruct((steps, slab, cols), BF16)
    return w.reshape(layers, steps, slab, cols), in_spec, out_spec, out_shape


def _mm_cast2_body(x_ref, w_ref, ca_ref, cb_ref, o_ref, ca_out_ref, cb_out_ref):
    _mm_body(x_ref, w_ref, o_ref)
    ca_out_ref[...] = ca_ref[...].astype(ca_out_ref.dtype)
    cb_out_ref[...] = cb_ref[...].astype(cb_out_ref.dtype)


def matmul_cast2(x, w, cast_a, cast_b, *, layer, bm, bn, name):
    m, k = x.shape
    n = w.shape[2]
    steps_j = n // bn
    steps = (m // bm) * steps_j
    step_of = lambda i, j: i * steps_j + j
    ca, ca_in, ca_out, ca_shape = _cast_slabs(cast_a, 0, steps, step_of)
    cb, cb_in, cb_out, cb_shape = _cast_slabs(cast_b, 0, steps, step_of)
    out, ca16, cb16 = pl.pallas_call(
        _mm_cast2_body,
        grid=(m // bm, steps_j),
        in_specs=[pl.BlockSpec((bm, k), lambda i, j: (i, 0)), _layer_spec(k, bn, layer),
                  ca_in, cb_in],
        out_specs=[pl.BlockSpec((bm, bn), lambda i, j: (i, j)), ca_out, cb_out],
        out_shape=[jax.ShapeDtypeStruct((m, n), BF16), ca_shape, cb_shape],
        compiler_params=_params(2),
        name=name,
    )(x, w, ca, cb)
    return out, ca16.reshape((1,) + cast_a.shape[1:]), cb16.reshape((1,) + cast_b.shape[1:])


def _glu_body(x_ref, w1_ref, b1_ref, w2_ref, b2_ref, o_ref):
    x = x_ref[...]
    a = jnp.dot(x, w1_ref[...].astype(BF16), preferred_element_type=F32) + b1_ref[...]
    b = jnp.dot(x, w2_ref[...].astype(BF16), preferred_element_type=F32) + b2_ref[...]
    o_ref[...] = (a * jax.nn.sigmoid(b)).astype(o_ref.dtype)


def glu_matmul(x, w1, b1, w2, b2, *, layer, bm, bn):
    m, k = x.shape
    n = w1.shape[2]
    w_spec = _layer_spec(k, bn, layer)
    b_spec = _layer_spec(1, bn, layer)
    return pl.pallas_call(
        _glu_body,
        grid=(m // bm, n // bn),
        in_specs=[pl.BlockSpec((bm, k), lambda i, j: (i, 0)),
                  w_spec, b_spec, w_spec, b_spec],
        out_specs=pl.BlockSpec((bm, bn), lambda i, j: (i, j)),
        out_shape=jax.ShapeDtypeStruct((m, n), BF16),
        compiler_params=_params(2),
        name="glu_matmul",
    )(x, w1, b1.reshape(b1.shape[0], 1, n), w2, b2.reshape(b2.shape[0], 1, n))


def _swiglu_up_body(x_ref, wg_ref, wu_ref, wd_ref, o_ref, wd_out_ref):
    x = x_ref[...]
    a = jnp.dot(x, wg_ref[...].astype(BF16), preferred_element_type=F32)
    b = jnp.dot(x, wu_ref[...].astype(BF16), preferred_element_type=F32)
    o_ref[...] = (jax.nn.silu(a) * b).astype(o_ref.dtype)
    wd_out_ref[...] = wd_ref[...].astype(wd_out_ref.dtype)


def swiglu_up(x, wg, wu, wd, *, layer, bm, bn):
    m, k = x.shape
    n = wg.shape[2]
    steps_j = n // bn
    wd4, wd_in, wd_out, wd_shape = _cast_slabs(
        wd, layer, (m // bm) * steps_j, lambda i, j: i * steps_j + j)
    w_spec = _layer_spec(k, bn, layer)
    x_spec = pl.BlockSpec((bm, k), lambda i, j: (i, 0), pipeline_mode=pl.Buffered(1))
    act, wd_bf16 = pl.pallas_call(
        _swiglu_up_body,
        grid=(m // bm, steps_j),
        in_specs=[x_spec, w_spec, w_spec, wd_in],
        out_specs=[pl.BlockSpec((bm, bn), lambda i, j: (i, j)), wd_out],
        out_shape=[jax.ShapeDtypeStruct((m, n), BF16), wd_shape],
        compiler_params=_params(2),
        name="swiglu_up",
    )(x, wg, wu, wd4)
    return act, wd_bf16.reshape((1,) + wd.shape[1:])


def _s5_tables(lam_re, lam_im, log_step, b_re, b_im, c_re, c_im, lc):
    g, p = lam_re.shape
    ch = b_re.shape[-1]
    assert ch == S5_GROUP and 2 * p == LANES
    ng = GROUPS_PER_TILE
    nt = g // ng
    hi = lax.Precision.HIGHEST
    lr, li = lam_re.astype(F32), lam_im.astype(F32)
    step = jnp.exp(log_step.astype(F32))[:, None]
    zr, zi = lr * step, li * step
    taus = jnp.arange(lc + 1, dtype=F32)[:, None, None]
    mag = jnp.exp(zr[None] * taus)
    pr = mag * jnp.cos(zi[None] * taus)
    pi = mag * jnp.sin(zi[None] * taus)
    nr, ni = pr[1] - 1.0, pi[1]
    den = lr * lr + li * li
    qr = (nr * lr + ni * li) / den
    qi = (ni * lr - nr * li) / den
    b_r = jnp.swapaxes(b_re.astype(F32), 1, 2)
    b_i = jnp.swapaxes(b_im.astype(F32), 1, 2)
    br = qr[:, None, :] * b_r - qi[:, None, :] * b_i
    bi = qr[:, None, :] * b_i + qi[:, None, :] * b_r
    cr, ci = c_re.astype(F32), c_im.astype(F32)
    w_re = pr[:lc, :, None, :] * br[None] - pi[:lc, :, None, :] * bi[None]
    w_im = pr[:lc, :, None, :] * bi[None] + pi[:lc, :, None, :] * br[None]
    k = (jnp.einsum('gop,tgcp->tgoc', cr, w_re, precision=hi)
         - jnp.einsum('gop,tgcp->tgoc', ci, w_im, precision=hi))
    eye = jnp.eye(ng, dtype=F32)
    kb = jnp.einsum('tjaoc,ab->jtacbo', k.reshape(lc, nt, ng, ch, ch), eye).reshape(
        nt, lc, LANES, LANES)

    zero = jnp.zeros_like(pr)
    even = (jnp.arange(g) % 2 == 0)[None, :, None]
    halves = lambda a: jnp.concatenate(
        [jnp.where(even, a, zero), jnp.where(even, zero, a)], axis=-1)
    per_tile = lambda a: jnp.swapaxes(a.reshape(lc + 1, nt, ng, a.shape[-1]), 0, 1)
    grp = lambda a: a.reshape((nt, ng) + a.shape[1:])
    cat = lambda *xs: jnp.concatenate(xs, axis=-1)
    return (kb.astype(BF16), per_tile(halves(pr)), per_tile(halves(pi)),
            grp(cat(br, br, bi, bi)), grp(cat(cr, cr, ci, ci)),
            pr[lc].reshape(nt, 1, ng * p), pi[lc].reshape(nt, 1, ng * p))


def _s5_build(kb_ref, prh_ref, pih_ref, bd_ref, cd_ref, m_ref, ws_ref, wyt_ref, lc, nk):
    zero = jnp.zeros((LANES, LANES), BF16)
    for s in range(lc):
        for t in range(lc):
            m_ref[s * LANES:(s + 1) * LANES, t * LANES:(t + 1) * LANES] = (
                kb_ref[0, t - s] if t >= s else zero)
    ws_ref[...] = jnp.zeros_like(ws_ref)
    wyt_ref[...] = jnp.zeros_like(wyt_ref)
    for a in range(GROUPS_PER_TILE):
        bd, cd = bd_ref[0, a], cd_ref[0, a]
        b_r, b_i = bd[:, :LANES], bd[:, LANES:]
        c_r, c_i = cd[:, :LANES], cd[:, LANES:]
        re_tile = slice((a // 2) * LANES, (a // 2 + 1) * LANES)
        im_tile = slice((nk + a // 2) * LANES, (nk + a // 2 + 1) * LANES)
        for s in range(lc):
            rows = slice(s * LANES + a * S5_GROUP, s * LANES + (a + 1) * S5_GROUP)
            p_r, p_i = prh_ref[0, lc - 1 - s, a:a + 1, :], pih_ref[0, lc - 1 - s, a:a + 1, :]
            ws_ref[rows, re_tile] = (p_r * b_r - p_i * b_i).astype(BF16)
            ws_ref[rows, im_tile] = (p_r * b_i + p_i * b_r).astype(BF16)
            p_r, p_i = prh_ref[0, s + 1, a:a + 1, :], pih_ref[0, s + 1, a:a + 1, :]
            wyt_ref[rows, re_tile] = (p_r * c_r - p_i * c_i).astype(BF16)
            wyt_ref[rows, im_tile] = (-(p_r * c_i + p_i * c_r)).astype(BF16)


def _s5_body(x_ref, kb_ref, prh_ref, pih_ref, bd_ref, cd_ref, are_ref, aim_ref, d_ref, o_ref,
             m_ref, ws_ref, wyt_ref, lhs_ref, yi_ref, s_ref, xp_ref, stage_ref, carry_ref,
             *, lc, nb, nc, ns):
    nk = ns // LANES
    fold = 8 // nb
    npk = nk // fold

    @pl.when(pl.program_id(1) == 0)
    def _():
        _s5_build(kb_ref, prh_ref, pih_ref, bd_ref, cd_ref, m_ref, ws_ref, wyt_ref, lc, nk)
        carry_ref[...] = jnp.zeros_like(carry_ref)

    tw = 2 * LANES
    for b in range(nb):
        rows_b = slice(b * nc, (b + 1) * nc)
        for s in range(lc):
            piece = x_ref[b, pl.ds(s, nc, stride=lc), :]
            lhs_ref[rows_b, s * LANES:(s + 1) * LANES] = piece.astype(BF16)
        lhs = lhs_ref[rows_b, :]
        s_b = jnp.dot(lhs, ws_ref[...], preferred_element_type=F32)
        for part in range(2):
            for k in range(nk):
                at = pl.ds((k // npk) * nb + b, nc, stride=8)
                s_ref[part * npk + k % npk, at, :] = (
                    s_b[:, (part * nk + k) * LANES:(part * nk + k + 1) * LANES])
        for t0 in range(0, lc * LANES, tw):
            yi_ref[rows_b, t0:t0 + tw] = jnp.dot(
                lhs[:, :t0 + tw], m_ref[:t0 + tw, t0:t0 + tw], preferred_element_type=F32)

    def stacked(ref, p):
        tiles = [ref[0, :, (h * npk + p) * LANES:(h * npk + p + 1) * LANES] for h in range(fold)]
        return jnp.concatenate([jnp.broadcast_to(t, (nb, LANES)) for t in tiles], axis=0)

    ar = [stacked(are_ref, p) for p in range(npk)]
    ai = [stacked(aim_ref, p) for p in range(npk)]

    def step(c, carry):
        r = pl.ds(pl.multiple_of(c * 8, 8), 8)
        new = []
        for p in range(npk):
            xr, xi = carry[2 * p], carry[2 * p + 1]
            xp_ref[p, r, :] = xr
            xp_ref[npk + p, r, :] = xi
            new.append(ar[p] * xr - ai[p] * xi + s_ref[p, r, :])
            new.append(ar[p] * xi + ai[p] * xr + s_ref[npk + p, r, :])
        return tuple(new)

    carry = lax.fori_loop(0, nc, step, tuple(carry_ref[k] for k in range(2 * npk)))
    for k in range(2 * npk):
        carry_ref[k] = carry[k]

    for b in range(nb):
        xp = jnp.concatenate(
            [xp_ref[part * npk + k % npk, pl.ds((k // npk) * nb + b, nc, stride=8), :]
             for part in range(2) for k in range(nk)], axis=-1)
        y = yi_ref[b * nc:(b + 1) * nc, :] + lax.dot_general(
            xp.astype(BF16), wyt_ref[...], (((1,), (1,)), ((), ())), preferred_element_type=F32)
        for t in range(lc):
            stage_ref[b, pl.ds(t, nc, stride=lc), :] = y[:, t * LANES:(t + 1) * LANES]
        yy = stage_ref[b] + d_ref[0] * x_ref[b]
        o_ref[b] = jax.nn.gelu(yy).astype(o_ref.dtype)


def s5_scan(hn, tables, d_skip, *, tc, lc=S5_CHUNK):
    kb, prh, pih, bd, cd, a_re, a_im = tables
    ng = GROUPS_PER_TILE
    nb, l, d = hn.shape
    nt = d // LANES
    nc = tc // lc
    ns = a_re.shape[-1]
    rows = nb * nc
    body = functools.partial(_s5_body, lc=lc, nb=nb, nc=nc, ns=ns)
    tile4 = lambda j, i: (j, 0, 0, 0)
    tile3 = lambda j, i: (j, 0, 0)
    assert 8 % nb == 0 and (ns // LANES) % (8 // nb) == 0
    packs = 2 * (ns // LANES) * nb // 8
    return pl.pallas_call(
        body,
        grid=(nt, l // tc),
        in_specs=[pl.BlockSpec((nb, tc, LANES), lambda j, i: (0, i, j)),
                  pl.BlockSpec((1, lc, LANES, LANES), tile4),
                  pl.BlockSpec((1, lc + 1, ng, LANES), tile4),
                  pl.BlockSpec((1, lc + 1, ng, LANES), tile4),
                  pl.BlockSpec((1, ng, S5_GROUP, 2 * LANES), tile4),
                  pl.BlockSpec((1, ng, S5_GROUP, 2 * LANES), tile4),
                  pl.BlockSpec((1, 1, ns), tile3),
                  pl.BlockSpec((1, 1, ns), tile3),
                  pl.BlockSpec((1, 1, LANES), lambda j, i: (0, 0, j))],
        out_specs=pl.BlockSpec((nb, tc, LANES), lambda j, i: (0, i, j)),
        out_shape=jax.ShapeDtypeStruct((nb, l, d), BF16),
        scratch_shapes=[pltpu.VMEM((lc * LANES, lc * LANES), BF16),
                        pltpu.VMEM((lc * LANES, 2 * ns), BF16),
                        pltpu.VMEM((lc * LANES, 2 * ns), BF16),
                        pltpu.VMEM((rows, lc * LANES), BF16),
                        pltpu.VMEM((rows, lc * LANES), F32),
                        pltpu.VMEM((packs, nc * 8, LANES), F32),
                        pltpu.VMEM((packs, nc * 8, LANES), F32),
                        pltpu.VMEM((nb, tc, LANES), F32),
                        pltpu.VMEM((packs, 8, LANES), F32)],
        compiler_params=_params(2),
        name="s5_scan",
    )(hn, kb, prh, pih, bd, cd, a_re, a_im, d_skip.reshape(1, 1, d).astype(F32))


def _rope(x, cos, sin_signed, first_half):
    w = x.shape[1]
    reps = w // LANES
    cosw = jnp.tile(cos, (1, reps))
    sinw = jnp.tile(sin_signed, (1, reps))
    sel = jnp.tile(first_half, (1, reps))
    half = HEAD_DIM // 2
    swapped = jnp.where(sel, pltpu.roll(x, w - half, 1), pltpu.roll(x, half, 1))
    return x * cosw + swapped * sinw


def _attn_body(sink_ref, q_ref, kc_ref, kp_ref, vc_ref, vp_ref, o_ref, *, q_per_kv):
    n = pl.program_id(1)
    blk = ATTN_BLOCK
    q = q_ref[...]
    k = jnp.concatenate([kp_ref[...], kc_ref[...]], axis=0).astype(F32)
    v = jnp.concatenate([vp_ref[...], vc_ref[...]], axis=0).astype(F32)

    qi = lax.broadcasted_iota(jnp.int32, (blk, 2 * blk), 0)
    si = lax.broadcasted_iota(jnp.int32, (blk, 2 * blk), 1)
    diff = blk + qi - si
    mask = (diff >= 0) & (diff < blk) & ((si >= blk) | (n > 0))
    bias = jnp.where(mask, 0.0, -jnp.inf).astype(F32)

    low = lax.broadcasted_iota(jnp.int32, (2 * blk, LANES), 1) < HEAD_DIM

    def split(tile, head_in_low):
        moved = pltpu.roll(tile, HEAD_DIM, 1)
        in_lo, in_hi = (tile, moved) if head_in_low else (moved, tile)
        return (jnp.where(low, in_lo, 0.0).astype(BF16),
                jnp.where(low, 0.0, in_hi).astype(BF16))

    for h in range(N_KV_HEADS):
        tile = slice((h // 2) * LANES, (h // 2 + 1) * LANES)
        k_lo, k_hi = split(k[:, tile], h % 2 == 0)
        v_lo, v_hi = split(v[:, tile], h % 2 == 0)
        for i in range(q_per_kv // 2):
            head = h * q_per_kv + 2 * i
            qt = q[:, (head // 2) * LANES:(head // 2 + 1) * LANES]
            acc = None
            for kk, vv, a in ((k_lo, v_lo, head), (k_hi, v_hi, head + 1)):
                s = lax.dot_general(qt, kk, (((1,), (1,)), ((), ())),
                                    preferred_element_type=F32) + bias
                sink = sink_ref[a] * LOG2E
                m = jnp.maximum(jnp.max(s, axis=-1, keepdims=True), sink)
                e = jnp.exp2(s - m)
                denom = jnp.sum(e, axis=-1, keepdims=True) + jnp.exp2(sink - m)
                o = jnp.dot(e.astype(BF16), vv, preferred_element_type=F32) * (1.0 / denom)
                acc = o if acc is None else acc + o
            o_ref[:, (head // 2) * LANES:(head // 2 + 1) * LANES] = acc.astype(o_ref.dtype)


def _qkv_rope_body(x_ref, w_ref, b_ref, pos_ref, invf_ref, o_ref, cos_ref, sin_ref,
                   *, q_cols, rope_cols):
    j = pl.program_id(1)
    lane = lax.broadcasted_iota(jnp.int32, cos_ref.shape, 1)
    first_half = (lane % HEAD_DIM) < (HEAD_DIM // 2)

    @pl.when(j == 0)
    def _():
        ang = pos_ref[...].astype(F32) * invf_ref[...]
        sin = jnp.sin(ang)
        cos_ref[...] = jnp.cos(ang)
        sin_ref[...] = jnp.where(first_half, -sin, sin)

    w = w_ref[...].astype(BF16)
    bn = o_ref.shape[1]
    col = j * bn + lax.broadcasted_iota(jnp.int32, (1, bn), 1)
    scale = jnp.where(col < q_cols, HEAD_DIM ** -0.5 * LOG2E, 1.0)
    rotated = col < rope_cols
    chunk = x_ref.shape[0] // ROPE_ROW_CHUNKS
    lane_c = lax.broadcasted_iota(jnp.int32, (chunk, LANES), 1)
    first_half_c = (lane_c % HEAD_DIM) < (HEAD_DIM // 2)
    for r in range(ROPE_ROW_CHUNKS):
        rows = slice(r * chunk, (r + 1) * chunk)
        acc = jnp.dot(x_ref[rows, :], w, preferred_element_type=F32) + b_ref[...]
        rot = _rope(acc, cos_ref[rows, :], sin_ref[rows, :], first_half_c) * scale
        o_ref[rows, :] = jnp.where(rotated, rot, acc).astype(o_ref.dtype)


def qkv_rope_proj(x, w, b, positions, *, layer, bm, bn):
    m, k = x.shape
    n = w.shape[2]
    kvw = N_KV_HEADS * HEAD_DIM
    qw = n - 2 * kvw
    half = HEAD_DIM // 2
    inv_freq = jnp.power(ROPE_THETA, -jnp.arange(half, dtype=F32) / half)
    invf = jnp.tile(inv_freq, LANES // half).reshape(1, LANES)
    pos = positions.reshape(m, 1).astype(jnp.int32)
    body = functools.partial(_qkv_rope_body, q_cols=qw, rope_cols=qw + kvw)
    return pl.pallas_call(
        body,
        grid=(m // bm, n // bn),
        in_specs=[pl.BlockSpec((bm, k), lambda i, j: (i, 0)),
                  _layer_spec(k, bn, layer),
                  _layer_spec(1, bn, layer),
                  pl.BlockSpec((bm, 1), lambda i, j: (i, 0)),
                  pl.BlockSpec((1, LANES), lambda i, j: (0, 0))],
        out_specs=pl.BlockSpec((bm, bn), lambda i, j: (i, j)),
        out_shape=jax.ShapeDtypeStruct((m, n), BF16),
        scratch_shapes=[pltpu.VMEM((bm, LANES), F32), pltpu.VMEM((bm, LANES), F32)],
        compiler_params=_params(2),
        name="qkv_rope_proj",
    )(x, w, b.reshape(b.shape[0], 1, n), pos, invf)


def swa_attention(qkv, sinks, *, batch, seq):
    t, width = qkv.shape
    kvw = N_KV_HEADS * HEAD_DIM
    qw = width - 2 * kvw
    n_q_heads = qw // HEAD_DIM
    q_per_kv = n_q_heads // N_KV_HEADS
    assert q_per_kv % 2 == 0 and 2 * HEAD_DIM == LANES
    blk = ATTN_BLOCK
    nblk = seq // blk
    cur = lambda b, n: (b * nblk + n, 0)
    kcol = qw // kvw
    body = functools.partial(_attn_body, q_per_kv=q_per_kv)
    return pl.pallas_call(
        body,
        grid=(batch, nblk),
        in_specs=[pl.BlockSpec(memory_space=pltpu.SMEM),
                  pl.BlockSpec((blk, qw), cur),
                  pl.BlockSpec((blk, kvw), lambda b, n: (b * nblk + n, kcol)),
                  pl.BlockSpec((blk, kvw), lambda b, n: (b * nblk + jnp.maximum(n - 1, 0), kcol)),
                  pl.BlockSpec((blk, kvw), lambda b, n: (b * nblk + n, kcol + 1)),
                  pl.BlockSpec((blk, kvw), lambda b, n: (b * nblk + jnp.maximum(n - 1, 0), kcol + 1))],
        out_specs=pl.BlockSpec((blk, qw), cur),
        out_shape=jax.ShapeDtypeStruct((t, qw), BF16),
        compiler_params=_params(2),
        name="swa_attention",
    )(sinks.astype(F32), qkv, qkv, qkv, qkv, qkv)


def _ffn(hn, w_gate, w_up, w_down, layer, casts=None):
    act, w_down_bf16 = swiglu_up(hn, w_gate, w_up, w_down, layer=layer, bm=2048, bn=256)
    if casts is None:
        return matmul(act, w_down_bf16, layer=0, bm=512, bn=512, name="ffn_down")
    return matmul_cast2(act, w_down_bf16, *casts, layer=0, bm=512, bn=512, name="ffn_down")


def kernel(x, positions, norm_pre_mix, norm_post_mix, norm_pre_ffn, norm_post_ffn, s5_lam_re, s5_lam_im, s5_log_step, s5_b_re, s5_b_im, s5_c_re, s5_c_im, s5_d, s5_w_out1, s5_b_out1, s5_w_out2, s5_b_out2, attn_w_qkv, attn_b_qkv, attn_w_o, attn_b_o, attn_sinks, ffn_w_gate, ffn_w_up, ffn_w_down):
    bsz, seq, d = x.shape
    t = bsz * seq
    h = x.reshape(t, d)

    hn = rmsnorm(h, norm_pre_mix[0], F32)
    tables = _s5_tables(s5_lam_re[0], s5_lam_im[0], s5_log_step[0], s5_b_re[0], s5_b_im[0],
                        s5_c_re[0], s5_c_im[0], S5_CHUNK)
    g = s5_scan(hn.reshape(bsz, seq, d), tables, s5_d[0], tc=2048).reshape(t, d)
    m = glu_matmul(g, s5_w_out1, s5_b_out1, s5_w_out2, s5_b_out2, layer=0, bm=1024, bn=512)
    hn = mix_norm(h, m, norm_post_mix[0], norm_pre_ffn[0])
    f, w_qkv, w_o = _ffn(hn, ffn_w_gate, ffn_w_up, ffn_w_down, 0, casts=(attn_w_qkv, attn_w_o))
    h, hn = block_resid_norm(h, m, f, norm_post_mix[0], norm_post_ffn[0], norm_pre_mix[1])

    qkv = qkv_rope_proj(hn, w_qkv, attn_b_qkv, positions, layer=0, bm=1024, bn=1024)
    att = swa_attention(qkv, attn_sinks[0], batch=bsz, seq=seq)
    m = matmul(att, w_o, attn_b_o, layer=0, bm=1024, bn=1024, name="o_proj")
    hn = mix_norm(h, m, norm_post_mix[1], norm_pre_ffn[1])
    f = _ffn(hn, ffn_w_gate, ffn_w_up, ffn_w_down, 1)
    h = block_resid(h, m, f, norm_post_mix[1], norm_post_ffn[1])
    return h.reshape(bsz, seq, d)
```

```python
import functools
import math

import jax
import jax.numpy as jnp
from jax import lax
from jax.experimental import pallas as pl
from jax.experimental.pallas import tpu as pltpu

F32 = jnp.float32
BF16 = jnp.bfloat16

EPS = 1e-6
LANES = 128
S5_GROUP = 16
GROUPS_PER_TILE = LANES // S5_GROUP
S5_CHUNK = 8
HEAD_DIM = 64
N_KV_HEADS = 8
ATTN_BLOCK = 128
ROPE_THETA = 10000.0
ROPE_ROW_CHUNKS = 4
LOG2E = math.log2(math.e)
VMEM_LIMIT = 60 * 1024 * 1024


def _params(n_axes, vmem=VMEM_LIMIT):
    return pltpu.CompilerParams(
        dimension_semantics=("arbitrary",) * n_axes, vmem_limit_bytes=vmem)


def _rms(x, g):
    r = lax.rsqrt(jnp.mean(x * x, axis=-1, keepdims=True) + EPS)
    return x * r * g


def _rmsnorm_body(x_ref, g_ref, o_ref):
    o_ref[...] = _rms(x_ref[...], g_ref[...]).astype(o_ref.dtype)


def _row_call(body, n_rows_in, n_vec_in, out_dtypes, t, d, name):
    streams = n_rows_in + len(out_dtypes)
    rows = 512 if streams <= 3 else 256
    row_spec = pl.BlockSpec((rows, d), lambda i: (i, 0))
    vec_spec = pl.BlockSpec((1, d), lambda i: (0, 0))
    return pl.pallas_call(
        body,
        grid=(t // rows,),
        in_specs=[row_spec] * n_rows_in + [vec_spec] * n_vec_in,
        out_specs=[row_spec] * len(out_dtypes),
        out_shape=[jax.ShapeDtypeStruct((t, d), dt) for dt in out_dtypes],
        compiler_params=_params(1),
        name=name,
    )


def rmsnorm(x, g, out_dtype):
    t, d = x.shape
    return _row_call(_rmsnorm_body, 1, 1, [out_dtype], t, d, "rmsnorm")(x, g.reshape(1, d))[0]


def _mix_norm_body(h_ref, m_ref, gm_ref, gpre_ref, hn_ref):
    h = h_ref[...] + _rms(m_ref[...].astype(F32), gm_ref[...])
    hn_ref[...] = _rms(h, gpre_ref[...]).astype(hn_ref.dtype)


def mix_norm(h, m, g_m, g_pre):
    t, d = h.shape
    return _row_call(_mix_norm_body, 2, 2, [BF16], t, d, "mix_norm")(
        h, m, g_m.reshape(1, d), g_pre.reshape(1, d))[0]


def _block_resid_norm_body(h_ref, m_ref, f_ref, gm_ref, gf_ref, gpre_ref, hout_ref, hn_ref):
    h = h_ref[...] + _rms(m_ref[...].astype(F32), gm_ref[...])
    h = h + _rms(f_ref[...].astype(F32), gf_ref[...])
    hout_ref[...] = h
    hn_ref[...] = _rms(h, gpre_ref[...]).astype(hn_ref.dtype)


def block_resid_norm(h, m, f, g_m, g_f, g_pre):
    t, d = h.shape
    return _row_call(_block_resid_norm_body, 3, 3, [F32, BF16], t, d, "block_resid_norm")(
        h, m, f, g_m.reshape(1, d), g_f.reshape(1, d), g_pre.reshape(1, d))


def _block_resid_body(h_ref, m_ref, f_ref, gm_ref, gf_ref, hout_ref):
    h = h_ref[...] + _rms(m_ref[...].astype(F32), gm_ref[...])
    hout_ref[...] = h + _rms(f_ref[...].astype(F32), gf_ref[...])


def block_resid(h, m, f, g_m, g_f):
    t, d = h.shape
    return _row_call(_block_resid_body, 3, 2, [F32], t, d, "block_resid")(
        h, m, f, g_m.reshape(1, d), g_f.reshape(1, d))[0]


def _mm_bias_body(x_ref, w_ref, b_ref, o_ref):
    acc = jnp.dot(x_ref[...], w_ref[...].astype(BF16), preferred_element_type=F32)
    o_ref[...] = (acc + b_ref[...]).astype(o_ref.dtype)


def _mm_body(x_ref, w_ref, o_ref):
    o_ref[...] = jnp.dot(x_ref[...], w_ref[...].astype(BF16),
                         preferred_element_type=F32).astype(o_ref.dtype)


def _layer_spec(rows, bn, layer):
    return pl.BlockSpec((None, rows, bn), lambda i, j: (layer, 0, j))


def matmul(x, w, b=None, *, layer, bm, bn, out_dtype=BF16, name="matmul"):
    m, k = x.shape
    n = w.shape[2]
    in_specs = [pl.BlockSpec((bm, k), lambda i, j: (i, 0)), _layer_spec(k, bn, layer)]
    args = [x, w]
    body = _mm_body
    if b is not None:
        in_specs.append(_layer_spec(1, bn, layer))
        args.append(b.reshape(b.shape[0], 1, n))
        body = _mm_bias_body
    return pl.pallas_call(
        body,
        grid=(m // bm, n // bn),
        in_specs=in_specs,
        out_specs=pl.BlockSpec((bm, bn), lambda i, j: (i, j)),
        out_shape=jax.ShapeDtypeStruct((m, n), out_dtype),
        compiler_params=_params(2),
        name=name,
    )(*args)


def _cast_slabs(w, layer, steps, step_of):
    layers, rows, cols = w.shape
    slab = rows // steps
    assert slab * steps == rows and slab % 16 == 0
    in_spec = pl.BlockSpec((None, None, slab, cols), lambda i, j: (layer, step_of(i, j), 0, 0))
    out_spec = pl.BlockSpec((None, slab, cols), lambda i, j: (step_of(i, j), 0, 0))
    out_shape = jax.ShapeDtypeStruct((steps, slab, cols), BF16)
    return w.reshape(layers, steps, slab, cols), in_spec, out_spec, out_shape


def _mm_cast2_body(x_ref, w_ref, ca_ref, cb_ref, o_ref, ca_out_ref, cb_out_ref):
    _mm_body(x_ref, w_ref, o_ref)
    ca_out_ref[...] = ca_ref[...].astype(ca_out_ref.dtype)
    cb_out_ref[...] = cb_ref[...].astype(cb_out_ref.dtype)


def matmul_cast2(x, w, cast_a, cast_b, *, layer, bm, bn, name):
    m, k = x.shape
    n = w.shape[2]
    steps_j = n // bn
    steps = (m // bm) * steps_j
    step_of = lambda i, j: i * steps_j + j
    ca, ca_in, ca_out, ca_shape = _cast_slabs(cast_a, 0, steps, step_of)
    cb, cb_in, cb_out, cb_shape = _cast_slabs(cast_b, 0, steps, step_of)
    out, ca16, cb16 = pl.pallas_call(
        _mm_cast2_body,
        grid=(m // bm, steps_j),
        in_specs=[pl.BlockSpec((bm, k), lambda i, j: (i, 0)), _layer_spec(k, bn, layer),
                  ca_in, cb_in],
        out_specs=[pl.BlockSpec((bm, bn), lambda i, j: (i, j)), ca_out, cb_out],
        out_shape=[jax.ShapeDtypeStruct((m, n), BF16), ca_shape, cb_shape],
        compiler_params=_params(2),
        name=name,
    )(x, w, ca, cb)
    return out, ca16.reshape((1,) + cast_a.shape[1:]), cb16.reshape((1,) + cast_b.shape[1:])


def _glu_body(x_ref, w1_ref, b1_ref, w2_ref, b2_ref, o_ref):
    x = x_ref[...]
    a = jnp.dot(x, w1_ref[...].astype(BF16), preferred_element_type=F32) + b1_ref[...]
    b = jnp.dot(x, w2_ref[...].astype(BF16), preferred_element_type=F32) + b2_ref[...]
    o_ref[...] = (a * jax.nn.sigmoid(b)).astype(o_ref.dtype)


def glu_matmul(x, w1, b1, w2, b2, *, layer, bm, bn):
    m, k = x.shape
    n = w1.shape[2]
    w_spec = _layer_spec(k, bn, layer)
    b_spec = _layer_spec(1, bn, layer)
    return pl.pallas_call(
        _glu_body,
        grid=(m // bm, n // bn),
        in_specs=[pl.BlockSpec((bm, k), lambda i, j: (i, 0)),
                  w_spec, b_spec, w_spec, b_spec],
        out_specs=pl.BlockSpec((bm, bn), lambda i, j: (i, j)),
        out_shape=jax.ShapeDtypeStruct((m, n), BF16),
        compiler_params=_params(2),
        name="glu_matmul",
    )(x, w1, b1.reshape(b1.shape[0], 1, n), w2, b2.reshape(b2.shape[0], 1, n))


def _swiglu_up_body(x_ref, wg_ref, wu_ref, wd_ref, o_ref, wd_out_ref):
    x = x_ref[...]
    a = jnp.dot(x, wg_ref[...].astype(BF16), preferred_element_type=F32)
    b = jnp.dot(x, wu_ref[...].astype(BF16), preferred_element_type=F32)
    o_ref[...] = (jax.nn.silu(a) * b).astype(o_ref.dtype)
    wd_out_ref[...] = wd_ref[...].astype(wd_out_ref.dtype)


def swiglu_up(x, wg, wu, wd, *, layer, bm, bn):
    m, k = x.shape
    n = wg.shape[2]
    steps_j = n // bn
    wd4, wd_in, wd_out, wd_shape = _cast_slabs(
        wd, layer, (m // bm) * steps_j, lambda i, j: i * steps_j + j)
    w_spec = _layer_spec(k, bn, layer)
    x_spec = pl.BlockSpec((bm, k), lambda i, j: (i, 0), pipeline_mode=pl.Buffered(1))
    act, wd_bf16 = pl.pallas_call(
        _swiglu_up_body,
        grid=(m // bm, steps_j),
        in_specs=[x_spec, w_spec, w_spec, wd_in],
        out_specs=[pl.BlockSpec((bm, bn), lambda i, j: (i, j)), wd_out],
        out_shape=[jax.ShapeDtypeStruct((m, n), BF16), wd_shape],
        compiler_params=_params(2),
        name="swiglu_up",
    )(x, wg, wu, wd4)
    return act, wd_bf16.reshape((1,) + wd.shape[1:])


def _s5_tables(lam_re, lam_im, log_step, b_re, b_im, c_re, c_im, lc):
    g, p = lam_re.shape
    ch = b_re.shape[-1]
    assert ch == S5_GROUP and 2 * p == LANES
    ng = GROUPS_PER_TILE
    nt = g // ng
    hi = lax.Precision.HIGHEST
    lr, li = lam_re.astype(F32), lam_im.astype(F32)
    step = jnp.exp(log_step.astype(F32))[:, None]
    zr, zi = lr * step, li * step
    taus = jnp.arange(lc + 1, dtype=F32)[:, None, None]
    mag = jnp.exp(zr[None] * taus)
    pr = mag * jnp.cos(zi[None] * taus)
    pi = mag * jnp.sin(zi[None] * taus)
    nr, ni = pr[1] - 1.0, pi[1]
    den = lr * lr + li * li
    qr = (nr * lr + ni * li) / den
    qi = (ni * lr - nr * li) / den
    b_r = jnp.swapaxes(b_re.astype(F32), 1, 2)
    b_i = jnp.swapaxes(b_im.astype(F32), 1, 2)
    br = qr[:, None, :] * b_r - qi[:, None, :] * b_i
    bi = qr[:, None, :] * b_i + qi[:, None, :] * b_r
    cr, ci = c_re.astype(F32), c_im.astype(F32)
    w_re = pr[:lc, :, None, :] * br[None] - pi[:lc, :, None, :] * bi[None]
    w_im = pr[:lc, :, None, :] * bi[None] + pi[:lc, :, None, :] * br[None]
    k = (jnp.einsum('gop,tgcp->tgoc', cr, w_re, precision=hi)
         - jnp.einsum('gop,tgcp->tgoc', ci, w_im, precision=hi))
    eye = jnp.eye(ng, dtype=F32)
    kb = jnp.einsum('tjaoc,ab->jtacbo', k.reshape(lc, nt, ng, ch, ch), eye).reshape(
        nt, lc, LANES, LANES)

    zero = jnp.zeros_like(pr)
    even = (jnp.arange(g) % 2 == 0)[None, :, None]
    halves = lambda a: jnp.concatenate(
        [jnp.where(even, a, zero), jnp.where(even, zero, a)], axis=-1)
    per_tile = lambda a: jnp.swapaxes(a.reshape(lc + 1, nt, ng, a.shape[-1]), 0, 1)
    grp = lambda a: a.reshape((nt, ng) + a.shape[1:])
    cat = lambda *xs: jnp.concatenate(xs, axis=-1)
    return (kb.astype(BF16), per_tile(halves(pr)), per_tile(halves(pi)),
            grp(cat(br, br, bi, bi)), grp(cat(cr, cr, ci, ci)),
            pr[lc].reshape(nt, 1, ng * p), pi[lc].reshape(nt, 1, ng * p))


def _s5_build(kb_ref, prh_ref, pih_ref, bd_ref, cd_ref, m_ref, ws_ref, wyt_ref, lc, nk):
    zero = jnp.zeros((LANES, LANES), BF16)
    for s in range(lc):
        for t in range(lc):
            m_ref[s * LANES:(s + 1) * LANES, t * LANES:(t + 1) * LANES] = (
                kb_ref[0, t - s] if t >= s else zero)
    ws_ref[...] = jnp.zeros_like(ws_ref)
    wyt_ref[...] = jnp.zeros_like(wyt_ref)
    for a in range(GROUPS_PER_TILE):
        bd, cd = bd_ref[0, a], cd_ref[0, a]
        b_r, b_i = bd[:, :LANES], bd[:, LANES:]
        c_r, c_i = cd[:, :LANES], cd[:, LANES:]
        re_tile = slice((a // 2) * LANES, (a // 2 + 1) * LANES)
        im_tile = slice((nk + a // 2) * LANES, (nk + a // 2 + 1) * LANES)
        for s in range(lc):
            rows = slice(s * LANES + a * S5_GROUP, s * LANES + (a + 1) * S5_GROUP)
            p_r, p_i = prh_ref[0, lc - 1 - s, a:a + 1, :], pih_ref[0, lc - 1 - s, a:a + 1, :]
            ws_ref[rows, re_tile] = (p_r * b_r - p_i * b_i).astype(BF16)
            ws_ref[rows, im_tile] = (p_r * b_i + p_i * b_r).astype(BF16)
            p_r, p_i = prh_ref[0, s + 1, a:a + 1, :], pih_ref[0, s + 1, a:a + 1, :]
            wyt_ref[rows, re_tile] = (p_r * c_r - p_i * c_i).astype(BF16)
            wyt_ref[rows, im_tile] = (-(p_r * c_i + p_i * c_r)).astype(BF16)


def _s5_body(x_ref, kb_ref, prh_ref, pih_ref, bd_ref, cd_ref, are_ref, aim_ref, d_ref, o_ref,
             m_ref, ws_ref, wyt_ref, lhs_ref, s_ref, xp_ref, stage_ref, carry_ref,
             *, lc, nb, nc, ns):
    nk = ns // LANES

    @pl.when(pl.program_id(1) == 0)
    def _():
        _s5_build(kb_ref, prh_ref, pih_ref, bd_ref, cd_ref, m_ref, ws_ref, wyt_ref, lc, nk)
        carry_ref[...] = jnp.zeros_like(carry_ref)

    for b in range(nb):
        for s in range(lc):
            piece = x_ref[b, pl.ds(s, nc, stride=lc), :]
            lhs_ref[b * nc:(b + 1) * nc, s * LANES:(s + 1) * LANES] = piece.astype(BF16)
    lhs = lhs_ref[...]
    s_all = jnp.dot(lhs, ws_ref[...], preferred_element_type=F32)
    for k in range(2 * nk):
        for b in range(nb):
            s_ref[k, pl.ds(b, nc, stride=nb), :] = (
                s_all[b * nc:(b + 1) * nc, k * LANES:(k + 1) * LANES])

    tile_rows = lambda ref, k: jnp.broadcast_to(ref[0, :, k * LANES:(k + 1) * LANES], (nb, LANES))
    ar = [tile_rows(are_ref, k) for k in range(nk)]
    ai = [tile_rows(aim_ref, k) for k in range(nk)]

    per = 8 // nb

    def step(i, carry):
        r = pl.ds(pl.multiple_of(i * 8, 8), 8)
        s_in = [s_ref[k, r, :] for k in range(2 * nk)]
        before = [[] for _ in range(2 * nk)]
        for u in range(per):
            new = []
            for k in range(nk):
                xr, xi = carry[2 * k], carry[2 * k + 1]
                before[k].append(xr)
                before[nk + k].append(xi)
                rows = slice(u * nb, (u + 1) * nb)
                new.append(ar[k] * xr - ai[k] * xi + s_in[k][rows])
                new.append(ar[k] * xi + ai[k] * xr + s_in[nk + k][rows])
            carry = tuple(new)
        for k in range(2 * nk):
            xp_ref[k, r, :] = jnp.concatenate(before[k], axis=0)
        return carry

    carry = lax.fori_loop(0, nc // per, step,
                          tuple(carry_ref[k, 0:nb, :] for k in range(2 * nk)))
    for k in range(2 * nk):
        carry_ref[k, 0:nb, :] = carry[k]

    xp = jnp.concatenate(
        [jnp.concatenate([xp_ref[k, pl.ds(b, nc, stride=nb), :] for b in range(nb)], axis=0)
         for k in range(2 * nk)], axis=-1)
    tw = 2 * LANES
    y_intra = jnp.concatenate(
        [jnp.dot(lhs[:, :t0 + tw], m_ref[:t0 + tw, t0:t0 + tw], preferred_element_type=F32)
         for t0 in range(0, lc * LANES, tw)], axis=-1)
    y = y_intra + lax.dot_general(xp.astype(BF16), wyt_ref[...], (((1,), (1,)), ((), ())),
                                  preferred_element_type=F32)
    for b in range(nb):
        for t in range(lc):
            stage_ref[b, pl.ds(t, nc, stride=lc), :] = (
                y[b * nc:(b + 1) * nc, t * LANES:(t + 1) * LANES])
    yy = stage_ref[...] + d_ref[...] * x_ref[...]
    o_ref[...] = jax.nn.gelu(yy).astype(o_ref.dtype)


def s5_scan(hn, tables, d_skip, *, tc, lc=S5_CHUNK):
    kb, prh, pih, bd, cd, a_re, a_im = tables
    ng = GROUPS_PER_TILE
    nb, l, d = hn.shape
    nt = d // LANES
    nc = tc // lc
    ns = a_re.shape[-1]
    rows = nb * nc
    body = functools.partial(_s5_body, lc=lc, nb=nb, nc=nc, ns=ns)
    tile4 = lambda j, i: (j, 0, 0, 0)
    tile3 = lambda j, i: (j, 0, 0)
    assert 8 % nb == 0
    return pl.pallas_call(
        body,
        grid=(nt, l // tc),
        in_specs=[pl.BlockSpec((nb, tc, LANES), lambda j, i: (0, i, j)),
                  pl.BlockSpec((1, lc, LANES, LANES), tile4),
                  pl.BlockSpec((1, lc + 1, ng, LANES), tile4),
                  pl.BlockSpec((1, lc + 1, ng, LANES), tile4),
                  pl.BlockSpec((1, ng, S5_GROUP, 2 * LANES), tile4),
                  pl.BlockSpec((1, ng, S5_GROUP, 2 * LANES), tile4),
                  pl.BlockSpec((1, 1, ns), tile3),
                  pl.BlockSpec((1, 1, ns), tile3),
                  pl.BlockSpec((1, 1, LANES), lambda j, i: (0, 0, j))],
        out_specs=pl.BlockSpec((nb, tc, LANES), lambda j, i: (0, i, j)),
        out_shape=jax.ShapeDtypeStruct((nb, l, d), BF16),
        scratch_shapes=[pltpu.VMEM((lc * LANES, lc * LANES), BF16),
                        pltpu.VMEM((lc * LANES, 2 * ns), BF16),
                        pltpu.VMEM((lc * LANES, 2 * ns), BF16),
                        pltpu.VMEM((rows, lc * LANES), BF16),
                        pltpu.VMEM((2 * ns // LANES, rows, LANES), F32),
                        pltpu.VMEM((2 * ns // LANES, rows, LANES), F32),
                        pltpu.VMEM((nb, tc, LANES), F32),
                        pltpu.VMEM((2 * ns // LANES, 8, LANES), F32)],
        compiler_params=_params(2),
        name="s5_scan",
    )(hn, kb, prh, pih, bd, cd, a_re, a_im, d_skip.reshape(1, 1, d).astype(F32))


def _rope(x, cos, sin_signed, first_half):
    w = x.shape[1]
    reps = w // LANES
    cosw = jnp.tile(cos, (1, reps))
    sinw = jnp.tile(sin_signed, (1, reps))
    sel = jnp.tile(first_half, (1, reps))
    half = HEAD_DIM // 2
    swapped = jnp.where(sel, pltpu.roll(x, w - half, 1), pltpu.roll(x, half, 1))
    return x * cosw + swapped * sinw


def _attn_body(sink_ref, q_ref, kc_ref, kp_ref, vc_ref, vp_ref, o_ref, *, q_per_kv):
    n = pl.program_id(1)
    blk = ATTN_BLOCK
    q = q_ref[...]
    k = jnp.concatenate([kp_ref[...], kc_ref[...]], axis=0).astype(F32)
    v = jnp.concatenate([vp_ref[...], vc_ref[...]], axis=0).astype(F32)

    qi = lax.broadcasted_iota(jnp.int32, (blk, 2 * blk), 0)
    si = lax.broadcasted_iota(jnp.int32, (blk, 2 * blk), 1)
    diff = blk + qi - si
    mask = (diff >= 0) & (diff < blk) & ((si >= blk) | (n > 0))
    bias = jnp.where(mask, 0.0, -jnp.inf).astype(F32)

    low = lax.broadcasted_iota(jnp.int32, (2 * blk, LANES), 1) < HEAD_DIM

    def split(tile, head_in_low):
        moved = pltpu.roll(tile, HEAD_DIM, 1)
        in_lo, in_hi = (tile, moved) if head_in_low else (moved, tile)
        return (jnp.where(low, in_lo, 0.0).astype(BF16),
                jnp.where(low, 0.0, in_hi).astype(BF16))

    for h in range(N_KV_HEADS):
        tile = slice((h // 2) * LANES, (h // 2 + 1) * LANES)
        k_lo, k_hi = split(k[:, tile], h % 2 == 0)
        v_lo, v_hi = split(v[:, tile], h % 2 == 0)
        for i in range(q_per_kv // 2):
            head = h * q_per_kv + 2 * i
            qt = q[:, (head // 2) * LANES:(head // 2 + 1) * LANES]
            acc = None
            for kk, vv, a in ((k_lo, v_lo, head), (k_hi, v_hi, head + 1)):
                s = lax.dot_general(qt, kk, (((1,), (1,)), ((), ())),
                                    preferred_element_type=F32) + bias
                sink = sink_ref[a] * LOG2E
                m = jnp.maximum(jnp.max(s, axis=-1, keepdims=True), sink)
                e = jnp.exp2(s - m)
                denom = jnp.sum(e, axis=-1, keepdims=True) + jnp.exp2(sink - m)
                o = jnp.dot(e.astype(BF16), vv, preferred_element_type=F32) * (1.0 / denom)
                acc = o if acc is None else acc + o
            o_ref[:, (head // 2) * LANES:(head // 2 + 1) * LANES] = acc.astype(o_ref.dtype)


def _qkv_rope_body(x_ref, w_ref, b_ref, pos_ref, invf_ref, o_ref, cos_ref, sin_ref,
                   *, q_cols, rope_cols):
    j = pl.program_id(1)
    lane = lax.broadcasted_iota(jnp.int32, cos_ref.shape, 1)
    first_half = (lane % HEAD_DIM) < (HEAD_DIM // 2)

    @pl.when(j == 0)
    def _():
        ang = pos_ref[...].astype(F32) * invf_ref[...]
        sin = jnp.sin(ang)
        cos_ref[...] = jnp.cos(ang)
        sin_ref[...] = jnp.where(first_half, -sin, sin)

    w = w_ref[...].astype(BF16)
    bn = o_ref.shape[1]
    col = j * bn + lax.broadcasted_iota(jnp.int32, (1, bn), 1)
    scale = jnp.where(col < q_cols, HEAD_DIM ** -0.5 * LOG2E, 1.0)
    rotated = col < rope_cols
    chunk = x_ref.shape[0] // ROPE_ROW_CHUNKS
    lane_c = lax.broadcasted_iota(jnp.int32, (chunk, LANES), 1)
    first_half_c = (lane_c % HEAD_DIM) < (HEAD_DIM // 2)
    for r in range(ROPE_ROW_CHUNKS):
        rows = slice(r * chunk, (r + 1) * chunk)
        acc = jnp.dot(x_ref[rows, :], w, preferred_element_type=F32) + b_ref[...]
        rot = _rope(acc, cos_ref[rows, :], sin_ref[rows, :], first_half_c) * scale
        o_ref[rows, :] = jnp.where(rotated, rot, acc).astype(o_ref.dtype)


def qkv_rope_proj(x, w, b, positions, *, layer, bm, bn):
    m, k = x.shape
    n = w.shape[2]
    kvw = N_KV_HEADS * HEAD_DIM
    qw = n - 2 * kvw
    half = HEAD_DIM // 2
    inv_freq = jnp.power(ROPE_THETA, -jnp.arange(half, dtype=F32) / half)
    invf = jnp.tile(inv_freq, LANES // half).reshape(1, LANES)
    pos = positions.reshape(m, 1).astype(jnp.int32)
    body = functools.partial(_qkv_rope_body, q_cols=qw, rope_cols=qw + kvw)
    return pl.pallas_call(
        body,
        grid=(m // bm, n // bn),
        in_specs=[pl.BlockSpec((bm, k), lambda i, j: (i, 0)),
                  _layer_spec(k, bn, layer),
                  _layer_spec(1, bn, layer),
                  pl.BlockSpec((bm, 1), lambda i, j: (i, 0)),
                  pl.BlockSpec((1, LANES), lambda i, j: (0, 0))],
        out_specs=pl.BlockSpec((bm, bn), lambda i, j: (i, j)),
        out_shape=jax.ShapeDtypeStruct((m, n), BF16),
        scratch_shapes=[pltpu.VMEM((bm, LANES), F32), pltpu.VMEM((bm, LANES), F32)],
        compiler_params=_params(2),
        name="qkv_rope_proj",
    )(x, w, b.reshape(b.shape[0], 1, n), pos, invf)


def swa_attention(qkv, sinks, *, batch, seq):
    t, width = qkv.shape
    kvw = N_KV_HEADS * HEAD_DIM
    qw = width - 2 * kvw
    n_q_heads = qw // HEAD_DIM
    q_per_kv = n_q_heads // N_KV_HEADS
    assert q_per_kv % 2 == 0 and 2 * HEAD_DIM == LANES
    blk = ATTN_BLOCK
    nblk = seq // blk
    cur = lambda b, n: (b * nblk + n, 0)
    kcol = qw // kvw
    body = functools.partial(_attn_body, q_per_kv=q_per_kv)
    return pl.pallas_call(
        body,
        grid=(batch, nblk),
        in_specs=[pl.BlockSpec(memory_space=pltpu.SMEM),
                  pl.BlockSpec((blk, qw), cur),
                  pl.BlockSpec((blk, kvw), lambda b, n: (b * nblk + n, kcol)),
                  pl.BlockSpec((blk, kvw), lambda b, n: (b * nblk + jnp.maximum(n - 1, 0), kcol)),
                  pl.BlockSpec((blk, kvw), lambda b, n: (b * nblk + n, kcol + 1)),
                  pl.BlockSpec((blk, kvw), lambda b, n: (b * nblk + jnp.maximum(n - 1, 0), kcol + 1))],
        out_specs=pl.BlockSpec((blk, qw), cur),
        out_shape=jax.ShapeDtypeStruct((t, qw), BF16),
        compiler_params=_params(2),
        name="swa_attention",
    )(sinks.astype(F32), qkv, qkv, qkv, qkv, qkv)


def _ffn(hn, w_gate, w_up, w_down, layer, casts=None):
    act, w_down_bf16 = swiglu_up(hn, w_gate, w_up, w_down, layer=layer, bm=2048, bn=256)
    if casts is None:
        return matmul(act, w_down_bf16, layer=0, bm=512, bn=512, name="ffn_down")
    return matmul_cast2(act, w_down_bf16, *casts, layer=0, bm=512, bn=512, name="ffn_down")


def kernel(x, positions, norm_pre_mix, norm_post_mix, norm_pre_ffn, norm_post_ffn, s5_lam_re, s5_lam_im, s5_log_step, s5_b_re, s5_b_im, s5_c_re, s5_c_im, s5_d, s5_w_out1, s5_b_out1, s5_w_out2, s5_b_out2, attn_w_qkv, attn_b_qkv, attn_w_o, attn_b_o, attn_sinks, ffn_w_gate, ffn_w_up, ffn_w_down):
    bsz, seq, d = x.shape
    t = bsz * seq
    h = x.reshape(t, d)

    hn = rmsnorm(h, norm_pre_mix[0], F32)
    tables = _s5_tables(s5_lam_re[0], s5_lam_im[0], s5_log_step[0], s5_b_re[0], s5_b_im[0],
                        s5_c_re[0], s5_c_im[0], S5_CHUNK)
    g = s5_scan(hn.reshape(bsz, seq, d), tables, s5_d[0], tc=2048).reshape(t, d)
    m = glu_matmul(g, s5_w_out1, s5_b_out1, s5_w_out2, s5_b_out2, layer=0, bm=1024, bn=512)
    hn = mix_norm(h, m, norm_post_mix[0], norm_pre_ffn[0])
    f, w_qkv, w_o = _ffn(hn, ffn_w_gate, ffn_w_up, ffn_w_down, 0, casts=(attn_w_qkv, attn_w_o))
    h, hn = block_resid_norm(h, m, f, norm_post_mix[0], norm_post_ffn[0], norm_pre_mix[1])

    qkv = qkv_rope_proj(hn, w_qkv, attn_b_qkv, positions, layer=0, bm=1024, bn=1024)
    att = swa_attention(qkv, attn_sinks[0], batch=bsz, seq=seq)
    m = matmul(att, w_o, attn_b_o, layer=0, bm=1024, bn=1024, name="o_proj")
    hn = mix_norm(h, m, norm_post_mix[1], norm_pre_ffn[1])
    f = _ffn(hn, ffn_w_gate, ffn_w_up, ffn_w_down, 1)
    h = block_resid(h, m, f, norm_post_mix[1], norm_post_ffn[1])
    return h.reshape(bsz, seq, d)
```

```python
import functools
import math

import jax
import jax.numpy as jnp
from jax import lax
from jax.experimental import pallas as pl
from jax.experimental.pallas import tpu as pltpu

F32 = jnp.float32
BF16 = jnp.bfloat16

EPS = 1e-6
LANES = 128
S5_GROUP = 16
GROUPS_PER_TILE = LANES // S5_GROUP
S5_CHUNK = 8
HEAD_DIM = 64
N_KV_HEADS = 8
ATTN_BLOCK = 128
ROPE_THETA = 10000.0
ROPE_ROW_CHUNKS = 4
LOG2E = math.log2(math.e)
VMEM_LIMIT = 60 * 1024 * 1024


def _params(n_axes, vmem=VMEM_LIMIT):
    return pltpu.CompilerParams(
        dimension_semantics=("arbitrary",) * n_axes, vmem_limit_bytes=vmem)


def _rms(x, g):
    r = lax.rsqrt(jnp.mean(x * x, axis=-1, keepdims=True) + EPS)
    return x * r * g


def _rmsnorm_body(x_ref, g_ref, o_ref):
    o_ref[...] = _rms(x_ref[...], g_ref[...]).astype(o_ref.dtype)


def _row_call(body, n_rows_in, n_vec_in, out_dtypes, t, d, name):
    streams = n_rows_in + len(out_dtypes)
    rows = 512 if streams <= 3 else 256
    row_spec = pl.BlockSpec((rows, d), lambda i: (i, 0))
    vec_spec = pl.BlockSpec((1, d), lambda i: (0, 0))
    return pl.pallas_call(
        body,
        grid=(t // rows,),
        in_specs=[row_spec] * n_rows_in + [vec_spec] * n_vec_in,
        out_specs=[row_spec] * len(out_dtypes),
        out_shape=[jax.ShapeDtypeStruct((t, d), dt) for dt in out_dtypes],
        compiler_params=_params(1),
        name=name,
    )


def rmsnorm(x, g, out_dtype):
    t, d = x.shape
    return _row_call(_rmsnorm_body, 1, 1, [out_dtype], t, d, "rmsnorm")(x, g.reshape(1, d))[0]


def _mix_norm_body(h_ref, m_ref, gm_ref, gpre_ref, hn_ref):
    h = h_ref[...] + _rms(m_ref[...].astype(F32), gm_ref[...])
    hn_ref[...] = _rms(h, gpre_ref[...]).astype(hn_ref.dtype)


def mix_norm(h, m, g_m, g_pre):
    t, d = h.shape
    return _row_call(_mix_norm_body, 2, 2, [BF16], t, d, "mix_norm")(
        h, m, g_m.reshape(1, d), g_pre.reshape(1, d))[0]


def _block_resid_norm_body(h_ref, m_ref, f_ref, gm_ref, gf_ref, gpre_ref, hout_ref, hn_ref):
    h = h_ref[...] + _rms(m_ref[...].astype(F32), gm_ref[...])
    h = h + _rms(f_ref[...].astype(F32), gf_ref[...])
    hout_ref[...] = h
    hn_ref[...] = _rms(h, gpre_ref[...]).astype(hn_ref.dtype)


def block_resid_norm(h, m, f, g_m, g_f, g_pre):
    t, d = h.shape
    return _row_call(_block_resid_norm_body, 3, 3, [F32, BF16], t, d, "block_resid_norm")(
        h, m, f, g_m.reshape(1, d), g_f.reshape(1, d), g_pre.reshape(1, d))


def _block_resid_body(h_ref, m_ref, f_ref, gm_ref, gf_ref, hout_ref):
    h = h_ref[...] + _rms(m_ref[...].astype(F32), gm_ref[...])
    hout_ref[...] = h + _rms(f_ref[...].astype(F32), gf_ref[...])


def block_resid(h, m, f, g_m, g_f):
    t, d = h.shape
    return _row_call(_block_resid_body, 3, 2, [F32], t, d, "block_resid")(
        h, m, f, g_m.reshape(1, d), g_f.reshape(1, d))[0]


def _mm_bias_body(x_ref, w_ref, b_ref, o_ref):
    acc = jnp.dot(x_ref[...], w_ref[...].astype(BF16), preferred_element_type=F32)
    o_ref[...] = (acc + b_ref[...]).astype(o_ref.dtype)


def _mm_body(x_ref, w_ref, o_ref):
    o_ref[...] = jnp.dot(x_ref[...], w_ref[...].astype(BF16),
                         preferred_element_type=F32).astype(o_ref.dtype)


def _layer_spec(rows, bn, layer):
    return pl.BlockSpec((None, rows, bn), lambda i, j: (layer, 0, j))


def matmul(x, w, b=None, *, layer, bm, bn, out_dtype=BF16, name="matmul"):
    m, k = x.shape
    n = w.shape[2]
    in_specs = [pl.BlockSpec((bm, k), lambda i, j: (i, 0)), _layer_spec(k, bn, layer)]
    args = [x, w]
    body = _mm_body
    if b is not None:
        in_specs.append(_layer_spec(1, bn, layer))
        args.append(b.reshape(b.shape[0], 1, n))
        body = _mm_bias_body
    return pl.pallas_call(
        body,
        grid=(m // bm, n // bn),
        in_specs=in_specs,
        out_specs=pl.BlockSpec((bm, bn), lambda i, j: (i, j)),
        out_shape=jax.ShapeDtypeStruct((m, n), out_dtype),
        compiler_params=_params(2),
        name=name,
    )(*args)


def _cast_slabs(w, layer, steps, step_of):
    layers, rows, cols = w.shape
    slab = rows // steps
    assert slab * steps == rows and slab % 16 == 0
    in_spec = pl.BlockSpec((None, None, slab, cols), lambda i, j: (layer, step_of(i, j), 0, 0))
    out_spec = pl.BlockSpec((None, slab, cols), lambda i, j: (step_of(i, j), 0, 0))
    out_shape = jax.ShapeDtypeStruct((steps, slab, cols), BF16)
    return w.reshape(layers, steps, slab, cols), in_spec, out_spec, out_shape


def _mm_cast2_body(x_ref, w_ref, ca_ref, cb_ref, o_ref, ca_out_ref, cb_out_ref):
    _mm_body(x_ref, w_ref, o_ref)
    ca_out_ref[...] = ca_ref[...].astype(ca_out_ref.dtype)
    cb_out_ref[...] = cb_ref[...].astype(cb_out_ref.dtype)


def matmul_cast2(x, w, cast_a, cast_b, *, layer, bm, bn, name):
    m, k = x.shape
    n = w.shape[2]
    steps_j = n // bn
    steps = (m // bm) * steps_j
    step_of = lambda i, j: i * steps_j + j
    ca, ca_in, ca_out, ca_shape = _cast_slabs(cast_a, 0, steps, step_of)
    cb, cb_in, cb_out, cb_shape = _cast_slabs(cast_b, 0, steps, step_of)
    out, ca16, cb16 = pl.pallas_call(
        _mm_cast2_body,
        grid=(m // bm, steps_j),
        in_specs=[pl.BlockSpec((bm, k), lambda i, j: (i, 0)), _layer_spec(k, bn, layer),
                  ca_in, cb_in],
        out_specs=[pl.BlockSpec((bm, bn), lambda i, j: (i, j)), ca_out, cb_out],
        out_shape=[jax.ShapeDtypeStruct((m, n), BF16), ca_shape, cb_shape],
        compiler_params=_params(2),
        name=name,
    )(x, w, ca, cb)
    return out, ca16.reshape((1,) + cast_a.shape[1:]), cb16.reshape((1,) + cast_b.shape[1:])


def _glu_body(x_ref, w1_ref, b1_ref, w2_ref, b2_ref, o_ref):
    x = x_ref[...]
    a = jnp.dot(x, w1_ref[...].astype(BF16), preferred_element_type=F32) + b1_ref[...]
    b = jnp.dot(x, w2_ref[...].astype(BF16), preferred_element_type=F32) + b2_ref[...]
    o_ref[...] = (a * jax.nn.sigmoid(b)).astype(o_ref.dtype)


def glu_matmul(x, w1, b1, w2, b2, *, layer, bm, bn):
    m, k = x.shape
    n = w1.shape[2]
    w_spec = _layer_spec(k, bn, layer)
    b_spec = _layer_spec(1, bn, layer)
    return pl.pallas_call(
        _glu_body,
        grid=(m // bm, n // bn),
        in_specs=[pl.BlockSpec((bm, k), lambda i, j: (i, 0)),
                  w_spec, b_spec, w_spec, b_spec],
        out_specs=pl.BlockSpec((bm, bn), lambda i, j: (i, j)),
        out_shape=jax.ShapeDtypeStruct((m, n), BF16),
        compiler_params=_params(2),
        name="glu_matmul",
    )(x, w1, b1.reshape(b1.shape[0], 1, n), w2, b2.reshape(b2.shape[0], 1, n))


def _swiglu_up_body(x_ref, wg_ref, wu_ref, wd_ref, o_ref, wd_out_ref):
    x = x_ref[...]
    a = jnp.dot(x, wg_ref[...].astype(BF16), preferred_element_type=F32)
    b = jnp.dot(x, wu_ref[...].astype(BF16), preferred_element_type=F32)
    o_ref[...] = (jax.nn.silu(a) * b).astype(o_ref.dtype)
    wd_out_ref[...] = wd_ref[...].astype(wd_out_ref.dtype)


def swiglu_up(x, wg, wu, wd, *, layer, bm, bn):
    m, k = x.shape
    n = wg.shape[2]
    steps_j = n // bn
    wd4, wd_in, wd_out, wd_shape = _cast_slabs(
        wd, layer, (m // bm) * steps_j, lambda i, j: i * steps_j + j)
    w_spec = _layer_spec(k, bn, layer)
    x_spec = pl.BlockSpec((bm, k), lambda i, j: (i, 0), pipeline_mode=pl.Buffered(1))
    act, wd_bf16 = pl.pallas_call(
        _swiglu_up_body,
        grid=(m // bm, steps_j),
        in_specs=[x_spec, w_spec, w_spec, wd_in],
        out_specs=[pl.BlockSpec((bm, bn), lambda i, j: (i, j)), wd_out],
        out_shape=[jax.ShapeDtypeStruct((m, n), BF16), wd_shape],
        compiler_params=_params(2),
        name="swiglu_up",
    )(x, wg, wu, wd4)
    return act, wd_bf16.reshape((1,) + wd.shape[1:])


def _s5_tables(lam_re, lam_im, log_step, b_re, b_im, c_re, c_im, lc):
    g, p = lam_re.shape
    ch = b_re.shape[-1]
    assert ch == S5_GROUP and 2 * p == LANES
    ng = GROUPS_PER_TILE
    nt = g // ng
    lr, li = lam_re.astype(F32), lam_im.astype(F32)
    step = jnp.exp(log_step.astype(F32))[:, None]
    zr, zi = lr * step, li * step
    taus = jnp.arange(lc + 1, dtype=F32)[:, None, None]
    mag = jnp.exp(zr[None] * taus)
    pr = mag * jnp.cos(zi[None] * taus)
    pi = mag * jnp.sin(zi[None] * taus)
    nr, ni = pr[1] - 1.0, pi[1]
    den = lr * lr + li * li
    qr = (nr * lr + ni * li) / den
    qi = (ni * lr - nr * li) / den
    b_r = jnp.swapaxes(b_re.astype(F32), 1, 2)
    b_i = jnp.swapaxes(b_im.astype(F32), 1, 2)
    br = qr[:, None, :] * b_r - qi[:, None, :] * b_i
    bi = qr[:, None, :] * b_i + qi[:, None, :] * b_r
    cr, ci = c_re.astype(F32), c_im.astype(F32)

    zero = jnp.zeros_like(pr)
    even = (jnp.arange(g) % 2 == 0)[None, :, None]
    halves = lambda a: jnp.concatenate(
        [jnp.where(even, a, zero), jnp.where(even, zero, a)], axis=-1)
    per_tile = lambda a: jnp.swapaxes(a.reshape(lc + 1, nt, ng, a.shape[-1]), 0, 1)
    grp = lambda a: a.reshape((nt, ng) + a.shape[1:])
    cat = lambda *xs: jnp.concatenate(xs, axis=-1)
    return (per_tile(halves(pr)), per_tile(halves(pi)),
            grp(cat(br, br, bi, bi)), grp(cat(cr, cr, ci, ci)),
            pr[lc].reshape(nt, 1, ng * p), pi[lc].reshape(nt, 1, ng * p))


def _split_bf16(x):
    hi = x.astype(BF16)
    return hi, (x - hi.astype(F32)).astype(BF16)


def _dot_nt_3pass(a, b):
    nt = lambda u, v: lax.dot_general(u, v, (((1,), (1,)), ((), ())), preferred_element_type=F32)
    a_hi, a_lo = _split_bf16(a)
    b_hi, b_lo = _split_bf16(b)
    return nt(a_hi, b_hi) + nt(a_hi, b_lo) + nt(a_lo, b_hi)


def _s5_build(prh_ref, pih_ref, bd_ref, cd_ref, m_ref, ws_ref, wyt_ref, wf_ref, cp_ref, lc, nk):
    ws_ref[...] = jnp.zeros_like(ws_ref)
    wyt_ref[...] = jnp.zeros_like(wyt_ref)
    wf_ref[...] = jnp.zeros_like(wf_ref)
    cp_ref[...] = jnp.zeros_like(cp_ref)
    for a in range(GROUPS_PER_TILE):
        bd, cd = bd_ref[0, a], cd_ref[0, a]
        b_r, b_i = bd[:, :LANES], bd[:, LANES:]
        c_r, c_i = cd[:, :LANES], cd[:, LANES:]
        re_tile = slice((a // 2) * LANES, (a // 2 + 1) * LANES)
        im_tile = slice((nk + a // 2) * LANES, (nk + a // 2 + 1) * LANES)
        grp = slice(a * S5_GROUP, (a + 1) * S5_GROUP)
        p_r, p_i = prh_ref[0, 0, a:a + 1, :], pih_ref[0, 0, a:a + 1, :]
        cp_ref[grp, re_tile] = p_r * c_r - p_i * c_i
        cp_ref[grp, im_tile] = -(p_r * c_i + p_i * c_r)
        for s in range(lc):
            rows = slice(s * LANES + a * S5_GROUP, s * LANES + (a + 1) * S5_GROUP)
            p_r, p_i = prh_ref[0, lc - 1 - s, a:a + 1, :], pih_ref[0, lc - 1 - s, a:a + 1, :]
            w_re, w_im = p_r * b_r - p_i * b_i, p_r * b_i + p_i * b_r
            wf_ref[rows, re_tile] = w_re
            wf_ref[rows, im_tile] = w_im
            ws_ref[rows, re_tile] = w_re.astype(BF16)
            ws_ref[rows, im_tile] = w_im.astype(BF16)
            p_r, p_i = prh_ref[0, s + 1, a:a + 1, :], pih_ref[0, s + 1, a:a + 1, :]
            wyt_ref[rows, re_tile] = (p_r * c_r - p_i * c_i).astype(BF16)
            wyt_ref[rows, im_tile] = (-(p_r * c_i + p_i * c_r)).astype(BF16)
    cp = cp_ref[...]
    zero = jnp.zeros((LANES, LANES), BF16)
    for tau in range(lc):
        rows = slice((lc - 1 - tau) * LANES, (lc - tau) * LANES)
        kb = _dot_nt_3pass(wf_ref[rows, :], cp).astype(BF16)
        for s in range(lc - tau):
            t = s + tau
            m_ref[s * LANES:(s + 1) * LANES, t * LANES:(t + 1) * LANES] = kb
    for s in range(lc):
        for t in range(s):
            m_ref[s * LANES:(s + 1) * LANES, t * LANES:(t + 1) * LANES] = zero


def _s5_body(x_ref, prh_ref, pih_ref, bd_ref, cd_ref, are_ref, aim_ref, d_ref, o_ref,
             m_ref, ws_ref, wyt_ref, wf_ref, cp_ref, lhs_ref, s_ref, xp_ref, stage_ref, carry_ref,
             *, lc, nb, nc, ns):
    nk = ns // LANES

    @pl.when(pl.program_id(1) == 0)
    def _():
        _s5_build(prh_ref, pih_ref, bd_ref, cd_ref, m_ref, ws_ref, wyt_ref, wf_ref, cp_ref, lc, nk)
        carry_ref[...] = jnp.zeros_like(carry_ref)

    for b in range(nb):
        for s in range(lc):
            piece = x_ref[b, pl.ds(s, nc, stride=lc), :]
            lhs_ref[b * nc:(b + 1) * nc, s * LANES:(s + 1) * LANES] = piece.astype(BF16)
    lhs = lhs_ref[...]
    s_all = jnp.dot(lhs, ws_ref[...], preferred_element_type=F32)
    for k in range(2 * nk):
        for b in range(nb):
            s_ref[k, pl.ds(b, nc, stride=nb), :] = (
                s_all[b * nc:(b + 1) * nc, k * LANES:(k + 1) * LANES])

    tile_rows = lambda ref, k: jnp.broadcast_to(ref[0, :, k * LANES:(k + 1) * LANES], (nb, LANES))
    ar = [tile_rows(are_ref, k) for k in range(nk)]
    ai = [tile_rows(aim_ref, k) for k in range(nk)]

    per = 8 // nb

    def step(i, carry):
        r = pl.ds(pl.multiple_of(i * 8, 8), 8)
        s_in = [s_ref[k, r, :] for k in range(2 * nk)]
        before = [[] for _ in range(2 * nk)]
        for u in range(per):
            new = []
            for k in range(nk):
                xr, xi = carry[2 * k], carry[2 * k + 1]
                before[k].append(xr)
                before[nk + k].append(xi)
                rows = slice(u * nb, (u + 1) * nb)
                new.append(ar[k] * xr - ai[k] * xi + s_in[k][rows])
                new.append(ar[k] * xi + ai[k] * xr + s_in[nk + k][rows])
            carry = tuple(new)
        for k in range(2 * nk):
            xp_ref[k, r, :] = jnp.concatenate(before[k], axis=0)
        return carry

    carry = lax.fori_loop(0, nc // per, step,
                          tuple(carry_ref[k, 0:nb, :] for k in range(2 * nk)))
    for k in range(2 * nk):
        carry_ref[k, 0:nb, :] = carry[k]

    xp = jnp.concatenate(
        [jnp.concatenate([xp_ref[k, pl.ds(b, nc, stride=nb), :] for b in range(nb)], axis=0)
         for k in range(2 * nk)], axis=-1)
    tw = 2 * LANES
    y_intra = jnp.concatenate(
        [jnp.dot(lhs[:, :t0 + tw], m_ref[:t0 + tw, t0:t0 + tw], preferred_element_type=F32)
         for t0 in range(0, lc * LANES, tw)], axis=-1)
    y = y_intra + lax.dot_general(xp.astype(BF16), wyt_ref[...], (((1,), (1,)), ((), ())),
                                  preferred_element_type=F32)
    for b in range(nb):
        for t in range(lc):
            stage_ref[b, pl.ds(t, nc, stride=lc), :] = (
                y[b * nc:(b + 1) * nc, t * LANES:(t + 1) * LANES])
    yy = stage_ref[...] + d_ref[...] * x_ref[...]
    o_ref[...] = jax.nn.gelu(yy).astype(o_ref.dtype)


def s5_scan(hn, tables, d_skip, *, tc, lc=S5_CHUNK):
    prh, pih, bd, cd, a_re, a_im = tables
    ng = GROUPS_PER_TILE
    nb, l, d = hn.shape
    nt = d // LANES
    nc = tc // lc
    ns = a_re.shape[-1]
    rows = nb * nc
    body = functools.partial(_s5_body, lc=lc, nb=nb, nc=nc, ns=ns)
    tile4 = lambda j, i: (j, 0, 0, 0)
    tile3 = lambda j, i: (j, 0, 0)
    assert 8 % nb == 0
    return pl.pallas_call(
        body,
        grid=(nt, l // tc),
        in_specs=[pl.BlockSpec((nb, tc, LANES), lambda j, i: (0, i, j)),
                  pl.BlockSpec((1, lc + 1, ng, LANES), tile4),
                  pl.BlockSpec((1, lc + 1, ng, LANES), tile4),
                  pl.BlockSpec((1, ng, S5_GROUP, 2 * LANES), tile4),
                  pl.BlockSpec((1, ng, S5_GROUP, 2 * LANES), tile4),
                  pl.BlockSpec((1, 1, ns), tile3),
                  pl.BlockSpec((1, 1, ns), tile3),
                  pl.BlockSpec((1, 1, LANES), lambda j, i: (0, 0, j))],
        out_specs=pl.BlockSpec((nb, tc, LANES), lambda j, i: (0, i, j)),
        out_shape=jax.ShapeDtypeStruct((nb, l, d), BF16),
        scratch_shapes=[pltpu.VMEM((lc * LANES, lc * LANES), BF16),
                        pltpu.VMEM((lc * LANES, 2 * ns), BF16),
                        pltpu.VMEM((lc * LANES, 2 * ns), BF16),
                        pltpu.VMEM((lc * LANES, 2 * ns), F32),
                        pltpu.VMEM((LANES, 2 * ns), F32),
                        pltpu.VMEM((rows, lc * LANES), BF16),
                        pltpu.VMEM((2 * ns // LANES, rows, LANES), F32),
                        pltpu.VMEM((2 * ns // LANES, rows, LANES), F32),
                        pltpu.VMEM((nb, tc, LANES), F32),
                        pltpu.VMEM((2 * ns // LANES, 8, LANES), F32)],
        compiler_params=_params(2),
        name="s5_scan",
    )(hn, prh, pih, bd, cd, a_re, a_im, d_skip.reshape(1, 1, d).astype(F32))


def _rope(x, cos, sin_signed, first_half):
    w = x.shape[1]
    reps = w // LANES
    cosw = jnp.tile(cos, (1, reps))
    sinw = jnp.tile(sin_signed, (1, reps))
    sel = jnp.tile(first_half, (1, reps))
    half = HEAD_DIM // 2
    swapped = jnp.where(sel, pltpu.roll(x, w - half, 1), pltpu.roll(x, half, 1))
    return x * cosw + swapped * sinw


def _attn_body(sink_ref, q_ref, kc_ref, kp_ref, vc_ref, vp_ref, o_ref, *, q_per_kv):
    n = pl.program_id(1)
    blk = ATTN_BLOCK
    q = q_ref[...]
    k = jnp.concatenate([kp_ref[...], kc_ref[...]], axis=0).astype(F32)
    v = jnp.concatenate([vp_ref[...], vc_ref[...]], axis=0).astype(F32)

    from_prev = (lax.broadcasted_iota(jnp.int32, (blk, blk), 1)
                 > lax.broadcasted_iota(jnp.int32, (blk, blk), 0))
    no_prev = jnp.where(n > 0, 0.0, -jnp.inf).astype(F32)

    low = lax.broadcasted_iota(jnp.int32, (2 * blk, LANES), 1) < HEAD_DIM

    def split(tile, head_in_low):
        moved = pltpu.roll(tile, HEAD_DIM, 1)
        in_lo, in_hi = (tile, moved) if head_in_low else (moved, tile)
        return (jnp.where(low, in_lo, 0.0).astype(BF16),
                jnp.where(low, 0.0, in_hi).astype(BF16))

    for h in range(N_KV_HEADS):
        tile = slice((h // 2) * LANES, (h // 2 + 1) * LANES)
        k_lo, k_hi = split(k[:, tile], h % 2 == 0)
        v_lo, v_hi = split(v[:, tile], h % 2 == 0)
        for i in range(q_per_kv // 2):
            head = h * q_per_kv + 2 * i
            qt = q[:, (head // 2) * LANES:(head // 2 + 1) * LANES]
            acc = None
            for kk, vv, a in ((k_lo, v_lo, head), (k_hi, v_hi, head + 1)):
                s = lax.dot_general(qt, kk, (((1,), (1,)), ((), ())),
                                    preferred_element_type=F32)
                z = jnp.where(from_prev, s[:, :blk] + no_prev, s[:, blk:])
                sink = sink_ref[a] * LOG2E
                m = jnp.maximum(jnp.max(z, axis=-1, keepdims=True), sink)
                e = jnp.exp2(z - m)
                denom = jnp.sum(e, axis=-1, keepdims=True) + jnp.exp2(sink - m)
                p = jnp.concatenate([jnp.where(from_prev, e, 0.0), jnp.where(from_prev, 0.0, e)],
                                    axis=1).astype(BF16)
                o = jnp.dot(p, vv, preferred_element_type=F32) * (1.0 / denom)
                acc = o if acc is None else acc + o
            o_ref[:, (head // 2) * LANES:(head // 2 + 1) * LANES] = acc.astype(o_ref.dtype)


def _qkv_rope_body(x_ref, w_ref, b_ref, pos_ref, invf_ref, o_ref, cos_ref, sin_ref,
                   *, q_cols, rope_cols):
    j = pl.program_id(1)
    lane = lax.broadcasted_iota(jnp.int32, cos_ref.shape, 1)
    first_half = (lane % HEAD_DIM) < (HEAD_DIM // 2)

    @pl.when(j == 0)
    def _():
        ang = pos_ref[...].astype(F32) * invf_ref[...]
        sin = jnp.sin(ang)
        cos_ref[...] = jnp.cos(ang)
        sin_ref[...] = jnp.where(first_half, -sin, sin)

    w = w_ref[...].astype(BF16)
    bn = o_ref.shape[1]
    col = j * bn + lax.broadcasted_iota(jnp.int32, (1, bn), 1)
    scale = jnp.where(col < q_cols, HEAD_DIM ** -0.5 * LOG2E, 1.0)
    rotated = col < rope_cols
    chunk = x_ref.shape[0] // ROPE_ROW_CHUNKS
    lane_c = lax.broadcasted_iota(jnp.int32, (chunk, LANES), 1)
    first_half_c = (lane_c % HEAD_DIM) < (HEAD_DIM // 2)
    for r in range(ROPE_ROW_CHUNKS):
        rows = slice(r * chunk, (r + 1) * chunk)
        acc = jnp.dot(x_ref[rows, :], w, preferred_element_type=F32) + b_ref[...]
        rot = _rope(acc, cos_ref[rows, :], sin_ref[rows, :], first_half_c) * scale
        o_ref[rows, :] = jnp.where(rotated, rot, acc).astype(o_ref.dtype)


def qkv_rope_proj(x, w, b, positions, *, layer, bm, bn):
    m, k = x.shape
    n = w.shape[2]
    kvw = N_KV_HEADS * HEAD_DIM
    qw = n - 2 * kvw
    half = HEAD_DIM // 2
    inv_freq = jnp.power(ROPE_THETA, -jnp.arange(half, dtype=F32) / half)
    invf = jnp.tile(inv_freq, LANES // half).reshape(1, LANES)
    pos = positions.reshape(m, 1).astype(jnp.int32)
    body = functools.partial(_qkv_rope_body, q_cols=qw, rope_cols=qw + kvw)
    return pl.pallas_call(
        body,
        grid=(m // bm, n // bn),
        in_specs=[pl.BlockSpec((bm, k), lambda i, j: (i, 0)),
                  _layer_spec(k, bn, layer),
                  _layer_spec(1, bn, layer),
                  pl.BlockSpec((bm, 1), lambda i, j: (i, 0)),
                  pl.BlockSpec((1, LANES), lambda i, j: (0, 0))],
        out_specs=pl.BlockSpec((bm, bn), lambda i, j: (i, j)),
        out_shape=jax.ShapeDtypeStruct((m, n), BF16),
        scratch_shapes=[pltpu.VMEM((bm, LANES), F32), pltpu.VMEM((bm, LANES), F32)],
        compiler_params=_params(2),
        name="qkv_rope_proj",
    )(x, w, b.reshape(b.shape[0], 1, n), pos, invf)


def swa_attention(qkv, sinks, *, batch, seq):
    t, width = qkv.shape
    kvw = N_KV_HEADS * HEAD_DIM
    qw = width - 2 * kvw
    n_q_heads = qw // HEAD_DIM
    q_per_kv = n_q_heads // N_KV_HEADS
    assert q_per_kv % 2 == 0 and 2 * HEAD_DIM == LANES
    blk = ATTN_BLOCK
    nblk = seq // blk
    cur = lambda b, n: (b * nblk + n, 0)
    kcol = qw // kvw
    body = functools.partial(_attn_body, q_per_kv=q_per_kv)
    return pl.pallas_call(
        body,
        grid=(batch, nblk),
        in_specs=[pl.BlockSpec(memory_space=pltpu.SMEM),
                  pl.BlockSpec((blk, qw), cur),
                  pl.BlockSpec((blk, kvw), lambda b, n: (b * nblk + n, kcol)),
                  pl.BlockSpec((blk, kvw), lambda b, n: (b * nblk + jnp.maximum(n - 1, 0), kcol)),
                  pl.BlockSpec((blk, kvw), lambda b, n: (b * nblk + n, kcol + 1)),
                  pl.BlockSpec((blk, kvw), lambda b, n: (b * nblk + jnp.maximum(n - 1, 0), kcol + 1))],
        out_specs=pl.BlockSpec((blk, qw), cur),
        out_shape=jax.ShapeDtypeStruct((t, qw), BF16),
        compiler_params=_params(2),
        name="swa_attention",
    )(sinks.astype(F32), qkv, qkv, qkv, qkv, qkv)


def _ffn(hn, w_gate, w_up, w_down, layer, casts=None):
    act, w_down_bf16 = swiglu_up(hn, w_gate, w_up, w_down, layer=layer, bm=2048, bn=256)
    if casts is None:
        return matmul(act, w_down_bf16, layer=0, bm=512, bn=512, name="ffn_down")
    return matmul_cast2(act, w_down_bf16, *casts, layer=0, bm=512, bn=512, name="ffn_down")


def kernel(x, positions, norm_pre_mix, norm_post_mix, norm_pre_ffn, norm_post_ffn, s5_lam_re, s5_lam_im, s5_log_step, s5_b_re, s5_b_im, s5_c_re, s5_c_im, s5_d, s5_w_out1, s5_b_out1, s5_w_out2, s5_b_out2, attn_w_qkv, attn_b_qkv, attn_w_o, attn_b_o, attn_sinks, ffn_w_gate, ffn_w_up, ffn_w_down):
    bsz, seq, d = x.shape
    t = bsz * seq
    h = x.reshape(t, d)

    hn = rmsnorm(h, norm_pre_mix[0], F32)
    tables = _s5_tables(s5_lam_re[0], s5_lam_im[0], s5_log_step[0], s5_b_re[0], s5_b_im[0],
                        s5_c_re[0], s5_c_im[0], S5_CHUNK)
    g = s5_scan(hn.reshape(bsz, seq, d), tables, s5_d[0], tc=2048).reshape(t, d)
    m = glu_matmul(g, s5_w_out1, s5_b_out1, s5_w_out2, s5_b_out2, layer=0, bm=1024, bn=512)
    hn = mix_norm(h, m, norm_post_mix[0], norm_pre_ffn[0])
    f, w_qkv, w_o = _ffn(hn, ffn_w_gate, ffn_w_up, ffn_w_down, 0, casts=(attn_w_qkv, attn_w_o))
    h, hn = block_resid_norm(h, m, f, norm_post_mix[0], norm_post_ffn[0], norm_pre_mix[1])

    qkv = qkv_rope_proj(hn, w_qkv, attn_b_qkv, positions, layer=0, bm=1024, bn=1024)
    att = swa_attention(qkv, attn_sinks[0], batch=bsz, seq=seq)
    m = matmul(att, w_o, attn_b_o, layer=0, bm=1024, bn=1024, name="o_proj")
    hn = mix_norm(h, m, norm_post_mix[1], norm_pre_ffn[1])
    f = _ffn(hn, ffn_w_gate, ffn_w_up, ffn_w_down, 1)
    h = block_resid(h, m, f, norm_post_mix[1], norm_post_ffn[1])
    return h.reshape(bsz, seq, d)
```

```python
import functools
import math

import jax
import jax.numpy as jnp
from jax import lax
from jax.experimental import pallas as pl
from jax.experimental.pallas import tpu as pltpu

F32 = jnp.float32
BF16 = jnp.bfloat16

EPS = 1e-6
LANES = 128
S5_GROUP = 16
GROUPS_PER_TILE = LANES // S5_GROUP
S5_CHUNK = 8
HEAD_DIM = 64
N_KV_HEADS = 8
ATTN_BLOCK = 128
ROPE_THETA = 10000.0
ROPE_ROW_CHUNKS = 4
LOG2E = math.log2(math.e)
VMEM_LIMIT = 60 * 1024 * 1024


def _params(n_axes, vmem=VMEM_LIMIT):
    return pltpu.CompilerParams(
        dimension_semantics=("arbitrary",) * n_axes, vmem_limit_bytes=vmem)


def _rms(x, g):
    r = lax.rsqrt(jnp.mean(x * x, axis=-1, keepdims=True) + EPS)
    return x * r * g


def _rmsnorm_body(x_ref, g_ref, o_ref):
    o_ref[...] = _rms(x_ref[...], g_ref[...]).astype(o_ref.dtype)


def _row_call(body, n_rows_in, n_vec_in, out_dtypes, t, d, name):
    streams = n_rows_in + len(out_dtypes)
    rows = 512 if streams <= 3 else 256
    row_spec = pl.BlockSpec((rows, d), lambda i: (i, 0))
    vec_spec = pl.BlockSpec((1, d), lambda i: (0, 0))
    return pl.pallas_call(
        body,
        grid=(t // rows,),
        in_specs=[row_spec] * n_rows_in + [vec_spec] * n_vec_in,
        out_specs=[row_spec] * len(out_dtypes),
        out_shape=[jax.ShapeDtypeStruct((t, d), dt) for dt in out_dtypes],
        compiler_params=_params(1),
        name=name,
    )


def rmsnorm(x, g, out_dtype):
    t, d = x.shape
    return _row_call(_rmsnorm_body, 1, 1, [out_dtype], t, d, "rmsnorm")(x, g.reshape(1, d))[0]


def _mix_norm_body(h_ref, m_ref, gm_ref, gpre_ref, hn_ref):
    h = h_ref[...] + _rms(m_ref[...].astype(F32), gm_ref[...])
    hn_ref[...] = _rms(h, gpre_ref[...]).astype(hn_ref.dtype)


def mix_norm(h, m, g_m, g_pre):
    t, d = h.shape
    return _row_call(_mix_norm_body, 2, 2, [BF16], t, d, "mix_norm")(
        h, m, g_m.reshape(1, d), g_pre.reshape(1, d))[0]


def _block_resid_norm_body(h_ref, m_ref, f_ref, gm_ref, gf_ref, gpre_ref, hout_ref, hn_ref):
    h = h_ref[...] + _rms(m_ref[...].astype(F32), gm_ref[...])
    h = h + _rms(f_ref[...].astype(F32), gf_ref[...])
    hout_ref[...] = h
    hn_ref[...] = _rms(h, gpre_ref[...]).astype(hn_ref.dtype)


def block_resid_norm(h, m, f, g_m, g_f, g_pre):
    t, d = h.shape
    return _row_call(_block_resid_norm_body, 3, 3, [F32, BF16], t, d, "block_resid_norm")(
        h, m, f, g_m.reshape(1, d), g_f.reshape(1, d), g_pre.reshape(1, d))


def _block_resid_body(h_ref, m_ref, f_ref, gm_ref, gf_ref, hout_ref):
    h = h_ref[...] + _rms(m_ref[...].astype(F32), gm_ref[...])
    hout_ref[...] = h + _rms(f_ref[...].astype(F32), gf_ref[...])


def block_resid(h, m, f, g_m, g_f):
    t, d = h.shape
    return _row_call(_block_resid_body, 3, 2, [F32], t, d, "block_resid")(
        h, m, f, g_m.reshape(1, d), g_f.reshape(1, d))[0]


def _mm_bias_body(x_ref, w_ref, b_ref, o_ref):
    acc = jnp.dot(x_ref[...], w_ref[...].astype(BF16), preferred_element_type=F32)
    o_ref[...] = (acc + b_ref[...]).astype(o_ref.dtype)


def _mm_body(x_ref, w_ref, o_ref):
    o_ref[...] = jnp.dot(x_ref[...], w_ref[...].astype(BF16),
                         preferred_element_type=F32).astype(o_ref.dtype)


def _layer_spec(rows, bn, layer):
    return pl.BlockSpec((None, rows, bn), lambda i, j: (layer, 0, j))


def matmul(x, w, b=None, *, layer, bm, bn, out_dtype=BF16, name="matmul"):
    m, k = x.shape
    n = w.shape[2]
    in_specs = [pl.BlockSpec((bm, k), lambda i, j: (i, 0)), _layer_spec(k, bn, layer)]
    args = [x, w]
    body = _mm_body
    if b is not None:
        in_specs.append(_layer_spec(1, bn, layer))
        args.append(b.reshape(b.shape[0], 1, n))
        body = _mm_bias_body
    return pl.pallas_call(
        body,
        grid=(m // bm, n // bn),
        in_specs=in_specs,
        out_specs=pl.BlockSpec((bm, bn), lambda i, j: (i, j)),
        out_shape=jax.ShapeDtypeStruct((m, n), out_dtype),
        compiler_params=_params(2),
        name=name,
    )(*args)


def _cast_slabs(w, layer, steps, step_of):
    layers, rows, cols = w.shape
    slab = rows // steps
    assert slab * steps == rows and slab % 16 == 0
    in_spec = pl.BlockSpec((None, None, slab, cols), lambda i, j: (layer, step_of(i, j), 0, 0))
    out_spec = pl.BlockSpec((None, slab, cols), lambda i, j: (step_of(i, j), 0, 0))
    out_shape = jax.ShapeDtypeStruct((steps, slab, cols), BF16)
    return w.reshape(layers, steps, slab, cols), in_spec, out_spec, out_shape


def _mm_cast2_body(x_ref, w_ref, ca_ref, cb_ref, o_ref, ca_out_ref, cb_out_ref):
    _mm_body(x_ref, w_ref, o_ref)
    ca_out_ref[...] = ca_ref[...].astype(ca_out_ref.dtype)
    cb_out_ref[...] = cb_ref[...].astype(cb_out_ref.dtype)


def matmul_cast2(x, w, cast_a, cast_b, *, layer, bm, bn, name):
    m, k = x.shape
    n = w.shape[2]
    steps_j = n // bn
    steps = (m // bm) * steps_j
    step_of = lambda i, j: i * steps_j + j
    ca, ca_in, ca_out, ca_shape = _cast_slabs(cast_a, 0, steps, step_of)
    cb, cb_in, cb_out, cb_shape = _cast_slabs(cast_b, 0, steps, step_of)
    out, ca16, cb16 = pl.pallas_call(
        _mm_cast2_body,
        grid=(m // bm, steps_j),
        in_specs=[pl.BlockSpec((bm, k), lambda i, j: (i, 0)), _layer_spec(k, bn, layer),
                  ca_in, cb_in],
        out_specs=[pl.BlockSpec((bm, bn), lambda i, j: (i, j)), ca_out, cb_out],
        out_shape=[jax.ShapeDtypeStruct((m, n), BF16), ca_shape, cb_shape],
        compiler_params=_params(2),
        name=name,
    )(x, w, ca, cb)
    return out, ca16.reshape((1,) + cast_a.shape[1:]), cb16.reshape((1,) + cast_b.shape[1:])


def _glu_body(x_ref, w1_ref, b1_ref, w2_ref, b2_ref, o_ref):
    x = x_ref[...]
    a = jnp.dot(x, w1_ref[...].astype(BF16), preferred_element_type=F32) + b1_ref[...]
    b = jnp.dot(x, w2_ref[...].astype(BF16), preferred_element_type=F32) + b2_ref[...]
    o_ref[...] = (a * jax.nn.sigmoid(b)).astype(o_ref.dtype)


def glu_matmul(x, w1, b1, w2, b2, *, layer, bm, bn):
    m, k = x.shape
    n = w1.shape[2]
    w_spec = _layer_spec(k, bn, layer)
    b_spec = _layer_spec(1, bn, layer)
    return pl.pallas_call(
        _glu_body,
        grid=(m // bm, n // bn),
        in_specs=[pl.BlockSpec((bm, k), lambda i, j: (i, 0)),
                  w_spec, b_spec, w_spec, b_spec],
        out_specs=pl.BlockSpec((bm, bn), lambda i, j: (i, j)),
        out_shape=jax.ShapeDtypeStruct((m, n), BF16),
        compiler_params=_params(2),
        name="glu_matmul",
    )(x, w1, b1.reshape(b1.shape[0], 1, n), w2, b2.reshape(b2.shape[0], 1, n))


def _swiglu_up_body(x_ref, wg_ref, wu_ref, wd_ref, o_ref, wd_out_ref):
    x = x_ref[...]
    a = jnp.dot(x, wg_ref[...].astype(BF16), preferred_element_type=F32)
    b = jnp.dot(x, wu_ref[...].astype(BF16), preferred_element_type=F32)
    o_ref[...] = (jax.nn.silu(a) * b).astype(o_ref.dtype)
    wd_out_ref[...] = wd_ref[...].astype(wd_out_ref.dtype)


def swiglu_up(x, wg, wu, wd, *, layer, bm, bn):
    m, k = x.shape
    n = wg.shape[2]
    steps_j = n // bn
    wd4, wd_in, wd_out, wd_shape = _cast_slabs(
        wd, layer, (m // bm) * steps_j, lambda i, j: i * steps_j + j)
    w_spec = _layer_spec(k, bn, layer)
    x_spec = pl.BlockSpec((bm, k), lambda i, j: (i, 0), pipeline_mode=pl.Buffered(1))
    act, wd_bf16 = pl.pallas_call(
        _swiglu_up_body,
        grid=(m // bm, steps_j),
        in_specs=[x_spec, w_spec, w_spec, wd_in],
        out_specs=[pl.BlockSpec((bm, bn), lambda i, j: (i, j)), wd_out],
        out_shape=[jax.ShapeDtypeStruct((m, n), BF16), wd_shape],
        compiler_params=_params(2),
        name="swiglu_up",
    )(x, wg, wu, wd4)
    return act, wd_bf16.reshape((1,) + wd.shape[1:])


def _s5_tables(lam_re, lam_im, log_step, b_re, b_im, c_re, c_im, lc):
    g, p = lam_re.shape
    ch = b_re.shape[-1]
    assert ch == S5_GROUP and 2 * p == LANES
    ng = GROUPS_PER_TILE
    nt = g // ng
    lr, li = lam_re.astype(F32), lam_im.astype(F32)
    step = jnp.exp(log_step.astype(F32))[:, None]
    zr, zi = lr * step, li * step
    taus = jnp.arange(lc + 1, dtype=F32)[:, None, None]
    mag = jnp.exp(zr[None] * taus)
    pr = mag * jnp.cos(zi[None] * taus)
    pi = mag * jnp.sin(zi[None] * taus)
    nr, ni = pr[1] - 1.0, pi[1]
    den = lr * lr + li * li
    qr = (nr * lr + ni * li) / den
    qi = (ni * lr - nr * li) / den
    b_r = jnp.swapaxes(b_re.astype(F32), 1, 2)
    b_i = jnp.swapaxes(b_im.astype(F32), 1, 2)
    br = qr[:, None, :] * b_r - qi[:, None, :] * b_i
    bi = qr[:, None, :] * b_i + qi[:, None, :] * b_r
    cr, ci = c_re.astype(F32), c_im.astype(F32)

    zero = jnp.zeros_like(pr)
    even = (jnp.arange(g) % 2 == 0)[None, :, None]
    halves = lambda a: jnp.concatenate(
        [jnp.where(even, a, zero), jnp.where(even, zero, a)], axis=-1)
    per_tile = lambda a: jnp.swapaxes(a.reshape(lc + 1, nt, ng, a.shape[-1]), 0, 1)
    grp = lambda a: a.reshape((nt, ng) + a.shape[1:])
    cat = lambda *xs: jnp.concatenate(xs, axis=-1)
    return (per_tile(halves(pr)), per_tile(halves(pi)),
            grp(cat(br, br, bi, bi)), grp(cat(cr, cr, ci, ci)),
            pr[lc].reshape(nt, 1, ng * p), pi[lc].reshape(nt, 1, ng * p))


def _split_bf16(x):
    hi = x.astype(BF16)
    return hi, (x - hi.astype(F32)).astype(BF16)


def _dot_nt_3pass(a, b):
    nt = lambda u, v: lax.dot_general(u, v, (((1,), (1,)), ((), ())), preferred_element_type=F32)
    a_hi, a_lo = _split_bf16(a)
    b_hi, b_lo = _split_bf16(b)
    return nt(a_hi, b_hi) + nt(a_hi, b_lo) + nt(a_lo, b_hi)


def _s5_build(prh_ref, pih_ref, bd_ref, cd_ref, m_ref, ws_ref, wyt_ref, wf_ref, cp_ref, lc, nk):
    @pl.when(pl.program_id(0) == 0)
    def _():
        ws_ref[...] = jnp.zeros_like(ws_ref)
        wyt_ref[...] = jnp.zeros_like(wyt_ref)
        wf_ref[...] = jnp.zeros_like(wf_ref)
        cp_ref[...] = jnp.zeros_like(cp_ref)
        m_ref[...] = jnp.zeros_like(m_ref)

    for a in range(GROUPS_PER_TILE):
        bd, cd = bd_ref[0, a], cd_ref[0, a]
        b_r, b_i = bd[:, :LANES], bd[:, LANES:]
        c_r, c_i = cd[:, :LANES], cd[:, LANES:]
        re_tile = slice((a // 2) * LANES, (a // 2 + 1) * LANES)
        im_tile = slice((nk + a // 2) * LANES, (nk + a // 2 + 1) * LANES)
        grp = slice(a * S5_GROUP, (a + 1) * S5_GROUP)
        p_r, p_i = prh_ref[0, 0, a:a + 1, :], pih_ref[0, 0, a:a + 1, :]
        cp_ref[grp, re_tile] = p_r * c_r - p_i * c_i
        cp_ref[grp, im_tile] = -(p_r * c_i + p_i * c_r)
        for s in range(lc):
            rows = slice(s * LANES + a * S5_GROUP, s * LANES + (a + 1) * S5_GROUP)
            p_r, p_i = prh_ref[0, lc - 1 - s, a:a + 1, :], pih_ref[0, lc - 1 - s, a:a + 1, :]
            w_re, w_im = p_r * b_r - p_i * b_i, p_r * b_i + p_i * b_r
            wf_ref[rows, re_tile] = w_re
            wf_ref[rows, im_tile] = w_im
            ws_ref[rows, re_tile] = w_re.astype(BF16)
            ws_ref[rows, im_tile] = w_im.astype(BF16)
            p_r, p_i = prh_ref[0, s + 1, a:a + 1, :], pih_ref[0, s + 1, a:a + 1, :]
            wyt_ref[rows, re_tile] = (p_r * c_r - p_i * c_i).astype(BF16)
            wyt_ref[rows, im_tile] = (-(p_r * c_i + p_i * c_r)).astype(BF16)
    cp = cp_ref[...]
    for tau in range(lc):
        rows = slice((lc - 1 - tau) * LANES, (lc - tau) * LANES)
        kb = _dot_nt_3pass(wf_ref[rows, :], cp).astype(BF16)
        for s in range(lc - tau):
            t = s + tau
            m_ref[s * LANES:(s + 1) * LANES, t * LANES:(t + 1) * LANES] = kb


def _s5_body(x_ref, prh_ref, pih_ref, bd_ref, cd_ref, are_ref, aim_ref, d_ref, o_ref,
             m_ref, ws_ref, wyt_ref, wf_ref, cp_ref, lhs_ref, s_ref, xp_ref, stage_ref, carry_ref,
             *, lc, nb, nc, ns):
    nk = ns // LANES

    @pl.when(pl.program_id(1) == 0)
    def _():
        _s5_build(prh_ref, pih_ref, bd_ref, cd_ref, m_ref, ws_ref, wyt_ref, wf_ref, cp_ref, lc, nk)
        carry_ref[...] = jnp.zeros_like(carry_ref)

    for b in range(nb):
        for s in range(lc):
            piece = x_ref[b, pl.ds(s, nc, stride=lc), :]
            lhs_ref[b * nc:(b + 1) * nc, s * LANES:(s + 1) * LANES] = piece.astype(BF16)
    lhs = lhs_ref[...]
    s_all = jnp.dot(lhs, ws_ref[...], preferred_element_type=F32)
    for k in range(2 * nk):
        for b in range(nb):
            s_ref[k, pl.ds(b, nc, stride=nb), :] = (
                s_all[b * nc:(b + 1) * nc, k * LANES:(k + 1) * LANES])

    tile_rows = lambda ref, k: jnp.broadcast_to(ref[0, :, k * LANES:(k + 1) * LANES], (nb, LANES))
    ar = [tile_rows(are_ref, k) for k in range(nk)]
    ai = [tile_rows(aim_ref, k) for k in range(nk)]

    per = 8 // nb

    def step(i, carry):
        r = pl.ds(pl.multiple_of(i * 8, 8), 8)
        s_in = [s_ref[k, r, :] for k in range(2 * nk)]
        before = [[] for _ in range(2 * nk)]
        for u in range(per):
            new = []
            for k in range(nk):
                xr, xi = carry[2 * k], carry[2 * k + 1]
                before[k].append(xr)
                before[nk + k].append(xi)
                rows = slice(u * nb, (u + 1) * nb)
                new.append(ar[k] * xr - ai[k] * xi + s_in[k][rows])
                new.append(ar[k] * xi + ai[k] * xr + s_in[nk + k][rows])
            carry = tuple(new)
        for k in range(2 * nk):
            xp_ref[k, r, :] = jnp.concatenate(before[k], axis=0)
        return carry

    carry = lax.fori_loop(0, nc // per, step,
                          tuple(carry_ref[k, 0:nb, :] for k in range(2 * nk)))
    for k in range(2 * nk):
        carry_ref[k, 0:nb, :] = carry[k]

    xp = jnp.concatenate(
        [jnp.concatenate([xp_ref[k, pl.ds(b, nc, stride=nb), :] for b in range(nb)], axis=0)
         for k in range(2 * nk)], axis=-1)
    tw = 2 * LANES
    y_intra = jnp.concatenate(
        [jnp.dot(lhs[:, :t0 + tw], m_ref[:t0 + tw, t0:t0 + tw], preferred_element_type=F32)
         for t0 in range(0, lc * LANES, tw)], axis=-1)
    y = y_intra + lax.dot_general(xp.astype(BF16), wyt_ref[...], (((1,), (1,)), ((), ())),
                                  preferred_element_type=F32)
    for b in range(nb):
        for t in range(lc):
            stage_ref[b, pl.ds(t, nc, stride=lc), :] = (
                y[b * nc:(b + 1) * nc, t * LANES:(t + 1) * LANES])
    yy = stage_ref[...] + d_ref[...] * x_ref[...]
    o_ref[...] = jax.nn.gelu(yy).astype(o_ref.dtype)


def s5_scan(hn, tables, d_skip, *, tc, lc=S5_CHUNK):
    prh, pih, bd, cd, a_re, a_im = tables
    ng = GROUPS_PER_TILE
    nb, l, d = hn.shape
    nt = d // LANES
    nc = tc // lc
    ns = a_re.shape[-1]
    rows = nb * nc
    body = functools.partial(_s5_body, lc=lc, nb=nb, nc=nc, ns=ns)
    tile4 = lambda j, i: (j, 0, 0, 0)
    tile3 = lambda j, i: (j, 0, 0)
    assert 8 % nb == 0
    return pl.pallas_call(
        body,
        grid=(nt, l // tc),
        in_specs=[pl.BlockSpec((nb, tc, LANES), lambda j, i: (0, i, j)),
                  pl.BlockSpec((1, lc + 1, ng, LANES), tile4),
                  pl.BlockSpec((1, lc + 1, ng, LANES), tile4),
                  pl.BlockSpec((1, ng, S5_GROUP, 2 * LANES), tile4),
                  pl.BlockSpec((1, ng, S5_GROUP, 2 * LANES), tile4),
                  pl.BlockSpec((1, 1, ns), tile3),
                  pl.BlockSpec((1, 1, ns), tile3),
                  pl.BlockSpec((1, 1, LANES), lambda j, i: (0, 0, j))],
        out_specs=pl.BlockSpec((nb, tc, LANES), lambda j, i: (0, i, j)),
        out_shape=jax.ShapeDtypeStruct((nb, l, d), BF16),
        scratch_shapes=[pltpu.VMEM((lc * LANES, lc * LANES), BF16),
                        pltpu.VMEM((lc * LANES, 2 * ns), BF16),
                        pltpu.VMEM((lc * LANES, 2 * ns), BF16),
                        pltpu.VMEM((lc * LANES, 2 * ns), F32),
                        pltpu.VMEM((LANES, 2 * ns), F32),
                        pltpu.VMEM((rows, lc * LANES), BF16),
                        pltpu.VMEM((2 * ns // LANES, rows, LANES), F32),
                        pltpu.VMEM((2 * ns // LANES, rows, LANES), F32),
                        pltpu.VMEM((nb, tc, LANES), F32),
                        pltpu.VMEM((2 * ns // LANES, 8, LANES), F32)],
        compiler_params=_params(2),
        name="s5_scan",
    )(hn, prh, pih, bd, cd, a_re, a_im, d_skip.reshape(1, 1, d).astype(F32))


def _rope(x, cos, sin_signed, first_half):
    w = x.shape[1]
    reps = w // LANES
    cosw = jnp.tile(cos, (1, reps))
    sinw = jnp.tile(sin_signed, (1, reps))
    sel = jnp.tile(first_half, (1, reps))
    half = HEAD_DIM // 2
    swapped = jnp.where(sel, pltpu.roll(x, w - half, 1), pltpu.roll(x, half, 1))
    return x * cosw + swapped * sinw


def _attn_body(sink_ref, q_ref, kc_ref, kp_ref, vc_ref, vp_ref, o_ref, *, q_per_kv):
    n = pl.program_id(1)
    blk = ATTN_BLOCK
    q = q_ref[...]
    k = jnp.concatenate([kp_ref[...], kc_ref[...]], axis=0).astype(F32)
    v = jnp.concatenate([vp_ref[...], vc_ref[...]], axis=0).astype(F32)

    from_prev = (lax.broadcasted_iota(jnp.int32, (blk, blk), 1)
                 > lax.broadcasted_iota(jnp.int32, (blk, blk), 0))
    no_prev = jnp.where(n > 0, 0.0, -jnp.inf).astype(F32)

    low = lax.broadcasted_iota(jnp.int32, (2 * blk, LANES), 1) < HEAD_DIM

    def split(tile, head_in_low):
        moved = pltpu.roll(tile, HEAD_DIM, 1)
        in_lo, in_hi = (tile, moved) if head_in_low else (moved, tile)
        return (jnp.where(low, in_lo, 0.0).astype(BF16),
                jnp.where(low, 0.0, in_hi).astype(BF16))

    for h in range(N_KV_HEADS):
        tile = slice((h // 2) * LANES, (h // 2 + 1) * LANES)
        k_lo, k_hi = split(k[:, tile], h % 2 == 0)
        v_lo, v_hi = split(v[:, tile], h % 2 == 0)
        for i in range(q_per_kv // 2):
            head = h * q_per_kv + 2 * i
            qt = q[:, (head // 2) * LANES:(head // 2 + 1) * LANES]
            acc = None
            for kk, vv, a in ((k_lo, v_lo, head), (k_hi, v_hi, head + 1)):
                s = lax.dot_general(qt, kk, (((1,), (1,)), ((), ())),
                                    preferred_element_type=F32)
                z = jnp.where(from_prev, s[:, :blk] + no_prev, s[:, blk:])
                sink = sink_ref[a] * LOG2E
                m = jnp.maximum(jnp.max(z, axis=-1, keepdims=True), sink)
                e = jnp.exp2(z - m)
                denom = jnp.sum(e, axis=-1, keepdims=True) + jnp.exp2(sink - m)
                p = jnp.concatenate([jnp.where(from_prev, e, 0.0), jnp.where(from_prev, 0.0, e)],
                                    axis=1).astype(BF16)
                o = jnp.dot(p, vv, preferred_element_type=F32) * (1.0 / denom)
                acc = o if acc is None else acc + o
            o_ref[:, (head // 2) * LANES:(head // 2 + 1) * LANES] = acc.astype(o_ref.dtype)


def _qkv_rope_body(x_ref, w_ref, b_ref, pos_ref, invf_ref, o_ref, cos_ref, sin_ref,
                   *, q_cols, rope_cols):
    j = pl.program_id(1)
    lane = lax.broadcasted_iota(jnp.int32, cos_ref.shape, 1)
    first_half = (lane % HEAD_DIM) < (HEAD_DIM // 2)

    @pl.when(j == 0)
    def _():
        ang = pos_ref[...].astype(F32) * invf_ref[...]
        sin = jnp.sin(ang)
        cos_ref[...] = jnp.cos(ang)
        sin_ref[...] = jnp.where(first_half, -sin, sin)

    w = w_ref[...].astype(BF16)
    bn = o_ref.shape[1]
    col = j * bn + lax.broadcasted_iota(jnp.int32, (1, bn), 1)
    scale = jnp.where(col < q_cols, HEAD_DIM ** -0.5 * LOG2E, 1.0)
    rotated = col < rope_cols
    chunk = x_ref.shape[0] // ROPE_ROW_CHUNKS
    lane_c = lax.broadcasted_iota(jnp.int32, (chunk, LANES), 1)
    first_half_c = (lane_c % HEAD_DIM) < (HEAD_DIM // 2)
    for r in range(ROPE_ROW_CHUNKS):
        rows = slice(r * chunk, (r + 1) * chunk)
        acc = jnp.dot(x_ref[rows, :], w, preferred_element_type=F32) + b_ref[...]
        rot = _rope(acc, cos_ref[rows, :], sin_ref[rows, :], first_half_c) * scale
        o_ref[rows, :] = jnp.where(rotated, rot, acc).astype(o_ref.dtype)


def qkv_rope_proj(x, w, b, positions, *, layer, bm, bn):
    m, k = x.shape
    n = w.shape[2]
    kvw = N_KV_HEADS * HEAD_DIM
    qw = n - 2 * kvw
    half = HEAD_DIM // 2
    inv_freq = jnp.power(ROPE_THETA, -jnp.arange(half, dtype=F32) / half)
    invf = jnp.tile(inv_freq, LANES // half).reshape(1, LANES)
    pos = positions.reshape(m, 1).astype(jnp.int32)
    body = functools.partial(_qkv_rope_body, q_cols=qw, rope_cols=qw + kvw)
    return pl.pallas_call(
        body,
        grid=(m // bm, n // bn),
        in_specs=[pl.BlockSpec((bm, k), lambda i, j: (i, 0)),
                  _layer_spec(k, bn, layer),
                  _layer_spec(1, bn, layer),
                  pl.BlockSpec((bm, 1), lambda i, j: (i, 0)),
                  pl.BlockSpec((1, LANES), lambda i, j: (0, 0))],
        out_specs=pl.BlockSpec((bm, bn), lambda i, j: (i, j)),
        out_shape=jax.ShapeDtypeStruct((m, n), BF16),
        scratch_shapes=[pltpu.VMEM((bm, LANES), F32), pltpu.VMEM((bm, LANES), F32)],
        compiler_params=_params(2),
        name="qkv_rope_proj",
    )(x, w, b.reshape(b.shape[0], 1, n), pos, invf)


def swa_attention(qkv, sinks, *, batch, seq):
    t, width = qkv.shape
    kvw = N_KV_HEADS * HEAD_DIM
    qw = width - 2 * kvw
    n_q_heads = qw // HEAD_DIM
    q_per_kv = n_q_heads // N_KV_HEADS
    assert q_per_kv % 2 == 0 and 2 * HEAD_DIM == LANES
    blk = ATTN_BLOCK
    nblk = seq // blk
    cur = lambda b, n: (b * nblk + n, 0)
    kcol = qw // kvw
    body = functools.partial(_attn_body, q_per_kv=q_per_kv)
    return pl.pallas_call(
        body,
        grid=(batch, nblk),
        in_specs=[pl.BlockSpec(memory_space=pltpu.SMEM),
                  pl.BlockSpec((blk, qw), cur),
                  pl.BlockSpec((blk, kvw), lambda b, n: (b * nblk + n, kcol)),
                  pl.BlockSpec((blk, kvw), lambda b, n: (b * nblk + jnp.maximum(n - 1, 0), kcol)),
                  pl.BlockSpec((blk, kvw), lambda b, n: (b * nblk + n, kcol + 1)),
                  pl.BlockSpec((blk, kvw), lambda b, n: (b * nblk + jnp.maximum(n - 1, 0), kcol + 1))],
        out_specs=pl.BlockSpec((blk, qw), cur),
        out_shape=jax.ShapeDtypeStruct((t, qw), BF16),
        compiler_params=_params(2),
        name="swa_attention",
    )(sinks.astype(F32), qkv, qkv, qkv, qkv, qkv)


def _ffn(hn, w_gate, w_up, w_down, layer, casts=None):
    act, w_down_bf16 = swiglu_up(hn, w_gate, w_up, w_down, layer=layer, bm=2048, bn=256)
    if casts is None:
        return matmul(act, w_down_bf16, layer=0, bm=512, bn=512, name="ffn_down")
    return matmul_cast2(act, w_down_bf16, *casts, layer=0, bm=512, bn=512, name="ffn_down")


def kernel(x, positions, norm_pre_mix, norm_post_mix, norm_pre_ffn, norm_post_ffn, s5_lam_re, s5_lam_im, s5_log_step, s5_b_re, s5_b_im, s5_c_re, s5_c_im, s5_d, s5_w_out1, s5_b_out1, s5_w_out2, s5_b_out2, attn_w_qkv, attn_b_qkv, attn_w_o, attn_b_o, attn_sinks, ffn_w_gate, ffn_w_up, ffn_w_down):
    bsz, seq, d = x.shape
    t = bsz * seq
    h = x.reshape(t, d)

    hn = rmsnorm(h, norm_pre_mix[0], F32)
    tables = _s5_tables(s5_lam_re[0], s5_lam_im[0], s5_log_step[0], s5_b_re[0], s5_b_im[0],
                        s5_c_re[0], s5_c_im[0], S5_CHUNK)
    g = s5_scan(hn.reshape(bsz, seq, d), tables, s5_d[0], tc=2048).reshape(t, d)
    m = glu_matmul(g, s5_w_out1, s5_b_out1, s5_w_out2, s5_b_out2, layer=0, bm=1024, bn=512)
    hn = mix_norm(h, m, norm_post_mix[0], norm_pre_ffn[0])
    f, w_qkv, w_o = _ffn(hn, ffn_w_gate, ffn_w_up, ffn_w_down, 0, casts=(attn_w_qkv, attn_w_o))
    h, hn = block_resid_norm(h, m, f, norm_post_mix[0], norm_post_ffn[0], norm_pre_mix[1])

    qkv = qkv_rope_proj(hn, w_qkv, attn_b_qkv, positions, layer=0, bm=1024, bn=1024)
    att = swa_attention(qkv, attn_sinks[0], batch=bsz, seq=seq)
    m = matmul(att, w_o, attn_b_o, layer=0, bm=1024, bn=1024, name="o_proj")
    hn = mix_norm(h, m, norm_post_mix[1], norm_pre_ffn[1])
    f = _ffn(hn, ffn_w_gate, ffn_w_up, ffn_w_down, 1)
    h = block_resid(h, m, f, norm_post_mix[1], norm_post_ffn[1])
    return h.reshape(bsz, seq, d)
```

```python
import functools
import math

import jax
import jax.numpy as jnp
from jax import lax
from jax.experimental import pallas as pl
from jax.experimental.pallas import tpu as pltpu

F32 = jnp.float32
BF16 = jnp.bfloat16

EPS = 1e-6
LANES = 128
S5_GROUP = 16
GROUPS_PER_TILE = LANES // S5_GROUP
S5_CHUNK = 8
HEAD_DIM = 64
N_KV_HEADS = 8
ATTN_BLOCK = 128
ROPE_THETA = 10000.0
ROPE_ROW_CHUNKS = 8
LOG2E = math.log2(math.e)
VMEM_LIMIT = 60 * 1024 * 1024


def _params(n_axes, vmem=VMEM_LIMIT):
    return pltpu.CompilerParams(
        dimension_semantics=("arbitrary",) * n_axes, vmem_limit_bytes=vmem)


def _rms(x, g):
    r = lax.rsqrt(jnp.mean(x * x, axis=-1, keepdims=True) + EPS)
    return x * r * g


def _rmsnorm_body(x_ref, g_ref, o_ref):
    o_ref[...] = _rms(x_ref[...], g_ref[...]).astype(o_ref.dtype)


def _row_call(body, n_rows_in, n_vec_in, out_dtypes, t, d, name):
    streams = n_rows_in + len(out_dtypes)
    rows = 512 if streams <= 3 else 256
    row_spec = pl.BlockSpec((rows, d), lambda i: (i, 0))
    vec_spec = pl.BlockSpec((1, d), lambda i: (0, 0))
    return pl.pallas_call(
        body,
        grid=(t // rows,),
        in_specs=[row_spec] * n_rows_in + [vec_spec] * n_vec_in,
        out_specs=[row_spec] * len(out_dtypes),
        out_shape=[jax.ShapeDtypeStruct((t, d), dt) for dt in out_dtypes],
        compiler_params=_params(1),
        name=name,
    )


def rmsnorm(x, g, out_dtype):
    t, d = x.shape
    return _row_call(_rmsnorm_body, 1, 1, [out_dtype], t, d, "rmsnorm")(x, g.reshape(1, d))[0]


def _mix_norm_body(h_ref, m_ref, gm_ref, gpre_ref, hn_ref):
    h = h_ref[...] + _rms(m_ref[...].astype(F32), gm_ref[...])
    hn_ref[...] = _rms(h, gpre_ref[...]).astype(hn_ref.dtype)


def mix_norm(h, m, g_m, g_pre):
    t, d = h.shape
    return _row_call(_mix_norm_body, 2, 2, [BF16], t, d, "mix_norm")(
        h, m, g_m.reshape(1, d), g_pre.reshape(1, d))[0]


def _block_resid_norm_body(h_ref, m_ref, f_ref, gm_ref, gf_ref, gpre_ref, hout_ref, hn_ref):
    h = h_ref[...] + _rms(m_ref[...].astype(F32), gm_ref[...])
    h = h + _rms(f_ref[...].astype(F32), gf_ref[...])
    hout_ref[...] = h
    hn_ref[...] = _rms(h, gpre_ref[...]).astype(hn_ref.dtype)


def block_resid_norm(h, m, f, g_m, g_f, g_pre):
    t, d = h.shape
    return _row_call(_block_resid_norm_body, 3, 3, [F32, BF16], t, d, "block_resid_norm")(
        h, m, f, g_m.reshape(1, d), g_f.reshape(1, d), g_pre.reshape(1, d))


def _block_resid_body(h_ref, m_ref, f_ref, gm_ref, gf_ref, hout_ref):
    h = h_ref[...] + _rms(m_ref[...].astype(F32), gm_ref[...])
    hout_ref[...] = h + _rms(f_ref[...].astype(F32), gf_ref[...])


def block_resid(h, m, f, g_m, g_f):
    t, d = h.shape
    return _row_call(_block_resid_body, 3, 2, [F32], t, d, "block_resid")(
        h, m, f, g_m.reshape(1, d), g_f.reshape(1, d))[0]


def _mm_bias_body(x_ref, w_ref, b_ref, o_ref):
    acc = jnp.dot(x_ref[...], w_ref[...].astype(BF16), preferred_element_type=F32)
    o_ref[...] = (acc + b_ref[...]).astype(o_ref.dtype)


def _mm_body(x_ref, w_ref, o_ref):
    o_ref[...] = jnp.dot(x_ref[...], w_ref[...].astype(BF16),
                         preferred_element_type=F32).astype(o_ref.dtype)


def _layer_spec(rows, bn, layer):
    return pl.BlockSpec((None, rows, bn), lambda i, j: (layer, 0, j))


def matmul(x, w, b=None, *, layer, bm, bn, out_dtype=BF16, name="matmul"):
    m, k = x.shape
    n = w.shape[2]
    in_specs = [pl.BlockSpec((bm, k), lambda i, j: (i, 0)), _layer_spec(k, bn, layer)]
    args = [x, w]
    body = _mm_body
    if b is not None:
        in_specs.append(_layer_spec(1, bn, layer))
        args.append(b.reshape(b.shape[0], 1, n))
        body = _mm_bias_body
    return pl.pallas_call(
        body,
        grid=(m // bm, n // bn),
        in_specs=in_specs,
        out_specs=pl.BlockSpec((bm, bn), lambda i, j: (i, j)),
        out_shape=jax.ShapeDtypeStruct((m, n), out_dtype),
        compiler_params=_params(2),
        name=name,
    )(*args)


def _cast_slabs(w, layer, steps, step_of):
    layers, rows, cols = w.shape
    slab = rows // steps
    assert slab * steps == rows and slab % 16 == 0
    in_spec = pl.BlockSpec((None, None, slab, cols), lambda i, j: (layer, step_of(i, j), 0, 0))
    out_spec = pl.BlockSpec((None, slab, cols), lambda i, j: (step_of(i, j), 0, 0))
    out_shape = jax.ShapeDtypeStruct((steps, slab, cols), BF16)
    return w.reshape(layers, steps, slab, cols), in_spec, out_spec, out_shape


def _mm_cast2_body(x_ref, w_ref, ca_ref, cb_ref, o_ref, ca_out_ref, cb_out_ref):
    _mm_body(x_ref, w_ref, o_ref)
    ca_out_ref[...] = ca_ref[...].astype(ca_out_ref.dtype)
    cb_out_ref[...] = cb_ref[...].astype(cb_out_ref.dtype)


def matmul_cast2(x, w, cast_a, cast_b, *, layer, bm, bn, name):
    m, k = x.shape
    n = w.shape[2]
    steps_j = n // bn
    steps = (m // bm) * steps_j
    step_of = lambda i, j: i * steps_j + j
    ca, ca_in, ca_out, ca_shape = _cast_slabs(cast_a, 0, steps, step_of)
    cb, cb_in, cb_out, cb_shape = _cast_slabs(cast_b, 0, steps, step_of)
    out, ca16, cb16 = pl.pallas_call(
        _mm_cast2_body,
        grid=(m // bm, steps_j),
        in_specs=[pl.BlockSpec((bm, k), lambda i, j: (i, 0)), _layer_spec(k, bn, layer),
                  ca_in, cb_in],
        out_specs=[pl.BlockSpec((bm, bn), lambda i, j: (i, j)), ca_out, cb_out],
        out_shape=[jax.ShapeDtypeStruct((m, n), BF16), ca_shape, cb_shape],
        compiler_params=_params(2),
        name=name,
    )(x, w, ca, cb)
    return out, ca16.reshape((1,) + cast_a.shape[1:]), cb16.reshape((1,) + cast_b.shape[1:])


def _glu_body(x_ref, w1_ref, b1_ref, w2_ref, b2_ref, o_ref):
    x = x_ref[...]
    a = jnp.dot(x, w1_ref[...].astype(BF16), preferred_element_type=F32) + b1_ref[...]
    b = jnp.dot(x, w2_ref[...].astype(BF16), preferred_element_type=F32) + b2_ref[...]
    o_ref[...] = (a * jax.nn.sigmoid(b)).astype(o_ref.dtype)


def glu_matmul(x, w1, b1, w2, b2, *, layer, bm, bn):
    m, k = x.shape
    n = w1.shape[2]
    w_spec = _layer_spec(k, bn, layer)
    b_spec = _layer_spec(1, bn, layer)
    return pl.pallas_call(
        _glu_body,
        grid=(m // bm, n // bn),
        in_specs=[pl.BlockSpec((bm, k), lambda i, j: (i, 0)),
                  w_spec, b_spec, w_spec, b_spec],
        out_specs=pl.BlockSpec((bm, bn), lambda i, j: (i, j)),
        out_shape=jax.ShapeDtypeStruct((m, n), BF16),
        compiler_params=_params(2),
        name="glu_matmul",
    )(x, w1, b1.reshape(b1.shape[0], 1, n), w2, b2.reshape(b2.shape[0], 1, n))


def _swiglu_up_body(x_ref, wg_ref, wu_ref, wd_ref, o_ref, wd_out_ref):
    x = x_ref[...]
    a = jnp.dot(x, wg_ref[...].astype(BF16), preferred_element_type=F32)
    b = jnp.dot(x, wu_ref[...].astype(BF16), preferred_element_type=F32)
    o_ref[...] = (jax.nn.silu(a) * b).astype(o_ref.dtype)
    wd_out_ref[...] = wd_ref[...].astype(wd_out_ref.dtype)


def swiglu_up(x, wg, wu, wd, *, layer, bm, bn):
    m, k = x.shape
    n = wg.shape[2]
    steps_j = n // bn
    wd4, wd_in, wd_out, wd_shape = _cast_slabs(
        wd, layer, (m // bm) * steps_j, lambda i, j: i * steps_j + j)
    w_spec = _layer_spec(k, bn, layer)
    x_spec = pl.BlockSpec((bm, k), lambda i, j: (i, 0), pipeline_mode=pl.Buffered(1))
    act, wd_bf16 = pl.pallas_call(
        _swiglu_up_body,
        grid=(m // bm, steps_j),
        in_specs=[x_spec, w_spec, w_spec, wd_in],
        out_specs=[pl.BlockSpec((bm, bn), lambda i, j: (i, j)), wd_out],
        out_shape=[jax.ShapeDtypeStruct((m, n), BF16), wd_shape],
        compiler_params=_params(2),
        name="swiglu_up",
    )(x, wg, wu, wd4)
    return act, wd_bf16.reshape((1,) + wd.shape[1:])


def _s5_tables(lam_re, lam_im, log_step, b_re, b_im, c_re, c_im, lc):
    g, p = lam_re.shape
    ch = b_re.shape[-1]
    assert ch == S5_GROUP and 2 * p == LANES
    ng = GROUPS_PER_TILE
    nt = g // ng
    lr, li = lam_re.astype(F32), lam_im.astype(F32)
    step = jnp.exp(log_step.astype(F32))[:, None]
    zr, zi = lr * step, li * step
    taus = jnp.arange(lc + 1, dtype=F32)[:, None, None]
    mag = jnp.exp(zr[None] * taus)
    pr = mag * jnp.cos(zi[None] * taus)
    pi = mag * jnp.sin(zi[None] * taus)
    nr, ni = pr[1] - 1.0, pi[1]
    den = lr * lr + li * li
    qr = (nr * lr + ni * li) / den
    qi = (ni * lr - nr * li) / den
    b_r = jnp.swapaxes(b_re.astype(F32), 1, 2)
    b_i = jnp.swapaxes(b_im.astype(F32), 1, 2)
    br = qr[:, None, :] * b_r - qi[:, None, :] * b_i
    bi = qr[:, None, :] * b_i + qi[:, None, :] * b_r
    cr, ci = c_re.astype(F32), c_im.astype(F32)

    zero = jnp.zeros_like(pr)
    even = (jnp.arange(g) % 2 == 0)[None, :, None]
    halves = lambda a: jnp.concatenate(
        [jnp.where(even, a, zero), jnp.where(even, zero, a)], axis=-1)
    per_tile = lambda a: jnp.swapaxes(a.reshape(lc + 1, nt, ng, a.shape[-1]), 0, 1)
    grp = lambda a: a.reshape((nt, ng) + a.shape[1:])
    cat = lambda *xs: jnp.concatenate(xs, axis=-1)
    return (per_tile(halves(pr)), per_tile(halves(pi)),
            grp(cat(br, br, bi, bi)), grp(cat(cr, cr, ci, ci)),
            pr[lc].reshape(nt, 1, ng * p), pi[lc].reshape(nt, 1, ng * p))


def _split_bf16(x):
    hi = x.astype(BF16)
    return hi, (x - hi.astype(F32)).astype(BF16)


def _dot_nt_3pass(a, b):
    nt = lambda u, v: lax.dot_general(u, v, (((1,), (1,)), ((), ())), preferred_element_type=F32)
    a_hi, a_lo = _split_bf16(a)
    b_hi, b_lo = _split_bf16(b)
    return nt(a_hi, b_hi) + nt(a_hi, b_lo) + nt(a_lo, b_hi)


def _s5_build(prh_ref, pih_ref, bd_ref, cd_ref, m_ref, ws_ref, wyt_ref, wf_ref, cp_ref, lc, nk):
    @pl.when(pl.program_id(0) == 0)
    def _():
        ws_ref[...] = jnp.zeros_like(ws_ref)
        wyt_ref[...] = jnp.zeros_like(wyt_ref)
        wf_ref[...] = jnp.zeros_like(wf_ref)
        cp_ref[...] = jnp.zeros_like(cp_ref)
        m_ref[...] = jnp.zeros_like(m_ref)

    for a in range(GROUPS_PER_TILE):
        bd, cd = bd_ref[0, a], cd_ref[0, a]
        b_r, b_i = bd[:, :LANES], bd[:, LANES:]
        c_r, c_i = cd[:, :LANES], cd[:, LANES:]
        re_tile = slice((a // 2) * LANES, (a // 2 + 1) * LANES)
        im_tile = slice((nk + a // 2) * LANES, (nk + a // 2 + 1) * LANES)
        grp = slice(a * S5_GROUP, (a + 1) * S5_GROUP)
        p_r, p_i = prh_ref[0, 0, a:a + 1, :], pih_ref[0, 0, a:a + 1, :]
        cp_ref[grp, re_tile] = p_r * c_r - p_i * c_i
        cp_ref[grp, im_tile] = -(p_r * c_i + p_i * c_r)
        for s in range(lc):
            rows = slice(s * LANES + a * S5_GROUP, s * LANES + (a + 1) * S5_GROUP)
            p_r, p_i = prh_ref[0, lc - 1 - s, a:a + 1, :], pih_ref[0, lc - 1 - s, a:a + 1, :]
            w_re, w_im = p_r * b_r - p_i * b_i, p_r * b_i + p_i * b_r
            wf_ref[rows, re_tile] = w_re
            wf_ref[rows, im_tile] = w_im
            ws_ref[rows, re_tile] = w_re.astype(BF16)
            ws_ref[rows, im_tile] = w_im.astype(BF16)
            p_r, p_i = prh_ref[0, s + 1, a:a + 1, :], pih_ref[0, s + 1, a:a + 1, :]
            wyt_ref[rows, re_tile] = (p_r * c_r - p_i * c_i).astype(BF16)
            wyt_ref[rows, im_tile] = (-(p_r * c_i + p_i * c_r)).astype(BF16)
    k_all = _dot_nt_3pass(wf_ref[...], cp_ref[...]).astype(BF16)
    for tau in range(lc):
        kb = k_all[(lc - 1 - tau) * LANES:(lc - tau) * LANES, :]
        for s in range(lc - tau):
            t = s + tau
            m_ref[s * LANES:(s + 1) * LANES, t * LANES:(t + 1) * LANES] = kb


def _s5_body(x_ref, prh_ref, pih_ref, bd_ref, cd_ref, are_ref, aim_ref, d_ref, o_ref,
             m_ref, ws_ref, wyt_ref, wf_ref, cp_ref, lhs_ref, s_ref, xp_ref, stage_ref, carry_ref,
             *, lc, nb, nc, ns):
    nk = ns // LANES

    @pl.when(pl.program_id(1) == 0)
    def _():
        _s5_build(prh_ref, pih_ref, bd_ref, cd_ref, m_ref, ws_ref, wyt_ref, wf_ref, cp_ref, lc, nk)
        carry_ref[...] = jnp.zeros_like(carry_ref)

    for b in range(nb):
        for s in range(lc):
            piece = x_ref[b, pl.ds(s, nc, stride=lc), :]
            lhs_ref[b * nc:(b + 1) * nc, s * LANES:(s + 1) * LANES] = piece.astype(BF16)
    lhs = lhs_ref[...]
    s_all = jnp.dot(lhs, ws_ref[...], preferred_element_type=F32)
    for k in range(2 * nk):
        for b in range(nb):
            s_ref[k, pl.ds(b, nc, stride=nb), :] = (
                s_all[b * nc:(b + 1) * nc, k * LANES:(k + 1) * LANES])

    tile_rows = lambda ref, k: jnp.broadcast_to(ref[0, :, k * LANES:(k + 1) * LANES], (nb, LANES))
    ar = [tile_rows(are_ref, k) for k in range(nk)]
    ai = [tile_rows(aim_ref, k) for k in range(nk)]

    per = 8 // nb

    def step(i, carry):
        r = pl.ds(pl.multiple_of(i * 8, 8), 8)
        s_in = [s_ref[k, r, :] for k in range(2 * nk)]
        before = [[] for _ in range(2 * nk)]
        for u in range(per):
            new = []
            for k in range(nk):
                xr, xi = carry[2 * k], carry[2 * k + 1]
                before[k].append(xr)
                before[nk + k].append(xi)
                rows = slice(u * nb, (u + 1) * nb)
                new.append(ar[k] * xr - ai[k] * xi + s_in[k][rows])
                new.append(ar[k] * xi + ai[k] * xr + s_in[nk + k][rows])
            carry = tuple(new)
        for k in range(2 * nk):
            xp_ref[k, r, :] = jnp.concatenate(before[k], axis=0)
        return carry

    carry = lax.fori_loop(0, nc // per, step,
                          tuple(carry_ref[k, 0:nb, :] for k in range(2 * nk)))
    for k in range(2 * nk):
        carry_ref[k, 0:nb, :] = carry[k]

    xp = jnp.concatenate(
        [jnp.concatenate([xp_ref[k, pl.ds(b, nc, stride=nb), :] for b in range(nb)], axis=0)
         for k in range(2 * nk)], axis=-1)
    tw = 2 * LANES
    y_intra = jnp.concatenate(
        [jnp.dot(lhs[:, :t0 + tw], m_ref[:t0 + tw, t0:t0 + tw], preferred_element_type=F32)
         for t0 in range(0, lc * LANES, tw)], axis=-1)
    y = y_intra + lax.dot_general(xp.astype(BF16), wyt_ref[...], (((1,), (1,)), ((), ())),
                                  preferred_element_type=F32)
    for b in range(nb):
        for t in range(lc):
            stage_ref[b, pl.ds(t, nc, stride=lc), :] = (
                y[b * nc:(b + 1) * nc, t * LANES:(t + 1) * LANES])
    yy = stage_ref[...] + d_ref[...] * x_ref[...]
    o_ref[...] = jax.nn.gelu(yy).astype(o_ref.dtype)


def s5_scan(hn, tables, d_skip, *, tc, lc=S5_CHUNK):
    prh, pih, bd, cd, a_re, a_im = tables
    ng = GROUPS_PER_TILE
    nb, l, d = hn.shape
    nt = d // LANES
    nc = tc // lc
    ns = a_re.shape[-1]
    rows = nb * nc
    body = functools.partial(_s5_body, lc=lc, nb=nb, nc=nc, ns=ns)
    tile4 = lambda j, i: (j, 0, 0, 0)
    tile3 = lambda j, i: (j, 0, 0)
    assert 8 % nb == 0
    return pl.pallas_call(
        body,
        grid=(nt, l // tc),
        in_specs=[pl.BlockSpec((nb, tc, LANES), lambda j, i: (0, i, j)),
                  pl.BlockSpec((1, lc + 1, ng, LANES), tile4),
                  pl.BlockSpec((1, lc + 1, ng, LANES), tile4),
                  pl.BlockSpec((1, ng, S5_GROUP, 2 * LANES), tile4),
                  pl.BlockSpec((1, ng, S5_GROUP, 2 * LANES), tile4),
                  pl.BlockSpec((1, 1, ns), tile3),
                  pl.BlockSpec((1, 1, ns), tile3),
                  pl.BlockSpec((1, 1, LANES), lambda j, i: (0, 0, j))],
        out_specs=pl.BlockSpec((nb, tc, LANES), lambda j, i: (0, i, j)),
        out_shape=jax.ShapeDtypeStruct((nb, l, d), BF16),
        scratch_shapes=[pltpu.VMEM((lc * LANES, lc * LANES), BF16),
                        pltpu.VMEM((lc * LANES, 2 * ns), BF16),
                        pltpu.VMEM((lc * LANES, 2 * ns), BF16),
                        pltpu.VMEM((lc * LANES, 2 * ns), F32),
                        pltpu.VMEM((LANES, 2 * ns), F32),
                        pltpu.VMEM((rows, lc * LANES), BF16),
                        pltpu.VMEM((2 * ns // LANES, rows, LANES), F32),
                        pltpu.VMEM((2 * ns // LANES, rows, LANES), F32),
                        pltpu.VMEM((nb, tc, LANES), F32),
                        pltpu.VMEM((2 * ns // LANES, 8, LANES), F32)],
        compiler_params=_params(2),
        name="s5_scan",
    )(hn, prh, pih, bd, cd, a_re, a_im, d_skip.reshape(1, 1, d).astype(F32))


def _rope(x, cos, sin_signed, first_half):
    w = x.shape[1]
    reps = w // LANES
    cosw = jnp.tile(cos, (1, reps))
    sinw = jnp.tile(sin_signed, (1, reps))
    sel = jnp.tile(first_half, (1, reps))
    half = HEAD_DIM // 2
    swapped = jnp.where(sel, pltpu.roll(x, w - half, 1), pltpu.roll(x, half, 1))
    return x * cosw + swapped * sinw


def _attn_body(sink_ref, q_ref, kc_ref, kp_ref, vc_ref, vp_ref, o_ref, *, q_per_kv):
    n = pl.program_id(1)
    blk = ATTN_BLOCK
    q = q_ref[...]
    k = jnp.concatenate([kp_ref[...], kc_ref[...]], axis=0).astype(F32)
    v = jnp.concatenate([vp_ref[...], vc_ref[...]], axis=0).astype(F32)

    from_prev = (lax.broadcasted_iota(jnp.int32, (blk, blk), 1)
                 > lax.broadcasted_iota(jnp.int32, (blk, blk), 0))
    no_prev = jnp.where(n > 0, 0.0, -jnp.inf).astype(F32)

    low = lax.broadcasted_iota(jnp.int32, (2 * blk, LANES), 1) < HEAD_DIM

    def split(tile, head_in_low):
        moved = pltpu.roll(tile, HEAD_DIM, 1)
        in_lo, in_hi = (tile, moved) if head_in_low else (moved, tile)
        return (jnp.where(low, in_lo, 0.0).astype(BF16),
                jnp.where(low, 0.0, in_hi).astype(BF16))

    for h in range(N_KV_HEADS):
        tile = slice((h // 2) * LANES, (h // 2 + 1) * LANES)
        k_lo, k_hi = split(k[:, tile], h % 2 == 0)
        v_lo, v_hi = split(v[:, tile], h % 2 == 0)
        for i in range(q_per_kv // 2):
            head = h * q_per_kv + 2 * i
            qt = q[:, (head // 2) * LANES:(head // 2 + 1) * LANES]
            acc = None
            for kk, vv, a in ((k_lo, v_lo, head), (k_hi, v_hi, head + 1)):
                s = lax.dot_general(qt, kk, (((1,), (1,)), ((), ())),
                                    preferred_element_type=F32)
                z = jnp.where(from_prev, s[:, :blk] + no_prev, s[:, blk:])
                sink = sink_ref[a] * LOG2E
                m = jnp.maximum(jnp.max(z, axis=-1, keepdims=True), sink)
                e = jnp.exp2(z - m)
                denom = jnp.sum(e, axis=-1, keepdims=True) + jnp.exp2(sink - m)
                p = jnp.concatenate([jnp.where(from_prev, e, 0.0), jnp.where(from_prev, 0.0, e)],
                                    axis=1).astype(BF16)
                o = jnp.dot(p, vv, preferred_element_type=F32) * (1.0 / denom)
                acc = o if acc is None else acc + o
            o_ref[:, (head // 2) * LANES:(head // 2 + 1) * LANES] = acc.astype(o_ref.dtype)


def _qkv_rope_body(x_ref, w_ref, b_ref, pos_ref, invf_ref, o_ref, cos_ref, sin_ref,
                   *, q_cols, rope_cols):
    j = pl.program_id(1)
    lane = lax.broadcasted_iota(jnp.int32, cos_ref.shape, 1)
    first_half = (lane % HEAD_DIM) < (HEAD_DIM // 2)

    @pl.when(j == 0)
    def _():
        ang = pos_ref[...].astype(F32) * invf_ref[...]
        sin = jnp.sin(ang)
        cos_ref[...] = jnp.cos(ang)
        sin_ref[...] = jnp.where(first_half, -sin, sin)

    w = w_ref[...].astype(BF16)
    bn = o_ref.shape[1]
    col = j * bn + lax.broadcasted_iota(jnp.int32, (1, bn), 1)
    scale = jnp.where(col < q_cols, HEAD_DIM ** -0.5 * LOG2E, 1.0)
    rotated = col < rope_cols
    chunk = x_ref.shape[0] // ROPE_ROW_CHUNKS
    lane_c = lax.broadcasted_iota(jnp.int32, (chunk, LANES), 1)
    first_half_c = (lane_c % HEAD_DIM) < (HEAD_DIM // 2)
    for r in range(ROPE_ROW_CHUNKS):
        rows = slice(r * chunk, (r + 1) * chunk)
        acc = jnp.dot(x_ref[rows, :], w, preferred_element_type=F32) + b_ref[...]
        rot = _rope(acc, cos_ref[rows, :], sin_ref[rows, :], first_half_c) * scale
        o_ref[rows, :] = jnp.where(rotated, rot, acc).astype(o_ref.dtype)


def qkv_rope_proj(x, w, b, positions, *, layer, bm, bn):
    m, k = x.shape
    n = w.shape[2]
    kvw = N_KV_HEADS * HEAD_DIM
    qw = n - 2 * kvw
    half = HEAD_DIM // 2
    inv_freq = jnp.power(ROPE_THETA, -jnp.arange(half, dtype=F32) / half)
    invf = jnp.tile(inv_freq, LANES // half).reshape(1, LANES)
    pos = positions.reshape(m, 1).astype(jnp.int32)
    body = functools.partial(_qkv_rope_body, q_cols=qw, rope_cols=qw + kvw)
    return pl.pallas_call(
        body,
        grid=(m // bm, n // bn),
        in_specs=[pl.BlockSpec((bm, k), lambda i, j: (i, 0)),
                  _layer_spec(k, bn, layer),
                  _layer_spec(1, bn, layer),
                  pl.BlockSpec((bm, 1), lambda i, j: (i, 0)),
                  pl.BlockSpec((1, LANES), lambda i, j: (0, 0))],
        out_specs=pl.BlockSpec((bm, bn), lambda i, j: (i, j)),
        out_shape=jax.ShapeDtypeStruct((m, n), BF16),
        scratch_shapes=[pltpu.VMEM((bm, LANES), F32), pltpu.VMEM((bm, LANES), F32)],
        compiler_params=_params(2),
        name="qkv_rope_proj",
    )(x, w, b.reshape(b.shape[0], 1, n), pos, invf)


def swa_attention(qkv, sinks, *, batch, seq):
    t, width = qkv.shape
    kvw = N_KV_HEADS * HEAD_DIM
    qw = width - 2 * kvw
    n_q_heads = qw // HEAD_DIM
    q_per_kv = n_q_heads // N_KV_HEADS
    assert q_per_kv % 2 == 0 and 2 * HEAD_DIM == LANES
    blk = ATTN_BLOCK
    nblk = seq // blk
    cur = lambda b, n: (b * nblk + n, 0)
    kcol = qw // kvw
    body = functools.partial(_attn_body, q_per_kv=q_per_kv)
    return pl.pallas_call(
        body,
        grid=(batch, nblk),
        in_specs=[pl.BlockSpec(memory_space=pltpu.SMEM),
                  pl.BlockSpec((blk, qw), cur),
                  pl.BlockSpec((blk, kvw), lambda b, n: (b * nblk + n, kcol)),
                  pl.BlockSpec((blk, kvw), lambda b, n: (b * nblk + jnp.maximum(n - 1, 0), kcol)),
                  pl.BlockSpec((blk, kvw), lambda b, n: (b * nblk + n, kcol + 1)),
                  pl.BlockSpec((blk, kvw), lambda b, n: (b * nblk + jnp.maximum(n - 1, 0), kcol + 1))],
        out_specs=pl.BlockSpec((blk, qw), cur),
        out_shape=jax.ShapeDtypeStruct((t, qw), BF16),
        compiler_params=_params(2),
        name="swa_attention",
    )(sinks.astype(F32), qkv, qkv, qkv, qkv, qkv)


def _ffn(hn, w_gate, w_up, w_down, layer, casts=None):
    act, w_down_bf16 = swiglu_up(hn, w_gate, w_up, w_down, layer=layer, bm=2048, bn=256)
    if casts is None:
        return matmul(act, w_down_bf16, layer=0, bm=512, bn=512, name="ffn_down")
    return matmul_cast2(act, w_down_bf16, *casts, layer=0, bm=512, bn=512, name="ffn_down")


def kernel(x, positions, norm_pre_mix, norm_post_mix, norm_pre_ffn, norm_post_ffn, s5_lam_re, s5_lam_im, s5_log_step, s5_b_re, s5_b_im, s5_c_re, s5_c_im, s5_d, s5_w_out1, s5_b_out1, s5_w_out2, s5_b_out2, attn_w_qkv, attn_b_qkv, attn_w_o, attn_b_o, attn_sinks, ffn_w_gate, ffn_w_up, ffn_w_down):
    bsz, seq, d = x.shape
    t = bsz * seq
    h = x.reshape(t, d)

    hn = rmsnorm(h, norm_pre_mix[0], F32)
    tables = _s5_tables(s5_lam_re[0], s5_lam_im[0], s5_log_step[0], s5_b_re[0], s5_b_im[0],
                        s5_c_re[0], s5_c_im[0], S5_CHUNK)
    g = s5_scan(hn.reshape(bsz, seq, d), tables, s5_d[0], tc=2048).reshape(t, d)
    m = glu_matmul(g, s5_w_out1, s5_b_out1, s5_w_out2, s5_b_out2, layer=0, bm=1024, bn=512)
    hn = mix_norm(h, m, norm_post_mix[0], norm_pre_ffn[0])
    f, w_qkv, w_o = _ffn(hn, ffn_w_gate, ffn_w_up, ffn_w_down, 0, casts=(attn_w_qkv, attn_w_o))
    h, hn = block_resid_norm(h, m, f, norm_post_mix[0], norm_post_ffn[0], norm_pre_mix[1])

    qkv = qkv_rope_proj(hn, w_qkv, attn_b_qkv, positions, layer=0, bm=1024, bn=1024)
    att = swa_attention(qkv, attn_sinks[0], batch=bsz, seq=seq)
    m = matmul(att, w_o, attn_b_o, layer=0, bm=1024, bn=1024, name="o_proj")
    hn = mix_norm(h, m, norm_post_mix[1], norm_pre_ffn[1])
    f = _ffn(hn, ffn_w_gate, ffn_w_up, ffn_w_down, 1)
    h = block_resid(h, m, f, norm_post_mix[1], norm_post_ffn[1])
    return h.reshape(bsz, seq, d)
```

```python
import functools
import math

import jax
import jax.numpy as jnp
from jax import lax
from jax.experimental import pallas as pl
from jax.experimental.pallas import tpu as pltpu

F32 = jnp.float32
BF16 = jnp.bfloat16

EPS = 1e-6
LANES = 128
S5_GROUP = 16
GROUPS_PER_TILE = LANES // S5_GROUP
S5_CHUNK = 8
HEAD_DIM = 64
N_KV_HEADS = 8
ATTN_BLOCK = 128
ROPE_THETA = 10000.0
ROPE_ROW_CHUNKS = 8
LOG2E = math.log2(math.e)
VMEM_LIMIT = 60 * 1024 * 1024


def _params(n_axes, vmem=VMEM_LIMIT):
    return pltpu.CompilerParams(
        dimension_semantics=("arbitrary",) * n_axes, vmem_limit_bytes=vmem)


def _rms(x, g):
    r = lax.rsqrt(jnp.mean(x * x, axis=-1, keepdims=True) + EPS)
    return x * r * g


def _rmsnorm_body(x_ref, g_ref, o_ref):
    o_ref[...] = _rms(x_ref[...], g_ref[...]).astype(o_ref.dtype)


def _row_call(body, n_rows_in, n_vec_in, out_dtypes, t, d, name):
    streams = n_rows_in + len(out_dtypes)
    rows = 512 if streams <= 3 else 256
    row_spec = pl.BlockSpec((rows, d), lambda i: (i, 0))
    vec_spec = pl.BlockSpec((1, d), lambda i: (0, 0))
    return pl.pallas_call(
        body,
        grid=(t // rows,),
        in_specs=[row_spec] * n_rows_in + [vec_spec] * n_vec_in,
        out_specs=[row_spec] * len(out_dtypes),
        out_shape=[jax.ShapeDtypeStruct((t, d), dt) for dt in out_dtypes],
        compiler_params=_params(1),
        name=name,
    )


def rmsnorm(x, g, out_dtype):
    t, d = x.shape
    return _row_call(_rmsnorm_body, 1, 1, [out_dtype], t, d, "rmsnorm")(x, g.reshape(1, d))[0]


def _mix_norm_body(h_ref, m_ref, gm_ref, gpre_ref, hn_ref):
    h = h_ref[...] + _rms(m_ref[...].astype(F32), gm_ref[...])
    hn_ref[...] = _rms(h, gpre_ref[...]).astype(hn_ref.dtype)


def mix_norm(h, m, g_m, g_pre):
    t, d = h.shape
    return _row_call(_mix_norm_body, 2, 2, [BF16], t, d, "mix_norm")(
        h, m, g_m.reshape(1, d), g_pre.reshape(1, d))[0]


def _block_resid_norm_body(h_ref, m_ref, f_ref, gm_ref, gf_ref, gpre_ref, hout_ref, hn_ref):
    h = h_ref[...] + _rms(m_ref[...].astype(F32), gm_ref[...])
    h = h + _rms(f_ref[...].astype(F32), gf_ref[...])
    hout_ref[...] = h
    hn_ref[...] = _rms(h, gpre_ref[...]).astype(hn_ref.dtype)


def block_resid_norm(h, m, f, g_m, g_f, g_pre):
    t, d = h.shape
    return _row_call(_block_resid_norm_body, 3, 3, [F32, BF16], t, d, "block_resid_norm")(
        h, m, f, g_m.reshape(1, d), g_f.reshape(1, d), g_pre.reshape(1, d))


def _block_resid_body(h_ref, m_ref, f_ref, gm_ref, gf_ref, hout_ref):
    h = h_ref[...] + _rms(m_ref[...].astype(F32), gm_ref[...])
    hout_ref[...] = h + _rms(f_ref[...].astype(F32), gf_ref[...])


def block_resid(h, m, f, g_m, g_f):
    t, d = h.shape
    return _row_call(_block_resid_body, 3, 2, [F32], t, d, "block_resid")(
        h, m, f, g_m.reshape(1, d), g_f.reshape(1, d))[0]


def _mm_bias_body(x_ref, w_ref, b_ref, o_ref):
    acc = jnp.dot(x_ref[...], w_ref[...].astype(BF16), preferred_element_type=F32)
    o_ref[...] = (acc + b_ref[...]).astype(o_ref.dtype)


def _mm_body(x_ref, w_ref, o_ref):
    o_ref[...] = jnp.dot(x_ref[...], w_ref[...].astype(BF16),
                         preferred_element_type=F32).astype(o_ref.dtype)


def _layer_spec(rows, bn, layer):
    return pl.BlockSpec((None, rows, bn), lambda i, j: (layer, 0, j))


def matmul(x, w, b=None, *, layer, bm, bn, out_dtype=BF16, name="matmul"):
    m, k = x.shape
    n = w.shape[2]
    in_specs = [pl.BlockSpec((bm, k), lambda i, j: (i, 0)), _layer_spec(k, bn, layer)]
    args = [x, w]
    body = _mm_body
    if b is not None:
        in_specs.append(_layer_spec(1, bn, layer))
        args.append(b.reshape(b.shape[0], 1, n))
        body = _mm_bias_body
    return pl.pallas_call(
        body,
        grid=(m // bm, n // bn),
        in_specs=in_specs,
        out_specs=pl.BlockSpec((bm, bn), lambda i, j: (i, j)),
        out_shape=jax.ShapeDtypeStruct((m, n), out_dtype),
        compiler_params=_params(2),
        name=name,
    )(*args)


def _cast_slabs(w, layer, steps, step_of):
    layers, rows, cols = w.shape
    slab = rows // steps
    assert slab * steps == rows and slab % 16 == 0
    in_spec = pl.BlockSpec((None, None, slab, cols), lambda i, j: (layer, step_of(i, j), 0, 0))
    out_spec = pl.BlockSpec((None, slab, cols), lambda i, j: (step_of(i, j), 0, 0))
    out_shape = jax.ShapeDtypeStruct((steps, slab, cols), BF16)
    return w.reshape(layers, steps, slab, cols), in_spec, out_spec, out_shape


def _glu_body(x_ref, w1_ref, b1_ref, w2_ref, b2_ref, o_ref):
    x = x_ref[...]
    a = jnp.dot(x, w1_ref[...].astype(BF16), preferred_element_type=F32) + b1_ref[...]
    b = jnp.dot(x, w2_ref[...].astype(BF16), preferred_element_type=F32) + b2_ref[...]
    o_ref[...] = (a * jax.nn.sigmoid(b)).astype(o_ref.dtype)


def glu_matmul(x, w1, b1, w2, b2, *, layer, bm, bn):
    m, k = x.shape
    n = w1.shape[2]
    w_spec = _layer_spec(k, bn, layer)
    b_spec = _layer_spec(1, bn, layer)
    return pl.pallas_call(
        _glu_body,
        grid=(m // bm, n // bn),
        in_specs=[pl.BlockSpec((bm, k), lambda i, j: (i, 0)),
                  w_spec, b_spec, w_spec, b_spec],
        out_specs=pl.BlockSpec((bm, bn), lambda i, j: (i, j)),
        out_shape=jax.ShapeDtypeStruct((m, n), BF16),
        compiler_params=_params(2),
        name="glu_matmul",
    )(x, w1, b1.reshape(b1.shape[0], 1, n), w2, b2.reshape(b2.shape[0], 1, n))


def _swiglu_up_body(x_ref, wg_ref, wu_ref, wd_ref, o_ref, wd_out_ref):
    x = x_ref[...]
    a = jnp.dot(x, wg_ref[...].astype(BF16), preferred_element_type=F32)
    b = jnp.dot(x, wu_ref[...].astype(BF16), preferred_element_type=F32)
    o_ref[...] = (jax.nn.silu(a) * b).astype(o_ref.dtype)
    wd_out_ref[...] = wd_ref[...].astype(wd_out_ref.dtype)


def swiglu_up(x, wg, wu, wd, *, layer, bm, bn):
    m, k = x.shape
    n = wg.shape[2]
    steps_j = n // bn
    wd4, wd_in, wd_out, wd_shape = _cast_slabs(
        wd, layer, (m // bm) * steps_j, lambda i, j: i * steps_j + j)
    w_spec = _layer_spec(k, bn, layer)
    x_spec = pl.BlockSpec((bm, k), lambda i, j: (i, 0), pipeline_mode=pl.Buffered(1))
    act, wd_bf16 = pl.pallas_call(
        _swiglu_up_body,
        grid=(m // bm, steps_j),
        in_specs=[x_spec, w_spec, w_spec, wd_in],
        out_specs=[pl.BlockSpec((bm, bn), lambda i, j: (i, j)), wd_out],
        out_shape=[jax.ShapeDtypeStruct((m, n), BF16), wd_shape],
        compiler_params=_params(2),
        name="swiglu_up",
    )(x, wg, wu, wd4)
    return act, wd_bf16.reshape((1,) + wd.shape[1:])


def _s5_tables(lam_re, lam_im, log_step, b_re, b_im, c_re, c_im, lc):
    g, p = lam_re.shape
    ch = b_re.shape[-1]
    assert ch == S5_GROUP and 2 * p == LANES
    ng = GROUPS_PER_TILE
    nt = g // ng
    lr, li = lam_re.astype(F32), lam_im.astype(F32)
    step = jnp.exp(log_step.astype(F32))[:, None]
    zr, zi = lr * step, li * step
    taus = jnp.arange(lc + 1, dtype=F32)[:, None, None]
    mag = jnp.exp(zr[None] * taus)
    pr = mag * jnp.cos(zi[None] * taus)
    pi = mag * jnp.sin(zi[None] * taus)
    nr, ni = pr[1] - 1.0, pi[1]
    den = lr * lr + li * li
    qr = (nr * lr + ni * li) / den
    qi = (ni * lr - nr * li) / den
    b_r = jnp.swapaxes(b_re.astype(F32), 1, 2)
    b_i = jnp.swapaxes(b_im.astype(F32), 1, 2)
    br = qr[:, None, :] * b_r - qi[:, None, :] * b_i
    bi = qr[:, None, :] * b_i + qi[:, None, :] * b_r
    cr, ci = c_re.astype(F32), c_im.astype(F32)

    zero = jnp.zeros_like(pr)
    even = (jnp.arange(g) % 2 == 0)[None, :, None]
    halves = lambda a: jnp.concatenate(
        [jnp.where(even, a, zero), jnp.where(even, zero, a)], axis=-1)
    per_tile = lambda a: jnp.swapaxes(a.reshape(lc + 1, nt, ng, a.shape[-1]), 0, 1)
    grp = lambda a: a.reshape((nt, ng) + a.shape[1:])
    cat = lambda *xs: jnp.concatenate(xs, axis=-1)
    return (per_tile(halves(pr)), per_tile(halves(pi)),
            grp(cat(br, br, bi, bi)), grp(cat(cr, cr, ci, ci)),
            pr[lc].reshape(nt, 1, ng * p), pi[lc].reshape(nt, 1, ng * p))


def _split_bf16(x):
    hi = x.astype(BF16)
    return hi, (x - hi.astype(F32)).astype(BF16)


def _dot_nt_3pass(a, b):
    nt = lambda u, v: lax.dot_general(u, v, (((1,), (1,)), ((), ())), preferred_element_type=F32)
    a_hi, a_lo = _split_bf16(a)
    b_hi, b_lo = _split_bf16(b)
    return nt(a_hi, b_hi) + nt(a_hi, b_lo) + nt(a_lo, b_hi)


def _s5_build(prh_ref, pih_ref, bd_ref, cd_ref, m_ref, ws_ref, wyt_ref, wf_ref, cp_ref, lc, nk):
    @pl.when(pl.program_id(0) == 0)
    def _():
        ws_ref[...] = jnp.zeros_like(ws_ref)
        wyt_ref[...] = jnp.zeros_like(wyt_ref)
        wf_ref[...] = jnp.zeros_like(wf_ref)
        cp_ref[...] = jnp.zeros_like(cp_ref)
        m_ref[...] = jnp.zeros_like(m_ref)

    for a in range(GROUPS_PER_TILE):
        bd, cd = bd_ref[0, a], cd_ref[0, a]
        b_r, b_i = bd[:, :LANES], bd[:, LANES:]
        c_r, c_i = cd[:, :LANES], cd[:, LANES:]
        re_tile = slice((a // 2) * LANES, (a // 2 + 1) * LANES)
        im_tile = slice((nk + a // 2) * LANES, (nk + a // 2 + 1) * LANES)
        grp = slice(a * S5_GROUP, (a + 1) * S5_GROUP)
        p_r, p_i = prh_ref[0, 0, a:a + 1, :], pih_ref[0, 0, a:a + 1, :]
        cp_ref[grp, re_tile] = p_r * c_r - p_i * c_i
        cp_ref[grp, im_tile] = -(p_r * c_i + p_i * c_r)
        for s in range(lc):
            rows = slice(s * LANES + a * S5_GROUP, s * LANES + (a + 1) * S5_GROUP)
            p_r, p_i = prh_ref[0, lc - 1 - s, a:a + 1, :], pih_ref[0, lc - 1 - s, a:a + 1, :]
            w_re, w_im = p_r * b_r - p_i * b_i, p_r * b_i + p_i * b_r
            wf_ref[rows, re_tile] = w_re
            wf_ref[rows, im_tile] = w_im
            ws_ref[rows, re_tile] = w_re.astype(BF16)
            ws_ref[rows, im_tile] = w_im.astype(BF16)
            p_r, p_i = prh_ref[0, s + 1, a:a + 1, :], pih_ref[0, s + 1, a:a + 1, :]
            wyt_ref[rows, re_tile] = (p_r * c_r - p_i * c_i).astype(BF16)
            wyt_ref[rows, im_tile] = (-(p_r * c_i + p_i * c_r)).astype(BF16)
    k_all = _dot_nt_3pass(wf_ref[...], cp_ref[...]).astype(BF16)
    for tau in range(lc):
        kb = k_all[(lc - 1 - tau) * LANES:(lc - tau) * LANES, :]
        for s in range(lc - tau):
            t = s + tau
            m_ref[s * LANES:(s + 1) * LANES, t * LANES:(t + 1) * LANES] = kb


def _s5_body(x_ref, prh_ref, pih_ref, bd_ref, cd_ref, are_ref, aim_ref, d_ref, ca_ref, cb_ref,
             o_ref, ca_out_ref, cb_out_ref,
             m_ref, ws_ref, wyt_ref, wf_ref, cp_ref, lhs_ref, s_ref, xp_ref, stage_ref, carry_ref,
             *, lc, nb, nc, ns):
    nk = ns // LANES
    ca_out_ref[...] = ca_ref[...].astype(ca_out_ref.dtype)
    cb_out_ref[...] = cb_ref[...].astype(cb_out_ref.dtype)

    @pl.when(pl.program_id(1) == 0)
    def _():
        _s5_build(prh_ref, pih_ref, bd_ref, cd_ref, m_ref, ws_ref, wyt_ref, wf_ref, cp_ref, lc, nk)
        carry_ref[...] = jnp.zeros_like(carry_ref)

    for b in range(nb):
        for s in range(lc):
            piece = x_ref[b, pl.ds(s, nc, stride=lc), :]
            lhs_ref[b * nc:(b + 1) * nc, s * LANES:(s + 1) * LANES] = piece.astype(BF16)
    lhs = lhs_ref[...]
    s_all = jnp.dot(lhs, ws_ref[...], preferred_element_type=F32)
    for k in range(2 * nk):
        for b in range(nb):
            s_ref[k, pl.ds(b, nc, stride=nb), :] = (
                s_all[b * nc:(b + 1) * nc, k * LANES:(k + 1) * LANES])

    tile_rows = lambda ref, k: jnp.broadcast_to(ref[0, :, k * LANES:(k + 1) * LANES], (nb, LANES))
    ar = [tile_rows(are_ref, k) for k in range(nk)]
    ai = [tile_rows(aim_ref, k) for k in range(nk)]

    per = 8 // nb

    def step(i, carry):
        r = pl.ds(pl.multiple_of(i * 8, 8), 8)
        s_in = [s_ref[k, r, :] for k in range(2 * nk)]
        before = [[] for _ in range(2 * nk)]
        for u in range(per):
            new = []
            for k in range(nk):
                xr, xi = carry[2 * k], carry[2 * k + 1]
                before[k].append(xr)
                before[nk + k].append(xi)
                rows = slice(u * nb, (u + 1) * nb)
                new.append(ar[k] * xr - ai[k] * xi + s_in[k][rows])
                new.append(ar[k] * xi + ai[k] * xr + s_in[nk + k][rows])
            carry = tuple(new)
        for k in range(2 * nk):
            xp_ref[k, r, :] = jnp.concatenate(before[k], axis=0)
        return carry

    carry = lax.fori_loop(0, nc // per, step,
                          tuple(carry_ref[k, 0:nb, :] for k in range(2 * nk)))
    for k in range(2 * nk):
        carry_ref[k, 0:nb, :] = carry[k]

    xp = jnp.concatenate(
        [jnp.concatenate([xp_ref[k, pl.ds(b, nc, stride=nb), :] for b in range(nb)], axis=0)
         for k in range(2 * nk)], axis=-1)
    tw = 2 * LANES
    y_intra = jnp.concatenate(
        [jnp.dot(lhs[:, :t0 + tw], m_ref[:t0 + tw, t0:t0 + tw], preferred_element_type=F32)
         for t0 in range(0, lc * LANES, tw)], axis=-1)
    y = y_intra + lax.dot_general(xp.astype(BF16), wyt_ref[...], (((1,), (1,)), ((), ())),
                                  preferred_element_type=F32)
    for b in range(nb):
        for t in range(lc):
            stage_ref[b, pl.ds(t, nc, stride=lc), :] = (
                y[b * nc:(b + 1) * nc, t * LANES:(t + 1) * LANES])
    yy = stage_ref[...] + d_ref[...] * x_ref[...]
    o_ref[...] = jax.nn.gelu(yy).astype(o_ref.dtype)


def s5_scan(hn, tables, d_skip, cast_a, cast_b, *, tc, lc=S5_CHUNK):
    prh, pih, bd, cd, a_re, a_im = tables
    ng = GROUPS_PER_TILE
    nb, l, d = hn.shape
    nt = d // LANES
    nc = tc // lc
    ns = a_re.shape[-1]
    rows = nb * nc
    body = functools.partial(_s5_body, lc=lc, nb=nb, nc=nc, ns=ns)
    tile4 = lambda j, i: (j, 0, 0, 0)
    tile3 = lambda j, i: (j, 0, 0)
    assert 8 % nb == 0
    steps_i = l // tc
    step_of = lambda j, i: j * steps_i + i
    ca, ca_in, ca_out, ca_shape = _cast_slabs(cast_a, 0, nt * steps_i, step_of)
    cb, cb_in, cb_out, cb_shape = _cast_slabs(cast_b, 0, nt * steps_i, step_of)
    g, ca16, cb16 = pl.pallas_call(
        body,
        grid=(nt, l // tc),
        in_specs=[pl.BlockSpec((nb, tc, LANES), lambda j, i: (0, i, j)),
                  pl.BlockSpec((1, lc + 1, ng, LANES), tile4),
                  pl.BlockSpec((1, lc + 1, ng, LANES), tile4),
                  pl.BlockSpec((1, ng, S5_GROUP, 2 * LANES), tile4),
                  pl.BlockSpec((1, ng, S5_GROUP, 2 * LANES), tile4),
                  pl.BlockSpec((1, 1, ns), tile3),
                  pl.BlockSpec((1, 1, ns), tile3),
                  pl.BlockSpec((1, 1, LANES), lambda j, i: (0, 0, j)),
                  ca_in, cb_in],
        out_specs=[pl.BlockSpec((nb, tc, LANES), lambda j, i: (0, i, j)), ca_out, cb_out],
        out_shape=[jax.ShapeDtypeStruct((nb, l, d), BF16), ca_shape, cb_shape],
        scratch_shapes=[pltpu.VMEM((lc * LANES, lc * LANES), BF16),
                        pltpu.VMEM((lc * LANES, 2 * ns), BF16),
                        pltpu.VMEM((lc * LANES, 2 * ns), BF16),
                        pltpu.VMEM((lc * LANES, 2 * ns), F32),
                        pltpu.VMEM((LANES, 2 * ns), F32),
                        pltpu.VMEM((rows, lc * LANES), BF16),
                        pltpu.VMEM((2 * ns // LANES, rows, LANES), F32),
                        pltpu.VMEM((2 * ns // LANES, rows, LANES), F32),
                        pltpu.VMEM((nb, tc, LANES), F32),
                        pltpu.VMEM((2 * ns // LANES, 8, LANES), F32)],
        compiler_params=_params(2),
        name="s5_scan",
    )(hn, prh, pih, bd, cd, a_re, a_im, d_skip.reshape(1, 1, d).astype(F32), ca, cb)
    return g, ca16.reshape((1,) + cast_a.shape[1:]), cb16.reshape((1,) + cast_b.shape[1:])


def _rope(x, cos, sin_signed, first_half):
    w = x.shape[1]
    reps = w // LANES
    cosw = jnp.tile(cos, (1, reps))
    sinw = jnp.tile(sin_signed, (1, reps))
    sel = jnp.tile(first_half, (1, reps))
    half = HEAD_DIM // 2
    swapped = jnp.where(sel, pltpu.roll(x, w - half, 1), pltpu.roll(x, half, 1))
    return x * cosw + swapped * sinw


def _attn_body(sink_ref, q_ref, kc_ref, kp_ref, vc_ref, vp_ref, o_ref, *, q_per_kv):
    n = pl.program_id(1)
    blk = ATTN_BLOCK
    q = q_ref[...]
    k = jnp.concatenate([kp_ref[...], kc_ref[...]], axis=0).astype(F32)
    v = jnp.concatenate([vp_ref[...], vc_ref[...]], axis=0).astype(F32)

    from_prev = (lax.broadcasted_iota(jnp.int32, (blk, blk), 1)
                 > lax.broadcasted_iota(jnp.int32, (blk, blk), 0))
    no_prev = jnp.where(n > 0, 0.0, -jnp.inf).astype(F32)

    low = lax.broadcasted_iota(jnp.int32, (2 * blk, LANES), 1) < HEAD_DIM

    def split(tile, head_in_low):
        moved = pltpu.roll(tile, HEAD_DIM, 1)
        in_lo, in_hi = (tile, moved) if head_in_low else (moved, tile)
        return (jnp.where(low, in_lo, 0.0).astype(BF16),
                jnp.where(low, 0.0, in_hi).astype(BF16))

    for h in range(N_KV_HEADS):
        tile = slice((h // 2) * LANES, (h // 2 + 1) * LANES)
        k_lo, k_hi = split(k[:, tile], h % 2 == 0)
        v_lo, v_hi = split(v[:, tile], h % 2 == 0)
        for i in range(q_per_kv // 2):
            head = h * q_per_kv + 2 * i
            qt = q[:, (head // 2) * LANES:(head // 2 + 1) * LANES]
            acc = None
            for kk, vv, a in ((k_lo, v_lo, head), (k_hi, v_hi, head + 1)):
                s = lax.dot_general(qt, kk, (((1,), (1,)), ((), ())),
                                    preferred_element_type=F32)
                z = jnp.where(from_prev, s[:, :blk] + no_prev, s[:, blk:])
                sink = sink_ref[a] * LOG2E
                m = jnp.maximum(jnp.max(z, axis=-1, keepdims=True), sink)
                e = jnp.exp2(z - m)
                denom = jnp.sum(e, axis=-1, keepdims=True) + jnp.exp2(sink - m)
                p = jnp.concatenate([jnp.where(from_prev, e, 0.0), jnp.where(from_prev, 0.0, e)],
                                    axis=1).astype(BF16)
                o = jnp.dot(p, vv, preferred_element_type=F32) * (1.0 / denom)
                acc = o if acc is None else acc + o
            o_ref[:, (head // 2) * LANES:(head // 2 + 1) * LANES] = acc.astype(o_ref.dtype)


def _qkv_rope_body(x_ref, w_ref, b_ref, pos_ref, invf_ref, o_ref, cos_ref, sin_ref,
                   *, q_cols, rope_cols):
    j = pl.program_id(1)
    lane = lax.broadcasted_iota(jnp.int32, cos_ref.shape, 1)
    first_half = (lane % HEAD_DIM) < (HEAD_DIM // 2)

    @pl.when(j == 0)
    def _():
        ang = pos_ref[...].astype(F32) * invf_ref[...]
        sin = jnp.sin(ang)
        cos_ref[...] = jnp.cos(ang)
        sin_ref[...] = jnp.where(first_half, -sin, sin)

    w = w_ref[...].astype(BF16)
    bn = o_ref.shape[1]
    col = j * bn + lax.broadcasted_iota(jnp.int32, (1, bn), 1)
    scale = jnp.where(col < q_cols, HEAD_DIM ** -0.5 * LOG2E, 1.0)
    rotated = col < rope_cols
    chunk = x_ref.shape[0] // ROPE_ROW_CHUNKS
    lane_c = lax.broadcasted_iota(jnp.int32, (chunk, LANES), 1)
    first_half_c = (lane_c % HEAD_DIM) < (HEAD_DIM // 2)
    for r in range(ROPE_ROW_CHUNKS):
        rows = slice(r * chunk, (r + 1) * chunk)
        acc = jnp.dot(x_ref[rows, :], w, preferred_element_type=F32) + b_ref[...]
        rot = _rope(acc, cos_ref[rows, :], sin_ref[rows, :], first_half_c) * scale
        o_ref[rows, :] = jnp.where(rotated, rot, acc).astype(o_ref.dtype)


def qkv_rope_proj(x, w, b, positions, *, layer, bm, bn):
    m, k = x.shape
    n = w.shape[2]
    kvw = N_KV_HEADS * HEAD_DIM
    qw = n - 2 * kvw
    half = HEAD_DIM // 2
    inv_freq = jnp.power(ROPE_THETA, -jnp.arange(half, dtype=F32) / half)
    invf = jnp.tile(inv_freq, LANES // half).reshape(1, LANES)
    pos = positions.reshape(m, 1).astype(jnp.int32)
    body = functools.partial(_qkv_rope_body, q_cols=qw, rope_cols=qw + kvw)
    return pl.pallas_call(
        body,
        grid=(m // bm, n // bn),
        in_specs=[pl.BlockSpec((bm, k), lambda i, j: (i, 0)),
                  _layer_spec(k, bn, layer),
                  _layer_spec(1, bn, layer),
                  pl.BlockSpec((bm, 1), lambda i, j: (i, 0)),
                  pl.BlockSpec((1, LANES), lambda i, j: (0, 0))],
        out_specs=pl.BlockSpec((bm, bn), lambda i, j: (i, j)),
        out_shape=jax.ShapeDtypeStruct((m, n), BF16),
        scratch_shapes=[pltpu.VMEM((bm, LANES), F32), pltpu.VMEM((bm, LANES), F32)],
        compiler_params=_params(2),
        name="qkv_rope_proj",
    )(x, w, b.reshape(b.shape[0], 1, n), pos, invf)


def swa_attention(qkv, sinks, *, batch, seq):
    t, width = qkv.shape
    kvw = N_KV_HEADS * HEAD_DIM
    qw = width - 2 * kvw
    n_q_heads = qw // HEAD_DIM
    q_per_kv = n_q_heads // N_KV_HEADS
    assert q_per_kv % 2 == 0 and 2 * HEAD_DIM == LANES
    blk = ATTN_BLOCK
    nblk = seq // blk
    cur = lambda b, n: (b * nblk + n, 0)
    kcol = qw // kvw
    body = functools.partial(_attn_body, q_per_kv=q_per_kv)
    return pl.pallas_call(
        body,
        grid=(batch, nblk),
        in_specs=[pl.BlockSpec(memory_space=pltpu.SMEM),
                  pl.BlockSpec((blk, qw), cur),
                  pl.BlockSpec((blk, kvw), lambda b, n: (b * nblk + n, kcol)),
                  pl.BlockSpec((blk, kvw), lambda b, n: (b * nblk + jnp.maximum(n - 1, 0), kcol)),
                  pl.BlockSpec((blk, kvw), lambda b, n: (b * nblk + n, kcol + 1)),
                  pl.BlockSpec((blk, kvw), lambda b, n: (b * nblk + jnp.maximum(n - 1, 0), kcol + 1))],
        out_specs=pl.BlockSpec((blk, qw), cur),
        out_shape=jax.ShapeDtypeStruct((t, qw), BF16),
        compiler_params=_params(2),
        name="swa_attention",
    )(sinks.astype(F32), qkv, qkv, qkv, qkv, qkv)


def _ffn(hn, w_gate, w_up, w_down, layer):
    act, w_down_bf16 = swiglu_up(hn, w_gate, w_up, w_down, layer=layer, bm=2048, bn=256)
    return matmul(act, w_down_bf16, layer=0, bm=512, bn=512, name="ffn_down")


def kernel(x, positions, norm_pre_mix, norm_post_mix, norm_pre_ffn, norm_post_ffn, s5_lam_re, s5_lam_im, s5_log_step, s5_b_re, s5_b_im, s5_c_re, s5_c_im, s5_d, s5_w_out1, s5_b_out1, s5_w_out2, s5_b_out2, attn_w_qkv, attn_b_qkv, attn_w_o, attn_b_o, attn_sinks, ffn_w_gate, ffn_w_up, ffn_w_down):
    bsz, seq, d = x.shape
    t = bsz * seq
    h = x.reshape(t, d)

    hn = rmsnorm(h, norm_pre_mix[0], F32)
    tables = _s5_tables(s5_lam_re[0], s5_lam_im[0], s5_log_step[0], s5_b_re[0], s5_b_im[0],
                        s5_c_re[0], s5_c_im[0], S5_CHUNK)
    g, w_qkv, w_o = s5_scan(hn.reshape(bsz, seq, d), tables, s5_d[0], attn_w_qkv, attn_w_o,
                            tc=2048)
    g = g.reshape(t, d)
    m = glu_matmul(g, s5_w_out1, s5_b_out1, s5_w_out2, s5_b_out2, layer=0, bm=1024, bn=512)
    hn = mix_norm(h, m, norm_post_mix[0], norm_pre_ffn[0])
    f = _ffn(hn, ffn_w_gate, ffn_w_up, ffn_w_down, 0)
    h, hn = block_resid_norm(h, m, f, norm_post_mix[0], norm_post_ffn[0], norm_pre_mix[1])

    qkv = qkv_rope_proj(hn, w_qkv, attn_b_qkv, positions, layer=0, bm=1024, bn=1024)
    att = swa_attention(qkv, attn_sinks[0], batch=bsz, seq=seq)
    m = matmul(att, w_o, attn_b_o, layer=0, bm=1024, bn=1024, name="o_proj")
    hn = mix_norm(h, m, norm_post_mix[1], norm_pre_ffn[1])
    f = _ffn(hn, ffn_w_gate, ffn_w_up, ffn_w_down, 1)
    h = block_resid(h, m, f, norm_post_mix[1], norm_post_ffn[1])
    return h.reshape(bsz, seq, d)
```

```python
import functools
import math

import jax
import jax.numpy as jnp
from jax import lax
from jax.experimental import pallas as pl
from jax.experimental.pallas import tpu as pltpu

F32 = jnp.float32
BF16 = jnp.bfloat16

EPS = 1e-6
LANES = 128
S5_GROUP = 16
GROUPS_PER_TILE = LANES // S5_GROUP
S5_CHUNK = 8
HEAD_DIM = 64
N_KV_HEADS = 8
ATTN_BLOCK = 128
ATTN_SUB_BLOCKS = 2
ROPE_THETA = 10000.0
ROPE_ROW_CHUNKS = 8
LOG2E = math.log2(math.e)
VMEM_LIMIT = 60 * 1024 * 1024


def _params(n_axes, vmem=VMEM_LIMIT):
    return pltpu.CompilerParams(
        dimension_semantics=("arbitrary",) * n_axes, vmem_limit_bytes=vmem)


def _rms(x, g):
    r = lax.rsqrt(jnp.mean(x * x, axis=-1, keepdims=True) + EPS)
    return x * r * g


def _rmsnorm_body(x_ref, g_ref, o_ref):
    o_ref[...] = _rms(x_ref[...], g_ref[...]).astype(o_ref.dtype)


def _row_call(body, n_rows_in, n_vec_in, out_dtypes, t, d, name):
    streams = n_rows_in + len(out_dtypes)
    rows = 512 if streams <= 3 else 256
    row_spec = pl.BlockSpec((rows, d), lambda i: (i, 0))
    vec_spec = pl.BlockSpec((1, d), lambda i: (0, 0))
    return pl.pallas_call(
        body,
        grid=(t // rows,),
        in_specs=[row_spec] * n_rows_in + [vec_spec] * n_vec_in,
        out_specs=[row_spec] * len(out_dtypes),
        out_shape=[jax.ShapeDtypeStruct((t, d), dt) for dt in out_dtypes],
        compiler_params=_params(1),
        name=name,
    )


def rmsnorm(x, g, out_dtype):
    t, d = x.shape
    return _row_call(_rmsnorm_body, 1, 1, [out_dtype], t, d, "rmsnorm")(x, g.reshape(1, d))[0]


def _mix_norm_body(h_ref, m_ref, gm_ref, gpre_ref, hn_ref):
    h = h_ref[...] + _rms(m_ref[...].astype(F32), gm_ref[...])
    hn_ref[...] = _rms(h, gpre_ref[...]).astype(hn_ref.dtype)


def mix_norm(h, m, g_m, g_pre):
    t, d = h.shape
    return _row_call(_mix_norm_body, 2, 2, [BF16], t, d, "mix_norm")(
        h, m, g_m.reshape(1, d), g_pre.reshape(1, d))[0]


def _block_resid_norm_body(h_ref, m_ref, f_ref, gm_ref, gf_ref, gpre_ref, hout_ref, hn_ref):
    h = h_ref[...] + _rms(m_ref[...].astype(F32), gm_ref[...])
    h = h + _rms(f_ref[...].astype(F32), gf_ref[...])
    hout_ref[...] = h
    hn_ref[...] = _rms(h, gpre_ref[...]).astype(hn_ref.dtype)


def block_resid_norm(h, m, f, g_m, g_f, g_pre):
    t, d = h.shape
    return _row_call(_block_resid_norm_body, 3, 3, [F32, BF16], t, d, "block_resid_norm")(
        h, m, f, g_m.reshape(1, d), g_f.reshape(1, d), g_pre.reshape(1, d))


def _block_resid_body(h_ref, m_ref, f_ref, gm_ref, gf_ref, hout_ref):
    h = h_ref[...] + _rms(m_ref[...].astype(F32), gm_ref[...])
    hout_ref[...] = h + _rms(f_ref[...].astype(F32), gf_ref[...])


def block_resid(h, m, f, g_m, g_f):
    t, d = h.shape
    return _row_call(_block_resid_body, 3, 2, [F32], t, d, "block_resid")(
        h, m, f, g_m.reshape(1, d), g_f.reshape(1, d))[0]


def _mm_bias_body(x_ref, w_ref, b_ref, o_ref):
    acc = jnp.dot(x_ref[...], w_ref[...].astype(BF16), preferred_element_type=F32)
    o_ref[...] = (acc + b_ref[...]).astype(o_ref.dtype)


def _mm_body(x_ref, w_ref, o_ref):
    o_ref[...] = jnp.dot(x_ref[...], w_ref[...].astype(BF16),
                         preferred_element_type=F32).astype(o_ref.dtype)


def _layer_spec(rows, bn, layer):
    return pl.BlockSpec((None, rows, bn), lambda i, j: (layer, 0, j))


def matmul(x, w, b=None, *, layer, bm, bn, out_dtype=BF16, name="matmul"):
    m, k = x.shape
    n = w.shape[2]
    in_specs = [pl.BlockSpec((bm, k), lambda i, j: (i, 0)), _layer_spec(k, bn, layer)]
    args = [x, w]
    body = _mm_body
    if b is not None:
        in_specs.append(_layer_spec(1, bn, layer))
        args.append(b.reshape(b.shape[0], 1, n))
        body = _mm_bias_body
    return pl.pallas_call(
        body,
        grid=(m // bm, n // bn),
        in_specs=in_specs,
        out_specs=pl.BlockSpec((bm, bn), lambda i, j: (i, j)),
        out_shape=jax.ShapeDtypeStruct((m, n), out_dtype),
        compiler_params=_params(2),
        name=name,
    )(*args)


def _cast_slabs(w, layer, steps, step_of):
    layers, rows, cols = w.shape
    slab = rows // steps
    assert slab * steps == rows and slab % 16 == 0
    in_spec = pl.BlockSpec((None, None, slab, cols), lambda i, j: (layer, step_of(i, j), 0, 0))
    out_spec = pl.BlockSpec((None, slab, cols), lambda i, j: (step_of(i, j), 0, 0))
    out_shape = jax.ShapeDtypeStruct((steps, slab, cols), BF16)
    return w.reshape(layers, steps, slab, cols), in_spec, out_spec, out_shape


def _glu_body(x_ref, w1_ref, b1_ref, w2_ref, b2_ref, o_ref):
    x = x_ref[...]
    a = jnp.dot(x, w1_ref[...].astype(BF16), preferred_element_type=F32) + b1_ref[...]
    b = jnp.dot(x, w2_ref[...].astype(BF16), preferred_element_type=F32) + b2_ref[...]
    o_ref[...] = (a * jax.nn.sigmoid(b)).astype(o_ref.dtype)


def glu_matmul(x, w1, b1, w2, b2, *, layer, bm, bn):
    m, k = x.shape
    n = w1.shape[2]
    w_spec = _layer_spec(k, bn, layer)
    b_spec = _layer_spec(1, bn, layer)
    return pl.pallas_call(
        _glu_body,
        grid=(m // bm, n // bn),
        in_specs=[pl.BlockSpec((bm, k), lambda i, j: (i, 0)),
                  w_spec, b_spec, w_spec, b_spec],
        out_specs=pl.BlockSpec((bm, bn), lambda i, j: (i, j)),
        out_shape=jax.ShapeDtypeStruct((m, n), BF16),
        compiler_params=_params(2),
        name="glu_matmul",
    )(x, w1, b1.reshape(b1.shape[0], 1, n), w2, b2.reshape(b2.shape[0], 1, n))


def _swiglu_up_body(x_ref, wg_ref, wu_ref, wd_ref, o_ref, wd_out_ref):
    x = x_ref[...]
    a = jnp.dot(x, wg_ref[...].astype(BF16), preferred_element_type=F32)
    b = jnp.dot(x, wu_ref[...].astype(BF16), preferred_element_type=F32)
    o_ref[...] = (jax.nn.silu(a) * b).astype(o_ref.dtype)
    wd_out_ref[...] = wd_ref[...].astype(wd_out_ref.dtype)


def swiglu_up(x, wg, wu, wd, *, layer, bm, bn):
    m, k = x.shape
    n = wg.shape[2]
    steps_j = n // bn
    wd4, wd_in, wd_out, wd_shape = _cast_slabs(
        wd, layer, (m // bm) * steps_j, lambda i, j: i * steps_j + j)
    w_spec = _layer_spec(k, bn, layer)
    x_spec = pl.BlockSpec((bm, k), lambda i, j: (i, 0), pipeline_mode=pl.Buffered(1))
    act, wd_bf16 = pl.pallas_call(
        _swiglu_up_body,
        grid=(m // bm, steps_j),
        in_specs=[x_spec, w_spec, w_spec, wd_in],
        out_specs=[pl.BlockSpec((bm, bn), lambda i, j: (i, j)), wd_out],
        out_shape=[jax.ShapeDtypeStruct((m, n), BF16), wd_shape],
        compiler_params=_params(2),
        name="swiglu_up",
    )(x, wg, wu, wd4)
    return act, wd_bf16.reshape((1,) + wd.shape[1:])


def _s5_tables(lam_re, lam_im, log_step, b_re, b_im, c_re, c_im, lc):
    g, p = lam_re.shape
    ch = b_re.shape[-1]
    assert ch == S5_GROUP and 2 * p == LANES
    ng = GROUPS_PER_TILE
    nt = g // ng
    lr, li = lam_re.astype(F32), lam_im.astype(F32)
    step = jnp.exp(log_step.astype(F32))[:, None]
    zr, zi = lr * step, li * step
    taus = jnp.arange(lc + 1, dtype=F32)[:, None, None]
    mag = jnp.exp(zr[None] * taus)
    pr = mag * jnp.cos(zi[None] * taus)
    pi = mag * jnp.sin(zi[None] * taus)
    nr, ni = pr[1] - 1.0, pi[1]
    den = lr * lr + li * li
    qr = (nr * lr + ni * li) / den
    qi = (ni * lr - nr * li) / den
    b_r = jnp.swapaxes(b_re.astype(F32), 1, 2)
    b_i = jnp.swapaxes(b_im.astype(F32), 1, 2)
    br = qr[:, None, :] * b_r - qi[:, None, :] * b_i
    bi = qr[:, None, :] * b_i + qi[:, None, :] * b_r
    cr, ci = c_re.astype(F32), c_im.astype(F32)

    zero = jnp.zeros_like(pr)
    even = (jnp.arange(g) % 2 == 0)[None, :, None]
    halves = lambda a: jnp.concatenate(
        [jnp.where(even, a, zero), jnp.where(even, zero, a)], axis=-1)
    per_tile = lambda a: jnp.swapaxes(a.reshape(lc + 1, nt, ng, a.shape[-1]), 0, 1)
    grp = lambda a: a.reshape((nt, ng) + a.shape[1:])
    cat = lambda *xs: jnp.concatenate(xs, axis=-1)
    return (per_tile(halves(pr)), per_tile(halves(pi)),
            grp(cat(br, br, bi, bi)), grp(cat(cr, cr, ci, ci)),
            pr[lc].reshape(nt, 1, ng * p), pi[lc].reshape(nt, 1, ng * p))


def _split_bf16(x):
    hi = x.astype(BF16)
    return hi, (x - hi.astype(F32)).astype(BF16)


def _dot_nt_3pass(a, b):
    nt = lambda u, v: lax.dot_general(u, v, (((1,), (1,)), ((), ())), preferred_element_type=F32)
    a_hi, a_lo = _split_bf16(a)
    b_hi, b_lo = _split_bf16(b)
    return nt(a_hi, b_hi) + nt(a_hi, b_lo) + nt(a_lo, b_hi)


def _s5_build(prh_ref, pih_ref, bd_ref, cd_ref, m_ref, ws_ref, wyt_ref, wf_ref, cp_ref, lc, nk):
    @pl.when(pl.program_id(0) == 0)
    def _():
        ws_ref[...] = jnp.zeros_like(ws_ref)
        wyt_ref[...] = jnp.zeros_like(wyt_ref)
        wf_ref[...] = jnp.zeros_like(wf_ref)
        cp_ref[...] = jnp.zeros_like(cp_ref)
        m_ref[...] = jnp.zeros_like(m_ref)

    for a in range(GROUPS_PER_TILE):
        bd, cd = bd_ref[0, a], cd_ref[0, a]
        b_r, b_i = bd[:, :LANES], bd[:, LANES:]
        c_r, c_i = cd[:, :LANES], cd[:, LANES:]
        re_tile = slice((a // 2) * LANES, (a // 2 + 1) * LANES)
        im_tile = slice((nk + a // 2) * LANES, (nk + a // 2 + 1) * LANES)
        grp = slice(a * S5_GROUP, (a + 1) * S5_GROUP)
        p_r, p_i = prh_ref[0, 0, a:a + 1, :], pih_ref[0, 0, a:a + 1, :]
        cp_ref[grp, re_tile] = p_r * c_r - p_i * c_i
        cp_ref[grp, im_tile] = -(p_r * c_i + p_i * c_r)
        for s in range(lc):
            rows = slice(s * LANES + a * S5_GROUP, s * LANES + (a + 1) * S5_GROUP)
            p_r, p_i = prh_ref[0, lc - 1 - s, a:a + 1, :], pih_ref[0, lc - 1 - s, a:a + 1, :]
            w_re, w_im = p_r * b_r - p_i * b_i, p_r * b_i + p_i * b_r
            wf_ref[rows, re_tile] = w_re
            wf_ref[rows, im_tile] = w_im
            ws_ref[rows, re_tile] = w_re.astype(BF16)
            ws_ref[rows, im_tile] = w_im.astype(BF16)
            p_r, p_i = prh_ref[0, s + 1, a:a + 1, :], pih_ref[0, s + 1, a:a + 1, :]
            wyt_ref[rows, re_tile] = (p_r * c_r - p_i * c_i).astype(BF16)
            wyt_ref[rows, im_tile] = (-(p_r * c_i + p_i * c_r)).astype(BF16)
    k_all = _dot_nt_3pass(wf_ref[...], cp_ref[...]).astype(BF16)
    for tau in range(lc):
        kb = k_all[(lc - 1 - tau) * LANES:(lc - tau) * LANES, :]
        for s in range(lc - tau):
            t = s + tau
            m_ref[s * LANES:(s + 1) * LANES, t * LANES:(t + 1) * LANES] = kb


def _s5_body(x_ref, prh_ref, pih_ref, bd_ref, cd_ref, are_ref, aim_ref, d_ref, ca_ref, cb_ref,
             o_ref, ca_out_ref, cb_out_ref,
             m_ref, ws_ref, wyt_ref, wf_ref, cp_ref, lhs_ref, s_ref, xp_ref, stage_ref, carry_ref,
             *, lc, nb, nc, ns):
    nk = ns // LANES
    ca_out_ref[...] = ca_ref[...].astype(ca_out_ref.dtype)
    cb_out_ref[...] = cb_ref[...].astype(cb_out_ref.dtype)

    @pl.when(pl.program_id(1) == 0)
    def _():
        _s5_build(prh_ref, pih_ref, bd_ref, cd_ref, m_ref, ws_ref, wyt_ref, wf_ref, cp_ref, lc, nk)
        carry_ref[...] = jnp.zeros_like(carry_ref)

    for b in range(nb):
        for s in range(lc):
            piece = x_ref[b, pl.ds(s, nc, stride=lc), :]
            lhs_ref[b * nc:(b + 1) * nc, s * LANES:(s + 1) * LANES] = piece.astype(BF16)
    lhs = lhs_ref[...]
    s_all = jnp.dot(lhs, ws_ref[...], preferred_element_type=F32)
    for k in range(2 * nk):
        for b in range(nb):
            s_ref[k, pl.ds(b, nc, stride=nb), :] = (
                s_all[b * nc:(b + 1) * nc, k * LANES:(k + 1) * LANES])

    tile_rows = lambda ref, k: jnp.broadcast_to(ref[0, :, k * LANES:(k + 1) * LANES], (nb, LANES))
    ar = [tile_rows(are_ref, k) for k in range(nk)]
    ai = [tile_rows(aim_ref, k) for k in range(nk)]

    per = 8 // nb

    def step(i, carry):
        r = pl.ds(pl.multiple_of(i * 8, 8), 8)
        s_in = [s_ref[k, r, :] for k in range(2 * nk)]
        before = [[] for _ in range(2 * nk)]
        for u in range(per):
            new = []
            for k in range(nk):
                xr, xi = carry[2 * k], carry[2 * k + 1]
                before[k].append(xr)
                before[nk + k].append(xi)
                rows = slice(u * nb, (u + 1) * nb)
                new.append(ar[k] * xr - ai[k] * xi + s_in[k][rows])
                new.append(ar[k] * xi + ai[k] * xr + s_in[nk + k][rows])
            carry = tuple(new)
        for k in range(2 * nk):
            xp_ref[k, r, :] = jnp.concatenate(before[k], axis=0)
        return carry

    carry = lax.fori_loop(0, nc // per, step,
                          tuple(carry_ref[k, 0:nb, :] for k in range(2 * nk)))
    for k in range(2 * nk):
        carry_ref[k, 0:nb, :] = carry[k]

    xp = jnp.concatenate(
        [jnp.concatenate([xp_ref[k, pl.ds(b, nc, stride=nb), :] for b in range(nb)], axis=0)
         for k in range(2 * nk)], axis=-1)
    tw = 2 * LANES
    y_intra = jnp.concatenate(
        [jnp.dot(lhs[:, :t0 + tw], m_ref[:t0 + tw, t0:t0 + tw], preferred_element_type=F32)
         for t0 in range(0, lc * LANES, tw)], axis=-1)
    y = y_intra + lax.dot_general(xp.astype(BF16), wyt_ref[...], (((1,), (1,)), ((), ())),
                                  preferred_element_type=F32)
    for b in range(nb):
        for t in range(lc):
            stage_ref[b, pl.ds(t, nc, stride=lc), :] = (
                y[b * nc:(b + 1) * nc, t * LANES:(t + 1) * LANES])
    yy = stage_ref[...] + d_ref[...] * x_ref[...]
    o_ref[...] = jax.nn.gelu(yy).astype(o_ref.dtype)


def s5_scan(hn, tables, d_skip, cast_a, cast_b, *, tc, lc=S5_CHUNK):
    prh, pih, bd, cd, a_re, a_im = tables
    ng = GROUPS_PER_TILE
    nb, l, d = hn.shape
    nt = d // LANES
    nc = tc // lc
    ns = a_re.shape[-1]
    rows = nb * nc
    body = functools.partial(_s5_body, lc=lc, nb=nb, nc=nc, ns=ns)
    tile4 = lambda j, i: (j, 0, 0, 0)
    tile3 = lambda j, i: (j, 0, 0)
    assert 8 % nb == 0
    steps_i = l // tc
    step_of = lambda j, i: j * steps_i + i
    ca, ca_in, ca_out, ca_shape = _cast_slabs(cast_a, 0, nt * steps_i, step_of)
    cb, cb_in, cb_out, cb_shape = _cast_slabs(cast_b, 0, nt * steps_i, step_of)
    g, ca16, cb16 = pl.pallas_call(
        body,
        grid=(nt, l // tc),
        in_specs=[pl.BlockSpec((nb, tc, LANES), lambda j, i: (0, i, j)),
                  pl.BlockSpec((1, lc + 1, ng, LANES), tile4),
                  pl.BlockSpec((1, lc + 1, ng, LANES), tile4),
                  pl.BlockSpec((1, ng, S5_GROUP, 2 * LANES), tile4),
                  pl.BlockSpec((1, ng, S5_GROUP, 2 * LANES), tile4),
                  pl.BlockSpec((1, 1, ns), tile3),
                  pl.BlockSpec((1, 1, ns), tile3),
                  pl.BlockSpec((1, 1, LANES), lambda j, i: (0, 0, j)),
                  ca_in, cb_in],
        out_specs=[pl.BlockSpec((nb, tc, LANES), lambda j, i: (0, i, j)), ca_out, cb_out],
        out_shape=[jax.ShapeDtypeStruct((nb, l, d), BF16), ca_shape, cb_shape],
        scratch_shapes=[pltpu.VMEM((lc * LANES, lc * LANES), BF16),
                        pltpu.VMEM((lc * LANES, 2 * ns), BF16),
                        pltpu.VMEM((lc * LANES, 2 * ns), BF16),
                        pltpu.VMEM((lc * LANES, 2 * ns), F32),
                        pltpu.VMEM((LANES, 2 * ns), F32),
                        pltpu.VMEM((rows, lc * LANES), BF16),
                        pltpu.VMEM((2 * ns // LANES, rows, LANES), F32),
                        pltpu.VMEM((2 * ns // LANES, rows, LANES), F32),
                        pltpu.VMEM((nb, tc, LANES), F32),
                        pltpu.VMEM((2 * ns // LANES, 8, LANES), F32)],
        compiler_params=_params(2),
        name="s5_scan",
    )(hn, prh, pih, bd, cd, a_re, a_im, d_skip.reshape(1, 1, d).astype(F32), ca, cb)
    return g, ca16.reshape((1,) + cast_a.shape[1:]), cb16.reshape((1,) + cast_b.shape[1:])


def _rope(x, cos, sin_signed, first_half):
    w = x.shape[1]
    reps = w // LANES
    cosw = jnp.tile(cos, (1, reps))
    sinw = jnp.tile(sin_signed, (1, reps))
    sel = jnp.tile(first_half, (1, reps))
    half = HEAD_DIM // 2
    swapped = jnp.where(sel, pltpu.roll(x, w - half, 1), pltpu.roll(x, half, 1))
    return x * cosw + swapped * sinw


def _attn_body(sink_ref, q_ref, kc_ref, kp_ref, vc_ref, vp_ref, o_ref, *, q_per_kv, sub_blocks):
    n = pl.program_id(1)
    blk = ATTN_BLOCK
    k_all = jnp.concatenate([kp_ref[...], kc_ref[...]], axis=0).astype(F32)
    v_all = jnp.concatenate([vp_ref[...], vc_ref[...]], axis=0).astype(F32)
    for u in range(sub_blocks):
        rows = slice(u * blk, (u + 1) * blk)
        no_prev = jnp.where(n > 0, 0.0, -jnp.inf).astype(F32) if u == 0 else jnp.float32(0.0)
        _attn_block(sink_ref, q_ref[rows, :], k_all[u * blk:(u + 2) * blk],
                    v_all[u * blk:(u + 2) * blk], no_prev, o_ref, rows, q_per_kv)


def _attn_block(sink_ref, q, k, v, no_prev, o_ref, rows, q_per_kv):
    blk = ATTN_BLOCK
    from_prev = (lax.broadcasted_iota(jnp.int32, (blk, blk), 1)
                 > lax.broadcasted_iota(jnp.int32, (blk, blk), 0))

    low = lax.broadcasted_iota(jnp.int32, (2 * blk, LANES), 1) < HEAD_DIM

    def split(tile, head_in_low):
        moved = pltpu.roll(tile, HEAD_DIM, 1)
        in_lo, in_hi = (tile, moved) if head_in_low else (moved, tile)
        return (jnp.where(low, in_lo, 0.0).astype(BF16),
                jnp.where(low, 0.0, in_hi).astype(BF16))

    for h in range(N_KV_HEADS):
        tile = slice((h // 2) * LANES, (h // 2 + 1) * LANES)
        k_lo, k_hi = split(k[:, tile], h % 2 == 0)
        v_lo, v_hi = split(v[:, tile], h % 2 == 0)
        for i in range(q_per_kv // 2):
            head = h * q_per_kv + 2 * i
            qt = q[:, (head // 2) * LANES:(head // 2 + 1) * LANES]
            acc = None
            for kk, vv, a in ((k_lo, v_lo, head), (k_hi, v_hi, head + 1)):
                s = lax.dot_general(qt, kk, (((1,), (1,)), ((), ())),
                                    preferred_element_type=F32)
                z = jnp.where(from_prev, s[:, :blk] + no_prev, s[:, blk:])
                sink = sink_ref[a] * LOG2E
                m = jnp.maximum(jnp.max(z, axis=-1, keepdims=True), sink)
                e = jnp.exp2(z - m)
                denom = jnp.sum(e, axis=-1, keepdims=True) + jnp.exp2(sink - m)
                p = jnp.concatenate([jnp.where(from_prev, e, 0.0), jnp.where(from_prev, 0.0, e)],
                                    axis=1).astype(BF16)
                o = jnp.dot(p, vv, preferred_element_type=F32) * (1.0 / denom)
                acc = o if acc is None else acc + o
            o_ref[rows, (head // 2) * LANES:(head // 2 + 1) * LANES] = acc.astype(o_ref.dtype)


def _qkv_rope_body(x_ref, w_ref, b_ref, pos_ref, invf_ref, o_ref, cos_ref, sin_ref,
                   *, q_cols, rope_cols):
    j = pl.program_id(1)
    lane = lax.broadcasted_iota(jnp.int32, cos_ref.shape, 1)
    first_half = (lane % HEAD_DIM) < (HEAD_DIM // 2)

    @pl.when(j == 0)
    def _():
        ang = pos_ref[...].astype(F32) * invf_ref[...]
        sin = jnp.sin(ang)
        cos_ref[...] = jnp.cos(ang)
        sin_ref[...] = jnp.where(first_half, -sin, sin)

    w = w_ref[...].astype(BF16)
    bn = o_ref.shape[1]
    col = j * bn + lax.broadcasted_iota(jnp.int32, (1, bn), 1)
    scale = jnp.where(col < q_cols, HEAD_DIM ** -0.5 * LOG2E, 1.0)
    rotated = col < rope_cols
    chunk = x_ref.shape[0] // ROPE_ROW_CHUNKS
    lane_c = lax.broadcasted_iota(jnp.int32, (chunk, LANES), 1)
    first_half_c = (lane_c % HEAD_DIM) < (HEAD_DIM // 2)
    for r in range(ROPE_ROW_CHUNKS):
        rows = slice(r * chunk, (r + 1) * chunk)
        acc = jnp.dot(x_ref[rows, :], w, preferred_element_type=F32) + b_ref[...]
        rot = _rope(acc, cos_ref[rows, :], sin_ref[rows, :], first_half_c) * scale
        o_ref[rows, :] = jnp.where(rotated, rot, acc).astype(o_ref.dtype)


def qkv_rope_proj(x, w, b, positions, *, layer, bm, bn):
    m, k = x.shape
    n = w.shape[2]
    kvw = N_KV_HEADS * HEAD_DIM
    qw = n - 2 * kvw
    half = HEAD_DIM // 2
    inv_freq = jnp.power(ROPE_THETA, -jnp.arange(half, dtype=F32) / half)
    invf = jnp.tile(inv_freq, LANES // half).reshape(1, LANES)
    pos = positions.reshape(m, 1).astype(jnp.int32)
    body = functools.partial(_qkv_rope_body, q_cols=qw, rope_cols=qw + kvw)
    return pl.pallas_call(
        body,
        grid=(m // bm, n // bn),
        in_specs=[pl.BlockSpec((bm, k), lambda i, j: (i, 0)),
                  _layer_spec(k, bn, layer),
                  _layer_spec(1, bn, layer),
                  pl.BlockSpec((bm, 1), lambda i, j: (i, 0)),
                  pl.BlockSpec((1, LANES), lambda i, j: (0, 0))],
        out_specs=pl.BlockSpec((bm, bn), lambda i, j: (i, j)),
        out_shape=jax.ShapeDtypeStruct((m, n), BF16),
        scratch_shapes=[pltpu.VMEM((bm, LANES), F32), pltpu.VMEM((bm, LANES), F32)],
        compiler_params=_params(2),
        name="qkv_rope_proj",
    )(x, w, b.reshape(b.shape[0], 1, n), pos, invf)


def swa_attention(qkv, sinks, *, batch, seq):
    t, width = qkv.shape
    kvw = N_KV_HEADS * HEAD_DIM
    qw = width - 2 * kvw
    n_q_heads = qw // HEAD_DIM
    q_per_kv = n_q_heads // N_KV_HEADS
    assert q_per_kv % 2 == 0 and 2 * HEAD_DIM == LANES
    blk = ATTN_BLOCK
    sub = ATTN_SUB_BLOCKS
    nblk = seq // blk
    nstep = nblk // sub
    assert nstep * sub == nblk
    cur = lambda b, n: (b * nstep + n, 0)
    prev = lambda col: (lambda b, n: (b * nblk + jnp.maximum(n * sub - 1, 0), col))
    kcol = qw // kvw
    body = functools.partial(_attn_body, q_per_kv=q_per_kv, sub_blocks=sub)
    return pl.pallas_call(
        body,
        grid=(batch, nstep),
        in_specs=[pl.BlockSpec(memory_space=pltpu.SMEM),
                  pl.BlockSpec((sub * blk, qw), cur),
                  pl.BlockSpec((sub * blk, kvw), lambda b, n: (b * nstep + n, kcol)),
                  pl.BlockSpec((blk, kvw), prev(kcol)),
                  pl.BlockSpec((sub * blk, kvw), lambda b, n: (b * nstep + n, kcol + 1)),
                  pl.BlockSpec((blk, kvw), prev(kcol + 1))],
        out_specs=pl.BlockSpec((sub * blk, qw), cur),
        out_shape=jax.ShapeDtypeStruct((t, qw), BF16),
        compiler_params=_params(2),
        name="swa_attention",
    )(sinks.astype(F32), qkv, qkv, qkv, qkv, qkv)


def _ffn(hn, w_gate, w_up, w_down, layer):
    act, w_down_bf16 = swiglu_up(hn, w_gate, w_up, w_down, layer=layer, bm=2048, bn=256)
    return matmul(act, w_down_bf16, layer=0, bm=512, bn=512, name="ffn_down")


def kernel(x, positions, norm_pre_mix, norm_post_mix, norm_pre_ffn, norm_post_ffn, s5_lam_re, s5_lam_im, s5_log_step, s5_b_re, s5_b_im, s5_c_re, s5_c_im, s5_d, s5_w_out1, s5_b_out1, s5_w_out2, s5_b_out2, attn_w_qkv, attn_b_qkv, attn_w_o, attn_b_o, attn_sinks, ffn_w_gate, ffn_w_up, ffn_w_down):
    bsz, seq, d = x.shape
    t = bsz * seq
    h = x.reshape(t, d)

    hn = rmsnorm(h, norm_pre_mix[0], F32)
    tables = _s5_tables(s5_lam_re[0], s5_lam_im[0], s5_log_step[0], s5_b_re[0], s5_b_im[0],
                        s5_c_re[0], s5_c_im[0], S5_CHUNK)
    g, w_qkv, w_o = s5_scan(hn.reshape(bsz, seq, d), tables, s5_d[0], attn_w_qkv, attn_w_o,
                            tc=2048)
    g = g.reshape(t, d)
    m = glu_matmul(g, s5_w_out1, s5_b_out1, s5_w_out2, s5_b_out2, layer=0, bm=1024, bn=512)
    hn = mix_norm(h, m, norm_post_mix[0], norm_pre_ffn[0])
    f = _ffn(hn, ffn_w_gate, ffn_w_up, ffn_w_down, 0)
    h, hn = block_resid_norm(h, m, f, norm_post_mix[0], norm_post_ffn[0], norm_pre_mix[1])

    qkv = qkv_rope_proj(hn, w_qkv, attn_b_qkv, positions, layer=0, bm=1024, bn=1024)
    att = swa_attention(qkv, attn_sinks[0], batch=bsz, seq=seq)
    m = matmul(att, w_o, attn_b_o, layer=0, bm=1024, bn=1024, name="o_proj")
    hn = mix_norm(h, m, norm_post_mix[1], norm_pre_ffn[1])
    f = _ffn(hn, ffn_w_gate, ffn_w_up, ffn_w_down, 1)
    h = block_resid(h, m, f, norm_post_mix[1], norm_post_ffn[1])
    return h.reshape(bsz, seq, d)
```

```python
import functools
import math

import jax
import jax.numpy as jnp
from jax import lax
from jax.experimental import pallas as pl
from jax.experimental.pallas import tpu as pltpu

F32 = jnp.float32
BF16 = jnp.bfloat16

EPS = 1e-6
LANES = 128
S5_GROUP = 16
GROUPS_PER_TILE = LANES // S5_GROUP
S5_CHUNK = 8
HEAD_DIM = 64
N_KV_HEADS = 8
ATTN_BLOCK = 128
ATTN_SUB_BLOCKS = 4
ROPE_THETA = 10000.0
ROPE_ROW_CHUNKS = 8
LOG2E = math.log2(math.e)
VMEM_LIMIT = 60 * 1024 * 1024


def _params(n_axes, vmem=VMEM_LIMIT):
    return pltpu.CompilerParams(
        dimension_semantics=("arbitrary",) * n_axes, vmem_limit_bytes=vmem)


def _rms(x, g):
    r = lax.rsqrt(jnp.mean(x * x, axis=-1, keepdims=True) + EPS)
    return x * r * g


def _rmsnorm_body(x_ref, g_ref, o_ref):
    o_ref[...] = _rms(x_ref[...], g_ref[...]).astype(o_ref.dtype)


def _row_call(body, n_rows_in, n_vec_in, out_dtypes, t, d, name):
    streams = n_rows_in + len(out_dtypes)
    rows = 512 if streams <= 3 else 256
    row_spec = pl.BlockSpec((rows, d), lambda i: (i, 0))
    vec_spec = pl.BlockSpec((1, d), lambda i: (0, 0))
    return pl.pallas_call(
        body,
        grid=(t // rows,),
        in_specs=[row_spec] * n_rows_in + [vec_spec] * n_vec_in,
        out_specs=[row_spec] * len(out_dtypes),
        out_shape=[jax.ShapeDtypeStruct((t, d), dt) for dt in out_dtypes],
        compiler_params=_params(1),
        name=name,
    )


def rmsnorm(x, g, out_dtype):
    t, d = x.shape
    return _row_call(_rmsnorm_body, 1, 1, [out_dtype], t, d, "rmsnorm")(x, g.reshape(1, d))[0]


def _mix_norm_body(h_ref, m_ref, gm_ref, gpre_ref, hn_ref):
    h = h_ref[...] + _rms(m_ref[...].astype(F32), gm_ref[...])
    hn_ref[...] = _rms(h, gpre_ref[...]).astype(hn_ref.dtype)


def mix_norm(h, m, g_m, g_pre):
    t, d = h.shape
    return _row_call(_mix_norm_body, 2, 2, [BF16], t, d, "mix_norm")(
        h, m, g_m.reshape(1, d), g_pre.reshape(1, d))[0]


def _block_resid_norm_body(h_ref, m_ref, f_ref, gm_ref, gf_ref, gpre_ref, hout_ref, hn_ref):
    h = h_ref[...] + _rms(m_ref[...].astype(F32), gm_ref[...])
    h = h + _rms(f_ref[...].astype(F32), gf_ref[...])
    hout_ref[...] = h
    hn_ref[...] = _rms(h, gpre_ref[...]).astype(hn_ref.dtype)


def block_resid_norm(h, m, f, g_m, g_f, g_pre):
    t, d = h.shape
    return _row_call(_block_resid_norm_body, 3, 3, [F32, BF16], t, d, "block_resid_norm")(
        h, m, f, g_m.reshape(1, d), g_f.reshape(1, d), g_pre.reshape(1, d))


def _block_resid_body(h_ref, m_ref, f_ref, gm_ref, gf_ref, hout_ref):
    h = h_ref[...] + _rms(m_ref[...].astype(F32), gm_ref[...])
    hout_ref[...] = h + _rms(f_ref[...].astype(F32), gf_ref[...])


def block_resid(h, m, f, g_m, g_f):
    t, d = h.shape
    return _row_call(_block_resid_body, 3, 2, [F32], t, d, "block_resid")(
        h, m, f, g_m.reshape(1, d), g_f.reshape(1, d))[0]


def _mm_bias_body(x_ref, w_ref, b_ref, o_ref):
    acc = jnp.dot(x_ref[...], w_ref[...].astype(BF16), preferred_element_type=F32)
    o_ref[...] = (acc + b_ref[...]).astype(o_ref.dtype)


def _mm_body(x_ref, w_ref, o_ref):
    o_ref[...] = jnp.dot(x_ref[...], w_ref[...].astype(BF16),
                         preferred_element_type=F32).astype(o_ref.dtype)


def _layer_spec(rows, bn, layer):
    return pl.BlockSpec((None, rows, bn), lambda i, j: (layer, 0, j))


def matmul(x, w, b=None, *, layer, bm, bn, out_dtype=BF16, name="matmul"):
    m, k = x.shape
    n = w.shape[2]
    in_specs = [pl.BlockSpec((bm, k), lambda i, j: (i, 0)), _layer_spec(k, bn, layer)]
    args = [x, w]
    body = _mm_body
    if b is not None:
        in_specs.append(_layer_spec(1, bn, layer))
        args.append(b.reshape(b.shape[0], 1, n))
        body = _mm_bias_body
    return pl.pallas_call(
        body,
        grid=(m // bm, n // bn),
        in_specs=in_specs,
        out_specs=pl.BlockSpec((bm, bn), lambda i, j: (i, j)),
        out_shape=jax.ShapeDtypeStruct((m, n), out_dtype),
        compiler_params=_params(2),
        name=name,
    )(*args)


def _cast_slabs(w, layer, steps, step_of):
    layers, rows, cols = w.shape
    slab = rows // steps
    assert slab * steps == rows and slab % 16 == 0
    in_spec = pl.BlockSpec((None, None, slab, cols), lambda i, j: (layer, step_of(i, j), 0, 0))
    out_spec = pl.BlockSpec((None, slab, cols), lambda i, j: (step_of(i, j), 0, 0))
    out_shape = jax.ShapeDtypeStruct((steps, slab, cols), BF16)
    return w.reshape(layers, steps, slab, cols), in_spec, out_spec, out_shape


def _glu_body(x_ref, w1_ref, b1_ref, w2_ref, b2_ref, o_ref):
    x = x_ref[...]
    a = jnp.dot(x, w1_ref[...].astype(BF16), preferred_element_type=F32) + b1_ref[...]
    b = jnp.dot(x, w2_ref[...].astype(BF16), preferred_element_type=F32) + b2_ref[...]
    o_ref[...] = (a * jax.nn.sigmoid(b)).astype(o_ref.dtype)


def glu_matmul(x, w1, b1, w2, b2, *, layer, bm, bn):
    m, k = x.shape
    n = w1.shape[2]
    w_spec = _layer_spec(k, bn, layer)
    b_spec = _layer_spec(1, bn, layer)
    return pl.pallas_call(
        _glu_body,
        grid=(m // bm, n // bn),
        in_specs=[pl.BlockSpec((bm, k), lambda i, j: (i, 0)),
                  w_spec, b_spec, w_spec, b_spec],
        out_specs=pl.BlockSpec((bm, bn), lambda i, j: (i, j)),
        out_shape=jax.ShapeDtypeStruct((m, n), BF16),
        compiler_params=_params(2),
        name="glu_matmul",
    )(x, w1, b1.reshape(b1.shape[0], 1, n), w2, b2.reshape(b2.shape[0], 1, n))


def _swiglu_up_body(x_ref, wg_ref, wu_ref, wd_ref, o_ref, wd_out_ref):
    x = x_ref[...]
    a = jnp.dot(x, wg_ref[...].astype(BF16), preferred_element_type=F32)
    b = jnp.dot(x, wu_ref[...].astype(BF16), preferred_element_type=F32)
    o_ref[...] = (jax.nn.silu(a) * b).astype(o_ref.dtype)
    wd_out_ref[...] = wd_ref[...].astype(wd_out_ref.dtype)


def swiglu_up(x, wg, wu, wd, *, layer, bm, bn):
    m, k = x.shape
    n = wg.shape[2]
    steps_j = n // bn
    wd4, wd_in, wd_out, wd_shape = _cast_slabs(
        wd, layer, (m // bm) * steps_j, lambda i, j: i * steps_j + j)
    w_spec = _layer_spec(k, bn, layer)
    x_spec = pl.BlockSpec((bm, k), lambda i, j: (i, 0), pipeline_mode=pl.Buffered(1))
    act, wd_bf16 = pl.pallas_call(
        _swiglu_up_body,
        grid=(m // bm, steps_j),
        in_specs=[x_spec, w_spec, w_spec, wd_in],
        out_specs=[pl.BlockSpec((bm, bn), lambda i, j: (i, j)), wd_out],
        out_shape=[jax.ShapeDtypeStruct((m, n), BF16), wd_shape],
        compiler_params=_params(2),
        name="swiglu_up",
    )(x, wg, wu, wd4)
    return act, wd_bf16.reshape((1,) + wd.shape[1:])


def _s5_tables(lam_re, lam_im, log_step, b_re, b_im, c_re, c_im, lc):
    g, p = lam_re.shape
    ch = b_re.shape[-1]
    assert ch == S5_GROUP and 2 * p == LANES
    ng = GROUPS_PER_TILE
    nt = g // ng
    lr, li = lam_re.astype(F32), lam_im.astype(F32)
    step = jnp.exp(log_step.astype(F32))[:, None]
    zr, zi = lr * step, li * step
    taus = jnp.arange(lc + 1, dtype=F32)[:, None, None]
    mag = jnp.exp(zr[None] * taus)
    pr = mag * jnp.cos(zi[None] * taus)
    pi = mag * jnp.sin(zi[None] * taus)
    nr, ni = pr[1] - 1.0, pi[1]
    den = lr * lr + li * li
    qr = (nr * lr + ni * li) / den
    qi = (ni * lr - nr * li) / den
    b_r = jnp.swapaxes(b_re.astype(F32), 1, 2)
    b_i = jnp.swapaxes(b_im.astype(F32), 1, 2)
    br = qr[:, None, :] * b_r - qi[:, None, :] * b_i
    bi = qr[:, None, :] * b_i + qi[:, None, :] * b_r
    cr, ci = c_re.astype(F32), c_im.astype(F32)

    zero = jnp.zeros_like(pr)
    even = (jnp.arange(g) % 2 == 0)[None, :, None]
    halves = lambda a: jnp.concatenate(
        [jnp.where(even, a, zero), jnp.where(even, zero, a)], axis=-1)
    per_tile = lambda a: jnp.swapaxes(a.reshape(lc + 1, nt, ng, a.shape[-1]), 0, 1)
    grp = lambda a: a.reshape((nt, ng) + a.shape[1:])
    cat = lambda *xs: jnp.concatenate(xs, axis=-1)
    return (per_tile(halves(pr)), per_tile(halves(pi)),
            grp(cat(br, br, bi, bi)), grp(cat(cr, cr, ci, ci)),
            pr[lc].reshape(nt, 1, ng * p), pi[lc].reshape(nt, 1, ng * p))


def _split_bf16(x):
    hi = x.astype(BF16)
    return hi, (x - hi.astype(F32)).astype(BF16)


def _dot_nt_3pass(a, b):
    nt = lambda u, v: lax.dot_general(u, v, (((1,), (1,)), ((), ())), preferred_element_type=F32)
    a_hi, a_lo = _split_bf16(a)
    b_hi, b_lo = _split_bf16(b)
    return nt(a_hi, b_hi) + nt(a_hi, b_lo) + nt(a_lo, b_hi)


def _s5_build(prh_ref, pih_ref, bd_ref, cd_ref, m_ref, ws_ref, wyt_ref, wf_ref, cp_ref, lc, nk):
    @pl.when(pl.program_id(0) == 0)
    def _():
        ws_ref[...] = jnp.zeros_like(ws_ref)
        wyt_ref[...] = jnp.zeros_like(wyt_ref)
        wf_ref[...] = jnp.zeros_like(wf_ref)
        cp_ref[...] = jnp.zeros_like(cp_ref)
        m_ref[...] = jnp.zeros_like(m_ref)

    for a in range(GROUPS_PER_TILE):
        bd, cd = bd_ref[0, a], cd_ref[0, a]
        b_r, b_i = bd[:, :LANES], bd[:, LANES:]
        c_r, c_i = cd[:, :LANES], cd[:, LANES:]
        re_tile = slice((a // 2) * LANES, (a // 2 + 1) * LANES)
        im_tile = slice((nk + a // 2) * LANES, (nk + a // 2 + 1) * LANES)
        grp = slice(a * S5_GROUP, (a + 1) * S5_GROUP)
        p_r, p_i = prh_ref[0, 0, a:a + 1, :], pih_ref[0, 0, a:a + 1, :]
        cp_ref[grp, re_tile] = p_r * c_r - p_i * c_i
        cp_ref[grp, im_tile] = -(p_r * c_i + p_i * c_r)
        for s in range(lc):
            rows = slice(s * LANES + a * S5_GROUP, s * LANES + (a + 1) * S5_GROUP)
            p_r, p_i = prh_ref[0, lc - 1 - s, a:a + 1, :], pih_ref[0, lc - 1 - s, a:a + 1, :]
            w_re, w_im = p_r * b_r - p_i * b_i, p_r * b_i + p_i * b_r
            wf_ref[rows, re_tile] = w_re
            wf_ref[rows, im_tile] = w_im
            ws_ref[rows, re_tile] = w_re.astype(BF16)
            ws_ref[rows, im_tile] = w_im.astype(BF16)
            p_r, p_i = prh_ref[0, s + 1, a:a + 1, :], pih_ref[0, s + 1, a:a + 1, :]
            wyt_ref[rows, re_tile] = (p_r * c_r - p_i * c_i).astype(BF16)
            wyt_ref[rows, im_tile] = (-(p_r * c_i + p_i * c_r)).astype(BF16)
    k_all = _dot_nt_3pass(wf_ref[...], cp_ref[...]).astype(BF16)
    for tau in range(lc):
        kb = k_all[(lc - 1 - tau) * LANES:(lc - tau) * LANES, :]
        for s in range(lc - tau):
            t = s + tau
            m_ref[s * LANES:(s + 1) * LANES, t * LANES:(t + 1) * LANES] = kb


def _s5_body(x_ref, prh_ref, pih_ref, bd_ref, cd_ref, are_ref, aim_ref, d_ref, ca_ref, cb_ref,
             o_ref, ca_out_ref, cb_out_ref,
             m_ref, ws_ref, wyt_ref, wf_ref, cp_ref, lhs_ref, s_ref, xp_ref, stage_ref, carry_ref,
             *, lc, nb, nc, ns):
    nk = ns // LANES
    ca_out_ref[...] = ca_ref[...].astype(ca_out_ref.dtype)
    cb_out_ref[...] = cb_ref[...].astype(cb_out_ref.dtype)

    @pl.when(pl.program_id(1) == 0)
    def _():
        _s5_build(prh_ref, pih_ref, bd_ref, cd_ref, m_ref, ws_ref, wyt_ref, wf_ref, cp_ref, lc, nk)
        carry_ref[...] = jnp.zeros_like(carry_ref)

    for b in range(nb):
        for s in range(lc):
            piece = x_ref[b, pl.ds(s, nc, stride=lc), :]
            lhs_ref[b * nc:(b + 1) * nc, s * LANES:(s + 1) * LANES] = piece.astype(BF16)
    lhs = lhs_ref[...]
    s_all = jnp.dot(lhs, ws_ref[...], preferred_element_type=F32)
    for k in range(2 * nk):
        for b in range(nb):
            s_ref[k, pl.ds(b, nc, stride=nb), :] = (
                s_all[b * nc:(b + 1) * nc, k * LANES:(k + 1) * LANES])

    tile_rows = lambda ref, k: jnp.broadcast_to(ref[0, :, k * LANES:(k + 1) * LANES], (nb, LANES))
    ar = [tile_rows(are_ref, k) for k in range(nk)]
    ai = [tile_rows(aim_ref, k) for k in range(nk)]

    per = 8 // nb

    def step(i, carry):
        r = pl.ds(pl.multiple_of(i * 8, 8), 8)
        s_in = [s_ref[k, r, :] for k in range(2 * nk)]
        before = [[] for _ in range(2 * nk)]
        for u in range(per):
            new = []
            for k in range(nk):
                xr, xi = carry[2 * k], carry[2 * k + 1]
                before[k].append(xr)
                before[nk + k].append(xi)
                rows = slice(u * nb, (u + 1) * nb)
                new.append(ar[k] * xr - ai[k] * xi + s_in[k][rows])
                new.append(ar[k] * xi + ai[k] * xr + s_in[nk + k][rows])
            carry = tuple(new)
        for k in range(2 * nk):
            xp_ref[k, r, :] = jnp.concatenate(before[k], axis=0)
        return carry

    carry = lax.fori_loop(0, nc // per, step,
                          tuple(carry_ref[k, 0:nb, :] for k in range(2 * nk)))
    for k in range(2 * nk):
        carry_ref[k, 0:nb, :] = carry[k]

    xp = jnp.concatenate(
        [jnp.concatenate([xp_ref[k, pl.ds(b, nc, stride=nb), :] for b in range(nb)], axis=0)
         for k in range(2 * nk)], axis=-1)
    tw = 2 * LANES
    y_intra = jnp.concatenate(
        [jnp.dot(lhs[:, :t0 + tw], m_ref[:t0 + tw, t0:t0 + tw], preferred_element_type=F32)
         for t0 in range(0, lc * LANES, tw)], axis=-1)
    y = y_intra + lax.dot_general(xp.astype(BF16), wyt_ref[...], (((1,), (1,)), ((), ())),
                                  preferred_element_type=F32)
    for b in range(nb):
        for t in range(lc):
            stage_ref[b, pl.ds(t, nc, stride=lc), :] = (
                y[b * nc:(b + 1) * nc, t * LANES:(t + 1) * LANES])
    yy = stage_ref[...] + d_ref[...] * x_ref[...]
    o_ref[...] = jax.nn.gelu(yy).astype(o_ref.dtype)


def s5_scan(hn, tables, d_skip, cast_a, cast_b, *, tc, lc=S5_CHUNK):
    prh, pih, bd, cd, a_re, a_im = tables
    ng = GROUPS_PER_TILE
    nb, l, d = hn.shape
    nt = d // LANES
    nc = tc // lc
    ns = a_re.shape[-1]
    rows = nb * nc
    body = functools.partial(_s5_body, lc=lc, nb=nb, nc=nc, ns=ns)
    tile4 = lambda j, i: (j, 0, 0, 0)
    tile3 = lambda j, i: (j, 0, 0)
    assert 8 % nb == 0
    steps_i = l // tc
    step_of = lambda j, i: j * steps_i + i
    ca, ca_in, ca_out, ca_shape = _cast_slabs(cast_a, 0, nt * steps_i, step_of)
    cb, cb_in, cb_out, cb_shape = _cast_slabs(cast_b, 0, nt * steps_i, step_of)
    g, ca16, cb16 = pl.pallas_call(
        body,
        grid=(nt, l // tc),
        in_specs=[pl.BlockSpec((nb, tc, LANES), lambda j, i: (0, i, j)),
                  pl.BlockSpec((1, lc + 1, ng, LANES), tile4),
                  pl.BlockSpec((1, lc + 1, ng, LANES), tile4),
                  pl.BlockSpec((1, ng, S5_GROUP, 2 * LANES), tile4),
                  pl.BlockSpec((1, ng, S5_GROUP, 2 * LANES), tile4),
                  pl.BlockSpec((1, 1, ns), tile3),
                  pl.BlockSpec((1, 1, ns), tile3),
                  pl.BlockSpec((1, 1, LANES), lambda j, i: (0, 0, j)),
                  ca_in, cb_in],
        out_specs=[pl.BlockSpec((nb, tc, LANES), lambda j, i: (0, i, j)), ca_out, cb_out],
        out_shape=[jax.ShapeDtypeStruct((nb, l, d), BF16), ca_shape, cb_shape],
        scratch_shapes=[pltpu.VMEM((lc * LANES, lc * LANES), BF16),
                        pltpu.VMEM((lc * LANES, 2 * ns), BF16),
                        pltpu.VMEM((lc * LANES, 2 * ns), BF16),
                        pltpu.VMEM((lc * LANES, 2 * ns), F32),
                        pltpu.VMEM((LANES, 2 * ns), F32),
                        pltpu.VMEM((rows, lc * LANES), BF16),
                        pltpu.VMEM((2 * ns // LANES, rows, LANES), F32),
                        pltpu.VMEM((2 * ns // LANES, rows, LANES), F32),
                        pltpu.VMEM((nb, tc, LANES), F32),
                        pltpu.VMEM((2 * ns // LANES, 8, LANES), F32)],
        compiler_params=_params(2),
        name="s5_scan",
    )(hn, prh, pih, bd, cd, a_re, a_im, d_skip.reshape(1, 1, d).astype(F32), ca, cb)
    return g, ca16.reshape((1,) + cast_a.shape[1:]), cb16.reshape((1,) + cast_b.shape[1:])


def _rope(x, cos, sin_signed, first_half):
    w = x.shape[1]
    reps = w // LANES
    cosw = jnp.tile(cos, (1, reps))
    sinw = jnp.tile(sin_signed, (1, reps))
    sel = jnp.tile(first_half, (1, reps))
    half = HEAD_DIM // 2
    swapped = jnp.where(sel, pltpu.roll(x, w - half, 1), pltpu.roll(x, half, 1))
    return x * cosw + swapped * sinw


def _attn_body(sink_ref, q_ref, kc_ref, kp_ref, vc_ref, vp_ref, o_ref, *, q_per_kv, sub_blocks):
    n = pl.program_id(1)
    blk = ATTN_BLOCK
    k_all = jnp.concatenate([kp_ref[...], kc_ref[...]], axis=0).astype(F32)
    v_all = jnp.concatenate([vp_ref[...], vc_ref[...]], axis=0).astype(F32)
    from_prev = (lax.broadcasted_iota(jnp.int32, (blk, blk), 1)
                 > lax.broadcasted_iota(jnp.int32, (blk, blk), 0))
    no_prev = [jnp.where(n > 0, 0.0, -jnp.inf).astype(F32)] + [jnp.float32(0.0)] * (sub_blocks - 1)
    low = lax.broadcasted_iota(jnp.int32, (k_all.shape[0], LANES), 1) < HEAD_DIM

    def split(tile, head_in_low):
        moved = pltpu.roll(tile, HEAD_DIM, 1)
        in_lo, in_hi = (tile, moved) if head_in_low else (moved, tile)
        return (jnp.where(low, in_lo, 0.0).astype(BF16),
                jnp.where(low, 0.0, in_hi).astype(BF16))

    for h in range(N_KV_HEADS):
        tile = slice((h // 2) * LANES, (h // 2 + 1) * LANES)
        k_lo, k_hi = split(k_all[:, tile], h % 2 == 0)
        v_lo, v_hi = split(v_all[:, tile], h % 2 == 0)
        for i in range(q_per_kv // 2):
            head = h * q_per_kv + 2 * i
            cols = slice((head // 2) * LANES, (head // 2 + 1) * LANES)
            for u in range(sub_blocks):
                rows = slice(u * blk, (u + 1) * blk)
                keys = slice(u * blk, (u + 2) * blk)
                qt = q_ref[rows, cols]
                acc = None
                for kk, vv, a in ((k_lo, v_lo, head), (k_hi, v_hi, head + 1)):
                    s = lax.dot_general(qt, kk[keys], (((1,), (1,)), ((), ())),
                                        preferred_element_type=F32)
                    z = jnp.where(from_prev, s[:, :blk] + no_prev[u], s[:, blk:])
                    sink = sink_ref[a] * LOG2E
                    m = jnp.maximum(jnp.max(z, axis=-1, keepdims=True), sink)
                    e = jnp.exp2(z - m)
                    denom = jnp.sum(e, axis=-1, keepdims=True) + jnp.exp2(sink - m)
                    p = jnp.concatenate(
                        [jnp.where(from_prev, e, 0.0), jnp.where(from_prev, 0.0, e)],
                        axis=1).astype(BF16)
                    o = jnp.dot(p, vv[keys], preferred_element_type=F32) * (1.0 / denom)
                    acc = o if acc is None else acc + o
                o_ref[rows, cols] = acc.astype(o_ref.dtype)


def _qkv_rope_body(x_ref, w_ref, b_ref, pos_ref, invf_ref, o_ref, cos_ref, sin_ref,
                   *, q_cols, rope_cols):
    j = pl.program_id(1)
    lane = lax.broadcasted_iota(jnp.int32, cos_ref.shape, 1)
    first_half = (lane % HEAD_DIM) < (HEAD_DIM // 2)

    @pl.when(j == 0)
    def _():
        ang = pos_ref[...].astype(F32) * invf_ref[...]
        sin = jnp.sin(ang)
        cos_ref[...] = jnp.cos(ang)
        sin_ref[...] = jnp.where(first_half, -sin, sin)

    w = w_ref[...].astype(BF16)
    bn = o_ref.shape[1]
    col = j * bn + lax.broadcasted_iota(jnp.int32, (1, bn), 1)
    scale = jnp.where(col < q_cols, HEAD_DIM ** -0.5 * LOG2E, 1.0)
    rotated = col < rope_cols
    chunk = x_ref.shape[0] // ROPE_ROW_CHUNKS
    lane_c = lax.broadcasted_iota(jnp.int32, (chunk, LANES), 1)
    first_half_c = (lane_c % HEAD_DIM) < (HEAD_DIM // 2)
    for r in range(ROPE_ROW_CHUNKS):
        rows = slice(r * chunk, (r + 1) * chunk)
        acc = jnp.dot(x_ref[rows, :], w, preferred_element_type=F32) + b_ref[...]
        rot = _rope(acc, cos_ref[rows, :], sin_ref[rows, :], first_half_c) * scale
        o_ref[rows, :] = jnp.where(rotated, rot, acc).astype(o_ref.dtype)


def qkv_rope_proj(x, w, b, positions, *, layer, bm, bn):
    m, k = x.shape
    n = w.shape[2]
    kvw = N_KV_HEADS * HEAD_DIM
    qw = n - 2 * kvw
    half = HEAD_DIM // 2
    inv_freq = jnp.power(ROPE_THETA, -jnp.arange(half, dtype=F32) / half)
    invf = jnp.tile(inv_freq, LANES // half).reshape(1, LANES)
    pos = positions.reshape(m, 1).astype(jnp.int32)
    body = functools.partial(_qkv_rope_body, q_cols=qw, rope_cols=qw + kvw)
    return pl.pallas_call(
        body,
        grid=(m // bm, n // bn),
        in_specs=[pl.BlockSpec((bm, k), lambda i, j: (i, 0)),
                  _layer_spec(k, bn, layer),
                  _layer_spec(1, bn, layer),
                  pl.BlockSpec((bm, 1), lambda i, j: (i, 0)),
                  pl.BlockSpec((1, LANES), lambda i, j: (0, 0))],
        out_specs=pl.BlockSpec((bm, bn), lambda i, j: (i, j)),
        out_shape=jax.ShapeDtypeStruct((m, n), BF16),
        scratch_shapes=[pltpu.VMEM((bm, LANES), F32), pltpu.VMEM((bm, LANES), F32)],
        compiler_params=_params(2),
        name="qkv_rope_proj",
    )(x, w, b.reshape(b.shape[0], 1, n), pos, invf)


def swa_attention(qkv, sinks, *, batch, seq):
    t, width = qkv.shape
    kvw = N_KV_HEADS * HEAD_DIM
    qw = width - 2 * kvw
    n_q_heads = qw // HEAD_DIM
    q_per_kv = n_q_heads // N_KV_HEADS
    assert q_per_kv % 2 == 0 and 2 * HEAD_DIM == LANES
    blk = ATTN_BLOCK
    sub = ATTN_SUB_BLOCKS
    nblk = seq // blk
    nstep = nblk // sub
    assert nstep * sub == nblk
    cur = lambda b, n: (b * nstep + n, 0)
    prev = lambda col: (lambda b, n: (b * nblk + jnp.maximum(n * sub - 1, 0), col))
    kcol = qw // kvw
    body = functools.partial(_attn_body, q_per_kv=q_per_kv, sub_blocks=sub)
    return pl.pallas_call(
        body,
        grid=(batch, nstep),
        in_specs=[pl.BlockSpec(memory_space=pltpu.SMEM),
                  pl.BlockSpec((sub * blk, qw), cur),
                  pl.BlockSpec((sub * blk, kvw), lambda b, n: (b * nstep + n, kcol)),
                  pl.BlockSpec((blk, kvw), prev(kcol)),
                  pl.BlockSpec((sub * blk, kvw), lambda b, n: (b * nstep + n, kcol + 1)),
                  pl.BlockSpec((blk, kvw), prev(kcol + 1))],
        out_specs=pl.BlockSpec((sub * blk, qw), cur),
        out_shape=jax.ShapeDtypeStruct((t, qw), BF16),
        compiler_params=_params(2),
        name="swa_attention",
    )(sinks.astype(F32), qkv, qkv, qkv, qkv, qkv)


def _ffn(hn, w_gate, w_up, w_down, layer):
    act, w_down_bf16 = swiglu_up(hn, w_gate, w_up, w_down, layer=layer, bm=2048, bn=256)
    return matmul(act, w_down_bf16, layer=0, bm=512, bn=512, name="ffn_down")


def kernel(x, positions, norm_pre_mix, norm_post_mix, norm_pre_ffn, norm_post_ffn, s5_lam_re, s5_lam_im, s5_log_step, s5_b_re, s5_b_im, s5_c_re, s5_c_im, s5_d, s5_w_out1, s5_b_out1, s5_w_out2, s5_b_out2, attn_w_qkv, attn_b_qkv, attn_w_o, attn_b_o, attn_sinks, ffn_w_gate, ffn_w_up, ffn_w_down):
    bsz, seq, d = x.shape
    t = bsz * seq
    h = x.reshape(t, d)

    hn = rmsnorm(h, norm_pre_mix[0], F32)
    tables = _s5_tables(s5_lam_re[0], s5_lam_im[0], s5_log_step[0], s5_b_re[0], s5_b_im[0],
                        s5_c_re[0], s5_c_im[0], S5_CHUNK)
    g, w_qkv, w_o = s5_scan(hn.reshape(bsz, seq, d), tables, s5_d[0], attn_w_qkv, attn_w_o,
                            tc=2048)
    g = g.reshape(t, d)
    m = glu_matmul(g, s5_w_out1, s5_b_out1, s5_w_out2, s5_b_out2, layer=0, bm=1024, bn=512)
    hn = mix_norm(h, m, norm_post_mix[0], norm_pre_ffn[0])
    f = _ffn(hn, ffn_w_gate, ffn_w_up, ffn_w_down, 0)
    h, hn = block_resid_norm(h, m, f, norm_post_mix[0], norm_post_ffn[0], norm_pre_mix[1])

    qkv = qkv_rope_proj(hn, w_qkv, attn_b_qkv, positions, layer=0, bm=1024, bn=1024)
    att = swa_attention(qkv, attn_sinks[0], batch=bsz, seq=seq)
    m = matmul(att, w_o, attn_b_o, layer=0, bm=1024, bn=1024, name="o_proj")
    hn = mix_norm(h, m, norm_post_mix[1], norm_pre_ffn[1])
    f = _ffn(hn, ffn_w_gate, ffn_w_up, ffn_w_down, 1)
    h = block_resid(h, m, f, norm_post_mix[1], norm_post_ffn[1])
    return h.reshape(bsz, seq, d)
```

```python
import functools
import math

import jax
import jax.numpy as jnp
from jax import lax
from jax.experimental import pallas as pl
from jax.experimental.pallas import tpu as pltpu

F32 = jnp.float32
BF16 = jnp.bfloat16

EPS = 1e-6
LANES = 128
S5_GROUP = 16
GROUPS_PER_TILE = LANES // S5_GROUP
S5_CHUNK = 8
HEAD_DIM = 64
N_KV_HEADS = 8
ATTN_BLOCK = 128
ATTN_SUB_BLOCKS = 2
ROPE_THETA = 10000.0
ROPE_ROW_CHUNKS = 8
LOG2E = math.log2(math.e)
VMEM_LIMIT = 60 * 1024 * 1024


def _params(n_axes, vmem=VMEM_LIMIT):
    return pltpu.CompilerParams(
        dimension_semantics=("arbitrary",) * n_axes, vmem_limit_bytes=vmem)


def _rms(x, g):
    r = lax.rsqrt(jnp.mean(x * x, axis=-1, keepdims=True) + EPS)
    return x * r * g


def _rmsnorm_body(x_ref, g_ref, o_ref):
    o_ref[...] = _rms(x_ref[...], g_ref[...]).astype(o_ref.dtype)


def _row_call(body, n_rows_in, n_vec_in, out_dtypes, t, d, name):
    streams = n_rows_in + len(out_dtypes)
    rows = 512 if streams <= 3 else 256
    row_spec = pl.BlockSpec((rows, d), lambda i: (i, 0))
    vec_spec = pl.BlockSpec((1, d), lambda i: (0, 0))
    return pl.pallas_call(
        body,
        grid=(t // rows,),
        in_specs=[row_spec] * n_rows_in + [vec_spec] * n_vec_in,
        out_specs=[row_spec] * len(out_dtypes),
        out_shape=[jax.ShapeDtypeStruct((t, d), dt) for dt in out_dtypes],
        compiler_params=_params(1),
        name=name,
    )


def rmsnorm(x, g, out_dtype):
    t, d = x.shape
    return _row_call(_rmsnorm_body, 1, 1, [out_dtype], t, d, "rmsnorm")(x, g.reshape(1, d))[0]


def _mix_norm_body(h_ref, m_ref, gm_ref, gpre_ref, hn_ref):
    h = h_ref[...] + _rms(m_ref[...].astype(F32), gm_ref[...])
    hn_ref[...] = _rms(h, gpre_ref[...]).astype(hn_ref.dtype)


def mix_norm(h, m, g_m, g_pre):
    t, d = h.shape
    return _row_call(_mix_norm_body, 2, 2, [BF16], t, d, "mix_norm")(
        h, m, g_m.reshape(1, d), g_pre.reshape(1, d))[0]


def _block_resid_norm_body(h_ref, m_ref, f_ref, gm_ref, gf_ref, gpre_ref, hout_ref, hn_ref):
    h = h_ref[...] + _rms(m_ref[...].astype(F32), gm_ref[...])
    h = h + _rms(f_ref[...].astype(F32), gf_ref[...])
    hout_ref[...] = h
    hn_ref[...] = _rms(h, gpre_ref[...]).astype(hn_ref.dtype)


def block_resid_norm(h, m, f, g_m, g_f, g_pre):
    t, d = h.shape
    return _row_call(_block_resid_norm_body, 3, 3, [F32, BF16], t, d, "block_resid_norm")(
        h, m, f, g_m.reshape(1, d), g_f.reshape(1, d), g_pre.reshape(1, d))


def _block_resid_body(h_ref, m_ref, f_ref, gm_ref, gf_ref, hout_ref):
    h = h_ref[...] + _rms(m_ref[...].astype(F32), gm_ref[...])
    hout_ref[...] = h + _rms(f_ref[...].astype(F32), gf_ref[...])


def block_resid(h, m, f, g_m, g_f):
    t, d = h.shape
    return _row_call(_block_resid_body, 3, 2, [F32], t, d, "block_resid")(
        h, m, f, g_m.reshape(1, d), g_f.reshape(1, d))[0]


def _mm_bias_body(x_ref, w_ref, b_ref, o_ref):
    acc = jnp.dot(x_ref[...], w_ref[...].astype(BF16), preferred_element_type=F32)
    o_ref[...] = (acc + b_ref[...]).astype(o_ref.dtype)


def _mm_body(x_ref, w_ref, o_ref):
    o_ref[...] = jnp.dot(x_ref[...], w_ref[...].astype(BF16),
                         preferred_element_type=F32).astype(o_ref.dtype)


def _layer_spec(rows, bn, layer):
    return pl.BlockSpec((None, rows, bn), lambda i, j: (layer, 0, j))


def matmul(x, w, b=None, *, layer, bm, bn, out_dtype=BF16, name="matmul"):
    m, k = x.shape
    n = w.shape[2]
    in_specs = [pl.BlockSpec((bm, k), lambda i, j: (i, 0)), _layer_spec(k, bn, layer)]
    args = [x, w]
    body = _mm_body
    if b is not None:
        in_specs.append(_layer_spec(1, bn, layer))
        args.append(b.reshape(b.shape[0], 1, n))
        body = _mm_bias_body
    return pl.pallas_call(
        body,
        grid=(m // bm, n // bn),
        in_specs=in_specs,
        out_specs=pl.BlockSpec((bm, bn), lambda i, j: (i, j)),
        out_shape=jax.ShapeDtypeStruct((m, n), out_dtype),
        compiler_params=_params(2),
        name=name,
    )(*args)


def _cast_slabs(w, layer, steps, step_of):
    layers, rows, cols = w.shape
    slab = rows // steps
    assert slab * steps == rows and slab % 16 == 0
    in_spec = pl.BlockSpec((None, None, slab, cols), lambda i, j: (layer, step_of(i, j), 0, 0))
    out_spec = pl.BlockSpec((None, slab, cols), lambda i, j: (step_of(i, j), 0, 0))
    out_shape = jax.ShapeDtypeStruct((steps, slab, cols), BF16)
    return w.reshape(layers, steps, slab, cols), in_spec, out_spec, out_shape


def _glu_body(x_ref, w1_ref, b1_ref, w2_ref, b2_ref, o_ref):
    x = x_ref[...]
    a = jnp.dot(x, w1_ref[...].astype(BF16), preferred_element_type=F32) + b1_ref[...]
    b = jnp.dot(x, w2_ref[...].astype(BF16), preferred_element_type=F32) + b2_ref[...]
    o_ref[...] = (a * jax.nn.sigmoid(b)).astype(o_ref.dtype)


def glu_matmul(x, w1, b1, w2, b2, *, layer, bm, bn):
    m, k = x.shape
    n = w1.shape[2]
    w_spec = _layer_spec(k, bn, layer)
    b_spec = _layer_spec(1, bn, layer)
    return pl.pallas_call(
        _glu_body,
        grid=(m // bm, n // bn),
        in_specs=[pl.BlockSpec((bm, k), lambda i, j: (i, 0)),
                  w_spec, b_spec, w_spec, b_spec],
        out_specs=pl.BlockSpec((bm, bn), lambda i, j: (i, j)),
        out_shape=jax.ShapeDtypeStruct((m, n), BF16),
        compiler_params=_params(2),
        name="glu_matmul",
    )(x, w1, b1.reshape(b1.shape[0], 1, n), w2, b2.reshape(b2.shape[0], 1, n))


def _swiglu_up_body(x_ref, wg_ref, wu_ref, wd_ref, o_ref, wd_out_ref):
    x = x_ref[...]
    a = jnp.dot(x, wg_ref[...].astype(BF16), preferred_element_type=F32)
    b = jnp.dot(x, wu_ref[...].astype(BF16), preferred_element_type=F32)
    o_ref[...] = (jax.nn.silu(a) * b).astype(o_ref.dtype)
    wd_out_ref[...] = wd_ref[...].astype(wd_out_ref.dtype)


def swiglu_up(x, wg, wu, wd, *, layer, bm, bn):
    m, k = x.shape
    n = wg.shape[2]
    steps_j = n // bn
    wd4, wd_in, wd_out, wd_shape = _cast_slabs(
        wd, layer, (m // bm) * steps_j, lambda i, j: i * steps_j + j)
    w_spec = _layer_spec(k, bn, layer)
    x_spec = pl.BlockSpec((bm, k), lambda i, j: (i, 0), pipeline_mode=pl.Buffered(1))
    act, wd_bf16 = pl.pallas_call(
        _swiglu_up_body,
        grid=(m // bm, steps_j),
        in_specs=[x_spec, w_spec, w_spec, wd_in],
        out_specs=[pl.BlockSpec((bm, bn), lambda i, j: (i, j)), wd_out],
        out_shape=[jax.ShapeDtypeStruct((m, n), BF16), wd_shape],
        compiler_params=_params(2),
        name="swiglu_up",
    )(x, wg, wu, wd4)
    return act, wd_bf16.reshape((1,) + wd.shape[1:])


def _s5_tables(lam_re, lam_im, log_step, b_re, b_im, c_re, c_im, lc):
    g, p = lam_re.shape
    ch = b_re.shape[-1]
    assert ch == S5_GROUP and 2 * p == LANES
    ng = GROUPS_PER_TILE
    nt = g // ng
    lr, li = lam_re.astype(F32), lam_im.astype(F32)
    step = jnp.exp(log_step.astype(F32))[:, None]
    zr, zi = lr * step, li * step
    taus = jnp.arange(lc + 1, dtype=F32)[:, None, None]
    mag = jnp.exp(zr[None] * taus)
    pr = mag * jnp.cos(zi[None] * taus)
    pi = mag * jnp.sin(zi[None] * taus)
    nr, ni = pr[1] - 1.0, pi[1]
    den = lr * lr + li * li
    qr = (nr * lr + ni * li) / den
    qi = (ni * lr - nr * li) / den
    b_r = jnp.swapaxes(b_re.astype(F32), 1, 2)
    b_i = jnp.swapaxes(b_im.astype(F32), 1, 2)
    br = qr[:, None, :] * b_r - qi[:, None, :] * b_i
    bi = qr[:, None, :] * b_i + qi[:, None, :] * b_r
    cr, ci = c_re.astype(F32), c_im.astype(F32)

    zero = jnp.zeros_like(pr)
    even = (jnp.arange(g) % 2 == 0)[None, :, None]
    halves = lambda a: jnp.concatenate(
        [jnp.where(even, a, zero), jnp.where(even, zero, a)], axis=-1)
    per_tile = lambda a: jnp.swapaxes(a.reshape(lc + 1, nt, ng, a.shape[-1]), 0, 1)
    grp = lambda a: a.reshape((nt, ng) + a.shape[1:])
    cat = lambda *xs: jnp.concatenate(xs, axis=-1)
    return (per_tile(halves(pr)), per_tile(halves(pi)),
            grp(cat(br, br, bi, bi)), grp(cat(cr, cr, ci, ci)),
            pr[lc].reshape(nt, 1, ng * p), pi[lc].reshape(nt, 1, ng * p))


def _split_bf16(x):
    hi = x.astype(BF16)
    return hi, (x - hi.astype(F32)).astype(BF16)


def _dot_nt_3pass(a, b):
    nt = lambda u, v: lax.dot_general(u, v, (((1,), (1,)), ((), ())), preferred_element_type=F32)
    a_hi, a_lo = _split_bf16(a)
    b_hi, b_lo = _split_bf16(b)
    return nt(a_hi, b_hi) + nt(a_hi, b_lo) + nt(a_lo, b_hi)


def _s5_build(prh_ref, pih_ref, bd_ref, cd_ref, m_ref, ws_ref, wyt_ref, wf_ref, cp_ref, lc, nk):
    @pl.when(pl.program_id(0) == 0)
    def _():
        ws_ref[...] = jnp.zeros_like(ws_ref)
        wyt_ref[...] = jnp.zeros_like(wyt_ref)
        wf_ref[...] = jnp.zeros_like(wf_ref)
        cp_ref[...] = jnp.zeros_like(cp_ref)
        m_ref[...] = jnp.zeros_like(m_ref)

    for a in range(GROUPS_PER_TILE):
        bd, cd = bd_ref[0, a], cd_ref[0, a]
        b_r, b_i = bd[:, :LANES], bd[:, LANES:]
        c_r, c_i = cd[:, :LANES], cd[:, LANES:]
        re_tile = slice((a // 2) * LANES, (a // 2 + 1) * LANES)
        im_tile = slice((nk + a // 2) * LANES, (nk + a // 2 + 1) * LANES)
        grp = slice(a * S5_GROUP, (a + 1) * S5_GROUP)
        p_r, p_i = prh_ref[0, 0, a:a + 1, :], pih_ref[0, 0, a:a + 1, :]
        cp_ref[grp, re_tile] = p_r * c_r - p_i * c_i
        cp_ref[grp, im_tile] = -(p_r * c_i + p_i * c_r)
        for s in range(lc):
            rows = slice(s * LANES + a * S5_GROUP, s * LANES + (a + 1) * S5_GROUP)
            p_r, p_i = prh_ref[0, lc - 1 - s, a:a + 1, :], pih_ref[0, lc - 1 - s, a:a + 1, :]
            w_re, w_im = p_r * b_r - p_i * b_i, p_r * b_i + p_i * b_r
            wf_ref[rows, re_tile] = w_re
            wf_ref[rows, im_tile] = w_im
            ws_ref[rows, re_tile] = w_re.astype(BF16)
            ws_ref[rows, im_tile] = w_im.astype(BF16)
            p_r, p_i = prh_ref[0, s + 1, a:a + 1, :], pih_ref[0, s + 1, a:a + 1, :]
            wyt_ref[rows, re_tile] = (p_r * c_r - p_i * c_i).astype(BF16)
            wyt_ref[rows, im_tile] = (-(p_r * c_i + p_i * c_r)).astype(BF16)
    k_all = _dot_nt_3pass(wf_ref[...], cp_ref[...]).astype(BF16)
    for tau in range(lc):
        kb = k_all[(lc - 1 - tau) * LANES:(lc - tau) * LANES, :]
        for s in range(lc - tau):
            t = s + tau
            m_ref[s * LANES:(s + 1) * LANES, t * LANES:(t + 1) * LANES] = kb


def _s5_body(x_ref, prh_ref, pih_ref, bd_ref, cd_ref, are_ref, aim_ref, d_ref, ca_ref, cb_ref,
             o_ref, ca_out_ref, cb_out_ref,
             m_ref, ws_ref, wyt_ref, wf_ref, cp_ref, lhs_ref, s_ref, xp_ref, stage_ref, carry_ref,
             *, lc, nb, nc, ns):
    nk = ns // LANES
    ca_out_ref[...] = ca_ref[...].astype(ca_out_ref.dtype)
    cb_out_ref[...] = cb_ref[...].astype(cb_out_ref.dtype)

    @pl.when(pl.program_id(1) == 0)
    def _():
        _s5_build(prh_ref, pih_ref, bd_ref, cd_ref, m_ref, ws_ref, wyt_ref, wf_ref, cp_ref, lc, nk)
        carry_ref[...] = jnp.zeros_like(carry_ref)

    for b in range(nb):
        for s in range(lc):
            piece = x_ref[b, pl.ds(s, nc, stride=lc), :]
            lhs_ref[b * nc:(b + 1) * nc, s * LANES:(s + 1) * LANES] = piece.astype(BF16)
    lhs = lhs_ref[...]
    s_all = jnp.dot(lhs, ws_ref[...], preferred_element_type=F32)
    for k in range(2 * nk):
        for b in range(nb):
            s_ref[k, pl.ds(b, nc, stride=nb), :] = (
                s_all[b * nc:(b + 1) * nc, k * LANES:(k + 1) * LANES])

    tile_rows = lambda ref, k: jnp.broadcast_to(ref[0, :, k * LANES:(k + 1) * LANES], (nb, LANES))
    ar = [tile_rows(are_ref, k) for k in range(nk)]
    ai = [tile_rows(aim_ref, k) for k in range(nk)]

    per = 8 // nb

    def step(i, carry):
        r = pl.ds(pl.multiple_of(i * 8, 8), 8)
        s_in = [s_ref[k, r, :] for k in range(2 * nk)]
        before = [[] for _ in range(2 * nk)]
        for u in range(per):
            new = []
            for k in range(nk):
                xr, xi = carry[2 * k], carry[2 * k + 1]
                before[k].append(xr)
                before[nk + k].append(xi)
                rows = slice(u * nb, (u + 1) * nb)
                new.append(ar[k] * xr - ai[k] * xi + s_in[k][rows])
                new.append(ar[k] * xi + ai[k] * xr + s_in[nk + k][rows])
            carry = tuple(new)
        for k in range(2 * nk):
            xp_ref[k, r, :] = jnp.concatenate(before[k], axis=0)
        return carry

    carry = lax.fori_loop(0, nc // per, step,
                          tuple(carry_ref[k, 0:nb, :] for k in range(2 * nk)))
    for k in range(2 * nk):
        carry_ref[k, 0:nb, :] = carry[k]

    xp = jnp.concatenate(
        [jnp.concatenate([xp_ref[k, pl.ds(b, nc, stride=nb), :] for b in range(nb)], axis=0)
         for k in range(2 * nk)], axis=-1)
    tw = 2 * LANES
    y_intra = jnp.concatenate(
        [jnp.dot(lhs[:, :t0 + tw], m_ref[:t0 + tw, t0:t0 + tw], preferred_element_type=F32)
         for t0 in range(0, lc * LANES, tw)], axis=-1)
    y = y_intra + lax.dot_general(xp.astype(BF16), wyt_ref[...], (((1,), (1,)), ((), ())),
                                  preferred_element_type=F32)
    for b in range(nb):
        for t in range(lc):
            stage_ref[b, pl.ds(t, nc, stride=lc), :] = (
                y[b * nc:(b + 1) * nc, t * LANES:(t + 1) * LANES])
    yy = stage_ref[...] + d_ref[...] * x_ref[...]
    z2 = (2.0 * math.sqrt(2.0 / math.pi)) * (yy + 0.044715 * (yy * yy * yy))
    o_ref[...] = (yy * jax.nn.sigmoid(z2)).astype(o_ref.dtype)


def s5_scan(hn, tables, d_skip, cast_a, cast_b, *, tc, lc=S5_CHUNK):
    prh, pih, bd, cd, a_re, a_im = tables
    ng = GROUPS_PER_TILE
    nb, l, d = hn.shape
    nt = d // LANES
    nc = tc // lc
    ns = a_re.shape[-1]
    rows = nb * nc
    body = functools.partial(_s5_body, lc=lc, nb=nb, nc=nc, ns=ns)
    tile4 = lambda j, i: (j, 0, 0, 0)
    tile3 = lambda j, i: (j, 0, 0)
    assert 8 % nb == 0
    steps_i = l // tc
    step_of = lambda j, i: j * steps_i + i
    ca, ca_in, ca_out, ca_shape = _cast_slabs(cast_a, 0, nt * steps_i, step_of)
    cb, cb_in, cb_out, cb_shape = _cast_slabs(cast_b, 0, nt * steps_i, step_of)
    g, ca16, cb16 = pl.pallas_call(
        body,
        grid=(nt, l // tc),
        in_specs=[pl.BlockSpec((nb, tc, LANES), lambda j, i: (0, i, j)),
                  pl.BlockSpec((1, lc + 1, ng, LANES), tile4),
                  pl.BlockSpec((1, lc + 1, ng, LANES), tile4),
                  pl.BlockSpec((1, ng, S5_GROUP, 2 * LANES), tile4),
                  pl.BlockSpec((1, ng, S5_GROUP, 2 * LANES), tile4),
                  pl.BlockSpec((1, 1, ns), tile3),
                  pl.BlockSpec((1, 1, ns), tile3),
                  pl.BlockSpec((1, 1, LANES), lambda j, i: (0, 0, j)),
                  ca_in, cb_in],
        out_specs=[pl.BlockSpec((nb, tc, LANES), lambda j, i: (0, i, j)), ca_out, cb_out],
        out_shape=[jax.ShapeDtypeStruct((nb, l, d), BF16), ca_shape, cb_shape],
        scratch_shapes=[pltpu.VMEM((lc * LANES, lc * LANES), BF16),
                        pltpu.VMEM((lc * LANES, 2 * ns), BF16),
                        pltpu.VMEM((lc * LANES, 2 * ns), BF16),
                        pltpu.VMEM((lc * LANES, 2 * ns), F32),
                        pltpu.VMEM((LANES, 2 * ns), F32),
                        pltpu.VMEM((rows, lc * LANES), BF16),
                        pltpu.VMEM((2 * ns // LANES, rows, LANES), F32),
                        pltpu.VMEM((2 * ns // LANES, rows, LANES), F32),
                        pltpu.VMEM((nb, tc, LANES), F32),
                        pltpu.VMEM((2 * ns // LANES, 8, LANES), F32)],
        compiler_params=_params(2),
        name="s5_scan",
    )(hn, prh, pih, bd, cd, a_re, a_im, d_skip.reshape(1, 1, d).astype(F32), ca, cb)
    return g, ca16.reshape((1,) + cast_a.shape[1:]), cb16.reshape((1,) + cast_b.shape[1:])


def _rope(x, cos, sin_signed, first_half):
    w = x.shape[1]
    reps = w // LANES
    cosw = jnp.tile(cos, (1, reps))
    sinw = jnp.tile(sin_signed, (1, reps))
    sel = jnp.tile(first_half, (1, reps))
    half = HEAD_DIM // 2
    swapped = jnp.where(sel, pltpu.roll(x, w - half, 1), pltpu.roll(x, half, 1))
    return x * cosw + swapped * sinw


def _attn_body(sink_ref, q_ref, kc_ref, kp_ref, vc_ref, vp_ref, o_ref, *, q_per_kv, sub_blocks):
    n = pl.program_id(1)
    blk = ATTN_BLOCK
    k_all = jnp.concatenate([kp_ref[...], kc_ref[...]], axis=0).astype(F32)
    v_all = jnp.concatenate([vp_ref[...], vc_ref[...]], axis=0).astype(F32)
    for u in range(sub_blocks):
        rows = slice(u * blk, (u + 1) * blk)
        no_prev = jnp.where(n > 0, 0.0, -jnp.inf).astype(F32) if u == 0 else jnp.float32(0.0)
        _attn_block(sink_ref, q_ref[rows, :], k_all[u * blk:(u + 2) * blk],
                    v_all[u * blk:(u + 2) * blk], no_prev, o_ref, rows, q_per_kv)


def _attn_block(sink_ref, q, k, v, no_prev, o_ref, rows, q_per_kv):
    blk = ATTN_BLOCK
    from_prev = (lax.broadcasted_iota(jnp.int32, (blk, blk), 1)
                 > lax.broadcasted_iota(jnp.int32, (blk, blk), 0))

    low = lax.broadcasted_iota(jnp.int32, (2 * blk, LANES), 1) < HEAD_DIM

    def split(tile, head_in_low):
        moved = pltpu.roll(tile, HEAD_DIM, 1)
        in_lo, in_hi = (tile, moved) if head_in_low else (moved, tile)
        return (jnp.where(low, in_lo, 0.0).astype(BF16),
                jnp.where(low, 0.0, in_hi).astype(BF16))

    for h in range(N_KV_HEADS):
        tile = slice((h // 2) * LANES, (h // 2 + 1) * LANES)
        k_lo, k_hi = split(k[:, tile], h % 2 == 0)
        v_lo, v_hi = split(v[:, tile], h % 2 == 0)
        for i in range(q_per_kv // 2):
            head = h * q_per_kv + 2 * i
            qt = q[:, (head // 2) * LANES:(head // 2 + 1) * LANES]
            acc = None
            for kk, vv, a in ((k_lo, v_lo, head), (k_hi, v_hi, head + 1)):
                s = lax.dot_general(qt, kk, (((1,), (1,)), ((), ())),
                                    preferred_element_type=F32)
                z = jnp.where(from_prev, s[:, :blk] + no_prev, s[:, blk:])
                sink = sink_ref[a] * LOG2E
                m = jnp.maximum(jnp.max(z, axis=-1, keepdims=True), sink)
                e = jnp.exp2(z - m)
                denom = jnp.sum(e, axis=-1, keepdims=True) + jnp.exp2(sink - m)
                p = jnp.concatenate([jnp.where(from_prev, e, 0.0), jnp.where(from_prev, 0.0, e)],
                                    axis=1).astype(BF16)
                o = jnp.dot(p, vv, preferred_element_type=F32) * (1.0 / denom)
                acc = o if acc is None else acc + o
            o_ref[rows, (head // 2) * LANES:(head // 2 + 1) * LANES] = acc.astype(o_ref.dtype)


def _qkv_rope_body(x_ref, w_ref, b_ref, pos_ref, invf_ref, o_ref, cos_ref, sin_ref,
                   *, q_cols, rope_cols):
    j = pl.program_id(1)
    lane = lax.broadcasted_iota(jnp.int32, cos_ref.shape, 1)
    first_half = (lane % HEAD_DIM) < (HEAD_DIM // 2)

    @pl.when(j == 0)
    def _():
        ang = pos_ref[...].astype(F32) * invf_ref[...]
        sin = jnp.sin(ang)
        cos_ref[...] = jnp.cos(ang)
        sin_ref[...] = jnp.where(first_half, -sin, sin)

    w = w_ref[...].astype(BF16)
    bn = o_ref.shape[1]
    col = j * bn + lax.broadcasted_iota(jnp.int32, (1, bn), 1)
    scale = jnp.where(col < q_cols, HEAD_DIM ** -0.5 * LOG2E, 1.0)
    rotated = col < rope_cols
    chunk = x_ref.shape[0] // ROPE_ROW_CHUNKS
    lane_c = lax.broadcasted_iota(jnp.int32, (chunk, LANES), 1)
    first_half_c = (lane_c % HEAD_DIM) < (HEAD_DIM // 2)
    for r in range(ROPE_ROW_CHUNKS):
        rows = slice(r * chunk, (r + 1) * chunk)
        acc = jnp.dot(x_ref[rows, :], w, preferred_element_type=F32) + b_ref[...]
        rot = _rope(acc, cos_ref[rows, :], sin_ref[rows, :], first_half_c) * scale
        o_ref[rows, :] = jnp.where(rotated, rot, acc).astype(o_ref.dtype)


def qkv_rope_proj(x, w, b, positions, *, layer, bm, bn):
    m, k = x.shape
    n = w.shape[2]
    kvw = N_KV_HEADS * HEAD_DIM
    qw = n - 2 * kvw
    half = HEAD_DIM // 2
    inv_freq = jnp.power(ROPE_THETA, -jnp.arange(half, dtype=F32) / half)
    invf = jnp.tile(inv_freq, LANES // half).reshape(1, LANES)
    pos = positions.reshape(m, 1).astype(jnp.int32)
    body = functools.partial(_qkv_rope_body, q_cols=qw, rope_cols=qw + kvw)
    return pl.pallas_call(
        body,
        grid=(m // bm, n // bn),
        in_specs=[pl.BlockSpec((bm, k), lambda i, j: (i, 0)),
                  _layer_spec(k, bn, layer),
                  _layer_spec(1, bn, layer),
                  pl.BlockSpec((bm, 1), lambda i, j: (i, 0)),
                  pl.BlockSpec((1, LANES), lambda i, j: (0, 0))],
        out_specs=pl.BlockSpec((bm, bn), lambda i, j: (i, j)),
        out_shape=jax.ShapeDtypeStruct((m, n), BF16),
        scratch_shapes=[pltpu.VMEM((bm, LANES), F32), pltpu.VMEM((bm, LANES), F32)],
        compiler_params=_params(2),
        name="qkv_rope_proj",
    )(x, w, b.reshape(b.shape[0], 1, n), pos, invf)


def swa_attention(qkv, sinks, *, batch, seq):
    t, width = qkv.shape
    kvw = N_KV_HEADS * HEAD_DIM
    qw = width - 2 * kvw
    n_q_heads = qw // HEAD_DIM
    q_per_kv = n_q_heads // N_KV_HEADS
    assert q_per_kv % 2 == 0 and 2 * HEAD_DIM == LANES
    blk = ATTN_BLOCK
    sub = ATTN_SUB_BLOCKS
    nblk = seq // blk
    nstep = nblk // sub
    assert nstep * sub == nblk
    cur = lambda b, n: (b * nstep + n, 0)
    prev = lambda col: (lambda b, n: (b * nblk + jnp.maximum(n * sub - 1, 0), col))
    kcol = qw // kvw
    body = functools.partial(_attn_body, q_per_kv=q_per_kv, sub_blocks=sub)
    return pl.pallas_call(
        body,
        grid=(batch, nstep),
        in_specs=[pl.BlockSpec(memory_space=pltpu.SMEM),
                  pl.BlockSpec((sub * blk, qw), cur),
                  pl.BlockSpec((sub * blk, kvw), lambda b, n: (b * nstep + n, kcol)),
                  pl.BlockSpec((blk, kvw), prev(kcol)),
                  pl.BlockSpec((sub * blk, kvw), lambda b, n: (b * nstep + n, kcol + 1)),
                  pl.BlockSpec((blk, kvw), prev(kcol + 1))],
        out_specs=pl.BlockSpec((sub * blk, qw), cur),
        out_shape=jax.ShapeDtypeStruct((t, qw), BF16),
        compiler_params=_params(2),
        name="swa_attention",
    )(sinks.astype(F32), qkv, qkv, qkv, qkv, qkv)


def _ffn(hn, w_gate, w_up, w_down, layer):
    act, w_down_bf16 = swiglu_up(hn, w_gate, w_up, w_down, layer=layer, bm=2048, bn=256)
    return matmul(act, w_down_bf16, layer=0, bm=512, bn=512, name="ffn_down")


def kernel(x, positions, norm_pre_mix, norm_post_mix, norm_pre_ffn, norm_post_ffn, s5_lam_re, s5_lam_im, s5_log_step, s5_b_re, s5_b_im, s5_c_re, s5_c_im, s5_d, s5_w_out1, s5_b_out1, s5_w_out2, s5_b_out2, attn_w_qkv, attn_b_qkv, attn_w_o, attn_b_o, attn_sinks, ffn_w_gate, ffn_w_up, ffn_w_down):
    bsz, seq, d = x.shape
    t = bsz * seq
    h = x.reshape(t, d)

    hn = rmsnorm(h, norm_pre_mix[0], F32)
    tables = _s5_tables(s5_lam_re[0], s5_lam_im[0], s5_log_step[0], s5_b_re[0], s5_b_im[0],
                        s5_c_re[0], s5_c_im[0], S5_CHUNK)
    g, w_qkv, w_o = s5_scan(hn.reshape(bsz, seq, d), tables, s5_d[0], attn_w_qkv, attn_w_o,
                            tc=2048)
    g = g.reshape(t, d)
    m = glu_matmul(g, s5_w_out1, s5_b_out1, s5_w_out2, s5_b_out2, layer=0, bm=1024, bn=512)
    hn = mix_norm(h, m, norm_post_mix[0], norm_pre_ffn[0])
    f = _ffn(hn, ffn_w_gate, ffn_w_up, ffn_w_down, 0)
    h, hn = block_resid_norm(h, m, f, norm_post_mix[0], norm_post_ffn[0], norm_pre_mix[1])

    qkv = qkv_rope_proj(hn, w_qkv, attn_b_qkv, positions, layer=0, bm=1024, bn=1024)
    att = swa_attention(qkv, attn_sinks[0], batch=bsz, seq=seq)
    m = matmul(att, w_o, attn_b_o, layer=0, bm=1024, bn=1024, name="o_proj")
    hn = mix_norm(h, m, norm_post_mix[1], norm_pre_ffn[1])
    f = _ffn(hn, ffn_w_gate, ffn_w_up, ffn_w_down, 1)
    h = block_resid(h, m, f, norm_post_mix[1], norm_post_ffn[1])
    return h.reshape(bsz, seq, d)
```
